```python
import math
import jax
import jax.numpy as jnp
from jax import lax
import numpy as np

D_MODEL = 1024
BATCH = 8
SEQ = 2048
DEPTH = 1
DEC_BATCH = 32
DEC_SEQ = 1
PAST_LEN = 8192
PAGE_SIZE = 128

HEAD_DIM = 64
NSA_HEADS = 8
NSA_GROUPS = 2
NSA_HPG = NSA_HEADS // NSA_GROUPS
CMP_BLK = 32
CMP_STRIDE = 16
SEL_BLK = 64
N_SEL = 16
WINDOW = 512
MOBA_HEADS = 8
MOBA_BLK = 256
MOBA_TOPK = 3
N_BUCKETS = 32
MAX_DISTANCE = 1024
PEER_HEADS = 8
PEER_N_KEYS = 128
PEER_HALF = 128
PEER_TOPK = 16
N_EXPERTS = PEER_N_KEYS * PEER_N_KEYS
NSA_QB = 64
MOBA_QB = 16
PEER_BLK = 256
DN_ALPHA = (2 * DEPTH) ** 0.25
DN_BETA = (8 * DEPTH) ** -0.25
LN_EPS = 1e-5
NEG = -1e30
SPLIT_SIZES = (NSA_HEADS * HEAD_DIM, 4 * NSA_GROUPS * HEAD_DIM, 2 * NSA_GROUPS * HEAD_DIM,
               3 * NSA_HEADS, MOBA_HEADS * HEAD_DIM, 2 * MOBA_HEADS * HEAD_DIM, 2 * D_MODEL)
N_IN = sum(SPLIT_SIZES)

kernel_name = 'hybrid_nsa_moba_peer_step'


def t5_bucket(dist):
    n = jnp.maximum(dist, 0)
    max_exact = N_BUCKETS // 2
    nf = jnp.maximum(n, 1).astype(jnp.float32)
    large = max_exact + (jnp.log(nf / max_exact) / math.log(MAX_DISTANCE / max_exact)
                         * (N_BUCKETS - max_exact)).astype(jnp.int32)
    return jnp.where(n < max_exact, n, jnp.minimum(large, N_BUCKETS - 1)).astype(jnp.int32)


def masked_softmax(s, mask):
    return jax.nn.softmax(jnp.where(mask, s, NEG), axis=-1)


def layer_norm(x, g, b):
    xf = x.astype(jnp.float32)
    mu = xf.mean(-1, keepdims=True)
    var = jnp.square(xf - mu).mean(-1, keepdims=True)
    y = (xf - mu) * lax.rsqrt(var + LN_EPS) * g.astype(jnp.float32) + b.astype(jnp.float32)
    return y.astype(x.dtype)


def query_block(n, qb_max):
    return qb_max if n % qb_max == 0 else n


def gather_blocks(blocks, idx):
    return jax.vmap(jax.vmap(lambda bl, i: bl[i]))(blocks, idx)


def head_bias(table, bucket):
    return jax.vmap(lambda tab, bk: tab[bk], in_axes=(1, 1), out_axes=1)(table, bucket)


def nsa_attention(q, gate_logits, kv_full, win_ctx, q_pos0, win_pos0, cmp_pe, cmp_w, bias_tab):
    B, Tq = q.shape[:2]
    L = kv_full.shape[1]
    dt = q.dtype
    scale = HEAD_DIM ** -0.5
    qg = q.reshape(B, Tq, NSA_GROUPS, NSA_HPG, HEAD_DIM)
    tpos = q_pos0 + jnp.arange(Tq, dtype=jnp.int32)

    r = CMP_BLK // CMP_STRIDE
    n_full = L // CMP_STRIDE
    n_cmp = n_full - r + 1
    seg = kv_full[:, :n_full * CMP_STRIDE, :2].reshape(B, n_full, CMP_STRIDE, 2, NSA_GROUPS, HEAD_DIM)
    pe = cmp_pe.reshape(2, r, CMP_STRIDE, HEAD_DIM)
    w = cmp_w.reshape(2, r, CMP_STRIDE, HEAD_DIM, HEAD_DIM)
    comp = jnp.einsum('bnscgd,csde->bncge',
                      seg[:, 0:n_cmp] + jnp.swapaxes(pe[:, 0], 0, 1)[:, :, None, :], w[:, 0])
    for m in range(1, r):
        comp = comp + jnp.einsum('bnscgd,csde->bncge',
                                 seg[:, m:m + n_cmp] + jnp.swapaxes(pe[:, m], 0, 1)[:, :, None, :], w[:, m])
    kc, vc = comp[:, :, 0], comp[:, :, 1]
    end_pos = jnp.arange(n_cmp, dtype=jnp.int32) * CMP_STRIDE + (CMP_BLK - 1)
    dist_c = tpos[:, None] - end_pos[None, :]
    mask_c = dist_c >= 0
    bias_c = jnp.transpose(bias_tab[t5_bucket(dist_c)], (2, 3, 0, 1))
    s_c = jnp.einsum('btghd,bngd->bghtn', qg, kc).astype(jnp.float32) * scale + bias_c
    p_c = masked_softmax(s_c, mask_c) * mask_c
    o_cmp = jnp.einsum('bghtn,bngd->btghd', p_c.astype(dt), vc)

    imp = p_c.sum(axis=2)
    n_seg = -(-L // CMP_STRIDE)
    seg_score = jnp.pad(imp, ((0, 0), (0, 0), (0, 0), (0, n_seg - n_cmp)))
    for m in range(1, r):
        seg_score = seg_score + jnp.pad(imp, ((0, 0), (0, 0), (0, 0), (m, n_seg - n_cmp - m)))
    spb = SEL_BLK // CMP_STRIDE
    n_sel = -(-L // SEL_BLK)
    blk_score = jnp.pad(seg_score, ((0, 0), (0, 0), (0, 0), (0, n_sel * spb - n_seg)))
    blk_score = blk_score.reshape(B, NSA_GROUPS, Tq, n_sel, spb).sum(-1)
    jblk = jnp.arange(n_sel, dtype=jnp.int32)[None, :]
    cur = (tpos // SEL_BLK)[:, None]
    valid = jblk <= cur
    forced = valid & ((jblk == 0) | (jblk == cur) | (jblk == cur - 1))
    score = jnp.where(forced, jnp.inf, jnp.where(valid, blk_score, -jnp.inf))
    top_s, sel_idx = lax.top_k(score, min(N_SEL, n_sel))
    sel_ok = top_s > -jnp.inf
    n_take = sel_idx.shape[-1]

    kv_s = jnp.pad(kv_full[:, :, 2:], ((0, 0), (0, n_sel * SEL_BLK - L), (0, 0), (0, 0), (0, 0)))
    kv_s = kv_s.reshape(B, n_sel, SEL_BLK, 2, NSA_GROUPS, HEAD_DIM).transpose(0, 4, 1, 2, 3, 5)
    win_pad = jnp.pad(win_ctx, ((0, 0), (WINDOW, 0), (0, 0), (0, 0), (0, 0)))

    qb = query_block(Tq, NSA_QB)
    nqb = Tq // qb
    q_blocks = jnp.moveaxis(qg.reshape(B, nqb, qb, NSA_GROUPS, NSA_HPG, HEAD_DIM), 1, 0)
    idx_blocks = jnp.moveaxis(sel_idx.reshape(B, NSA_GROUPS, nqb, qb, n_take), 2, 0)
    ok_blocks = jnp.moveaxis(sel_ok.reshape(B, NSA_GROUPS, nqb, qb, n_take), 2, 0)
    pos_blocks = tpos.reshape(nqb, qb)
    arange_sel = jnp.arange(SEL_BLK, dtype=jnp.int32)
    arange_win = jnp.arange(WINDOW + qb, dtype=jnp.int32)

    def one_block(args):
        qblk, idx, ok, tp = args
        kvg = gather_blocks(kv_s, idx).reshape(B, NSA_GROUPS, qb, n_take * SEL_BLK, 2, HEAD_DIM)
        kpos = (idx[..., None] * SEL_BLK + arange_sel).reshape(B, NSA_GROUPS, qb, n_take * SEL_BLK)
        dist = tp[None, None, :, None] - kpos
        msk = jnp.repeat(ok, SEL_BLK, axis=-1) & (dist >= 0)
        bias = jnp.moveaxis(head_bias(bias_tab, t5_bucket(dist)), -1, 2)
        s = jnp.einsum('bqghd,bgqkd->bghqk', qblk, kvg[..., 0, :]).astype(jnp.float32) * scale + bias
        p = masked_softmax(s, msk[:, :, None])
        o_sel = jnp.einsum('bghqk,bgqkd->bqghd', p.astype(dt), kvg[..., 1, :])
        wkv = lax.dynamic_slice_in_dim(win_pad, tp[0] - win_pos0, WINDOW + qb, axis=1)
        wpos = tp[0] - WINDOW + arange_win
        wd = tp[:, None] - wpos[None, :]
        wm = (wpos[None, :] >= 0) & (wd >= 0) & (wd <= WINDOW)
        wbias = jnp.transpose(bias_tab[t5_bucket(wd)], (2, 3, 0, 1))
        s = jnp.einsum('bqghd,bkgd->bghqk', qblk, wkv[:, :, 0]).astype(jnp.float32) * scale + wbias
        p = masked_softmax(s, wm)
        o_win = jnp.einsum('bghqk,bkgd->bqghd', p.astype(dt), wkv[:, :, 1])
        return o_sel, o_win

    o_sel, o_win = lax.map(one_block, (q_blocks, idx_blocks, ok_blocks, pos_blocks))
    o_sel = jnp.moveaxis(o_sel, 0, 1).reshape(B, Tq, NSA_GROUPS, NSA_HPG, HEAD_DIM)
    o_win = jnp.moveaxis(o_win, 0, 1).reshape(B, Tq, NSA_GROUPS, NSA_HPG, HEAD_DIM)
    g = jax.nn.sigmoid(gate_logits.astype(jnp.float32)).astype(dt).reshape(B, Tq, NSA_GROUPS, NSA_HPG, 3)
    o = g[..., 0:1] * o_cmp + g[..., 1:2] * o_sel + g[..., 2:3] * o_win
    return o.reshape(B, Tq, NSA_HEADS * HEAD_DIM)


def moba_attention(q, kv_full, q_pos0, bias_tab):
    B, Tq = q.shape[:2]
    L = kv_full.shape[1]
    dt = q.dtype
    scale = HEAD_DIM ** -0.5
    tpos = q_pos0 + jnp.arange(Tq, dtype=jnp.int32)
    nb = -(-L // MOBA_BLK)
    kvp = jnp.pad(kv_full, ((0, 0), (0, nb * MOBA_BLK - L), (0, 0), (0, 0), (0, 0)))
    blocks = kvp.reshape(B, nb, MOBA_BLK, 2, MOBA_HEADS, HEAD_DIM).transpose(0, 4, 1, 2, 3, 5)
    kmean = blocks[..., 0, :].astype(jnp.float32).mean(axis=3)
    own = tpos // MOBA_BLK
    gate = jnp.einsum('bthd,bhnd->bhtn', q.astype(jnp.float32), kmean)
    past = jnp.arange(nb, dtype=jnp.int32)[None, :] < own[:, None]
    top_s, top_i = lax.top_k(jnp.where(past, gate, -jnp.inf), min(MOBA_TOPK, nb))
    own_b = jnp.broadcast_to(own[None, None, :, None], (B, MOBA_HEADS, Tq, 1)).astype(top_i.dtype)
    idx = jnp.concatenate([top_i, own_b], axis=-1)
    ok = jnp.concatenate([top_s > -jnp.inf, jnp.ones((B, MOBA_HEADS, Tq, 1), dtype=bool)], axis=-1)
    n_take = idx.shape[-1]
    qb = query_block(Tq, MOBA_QB)
    nqb = Tq // qb
    q_blocks = jnp.moveaxis(q.reshape(B, nqb, qb, MOBA_HEADS, HEAD_DIM), 1, 0)
    idx_blocks = jnp.moveaxis(idx.reshape(B, MOBA_HEADS, nqb, qb, n_take), 2, 0)
    ok_blocks = jnp.moveaxis(ok.reshape(B, MOBA_HEADS, nqb, qb, n_take), 2, 0)
    pos_blocks = tpos.reshape(nqb, qb)
    arange_blk = jnp.arange(MOBA_BLK, dtype=jnp.int32)

    def one_block(args):
        qblk, ib, okb, tp = args
        kvg = gather_blocks(blocks, ib).reshape(B, MOBA_HEADS, qb, n_take * MOBA_BLK, 2, HEAD_DIM)
        kpos = (ib[..., None] * MOBA_BLK + arange_blk).reshape(B, MOBA_HEADS, qb, n_take * MOBA_BLK)
        dist = tp[None, None, :, None] - kpos
        msk = jnp.repeat(okb, MOBA_BLK, axis=-1) & (dist >= 0)
        s = (jnp.einsum('bqhd,bhqkd->bhqk', qblk, kvg[..., 0, :]).astype(jnp.float32) * scale
             + head_bias(bias_tab, t5_bucket(dist)))
        p = masked_softmax(s, msk)
        return jnp.einsum('bhqk,bhqkd->bqhd', p.astype(dt), kvg[..., 1, :])

    o = lax.map(one_block, (q_blocks, idx_blocks, ok_blocks, pos_blocks))
    return jnp.moveaxis(o, 0, 1).reshape(B, Tq, MOBA_HEADS * HEAD_DIM)


def peer_ffn(x, wq, keys, u_tab, v_tab):
    n, d = x.shape
    blk = min(PEER_BLK, n)
    nblk = -(-n // blk)
    xp = jnp.pad(x, ((0, nblk * blk - n), (0, 0))).reshape(nblk, blk, d)

    def one_block(xb):
        q = jnp.einsum('nd,de->ne', xb, wq).reshape(blk, PEER_HEADS, 2, PEER_HALF)
        s = jnp.einsum('nhcd,hckd->nhck', q, keys).astype(jnp.float32)
        sv, si = lax.top_k(s, PEER_TOPK)
        cand = (sv[:, :, 0, :, None] + sv[:, :, 1, None, :]).reshape(blk, PEER_HEADS, PEER_TOPK * PEER_TOPK)
        cidx = (si[:, :, 0, :, None] * PEER_N_KEYS + si[:, :, 1, None, :]).reshape(blk, PEER_HEADS, PEER_TOPK * PEER_TOPK)
        fs, fpos = lax.top_k(cand, PEER_TOPK)
        eidx = jnp.take_along_axis(cidx, fpos, axis=-1)
        g = jax.nn.softmax(fs, axis=-1)
        act = jax.nn.gelu(jnp.einsum('nd,nhkd->nhk', xb, u_tab[eidx]).astype(jnp.float32), approximate=False)
        return jnp.einsum('nhk,nhkd->nd', (g * act).astype(xb.dtype), v_tab[eidx])

    return lax.map(one_block, xp).reshape(nblk * blk, d)[:n]


def decoder_layer(x, q_pos0, nsa_past, moba_past, win_past, win_keep, w_in, cmp_pe, cmp_w,
                  w_branch_a, w_branch_b, w_out, ln1_g, ln1_b, peer_wq, peer_keys, peer_u, peer_v,
                  ln2_g, ln2_b, rel_bias):
    B, T, _ = x.shape
    proj = jnp.einsum('btd,dn->btn', x, w_in)
    cuts = []
    acc = 0
    for size in SPLIT_SIZES[:-1]:
        acc += size
        cuts.append(acc)
    q_a, kv_nsa, kv_win, gate_a, q_b, kv_b, g_merge = jnp.split(proj, cuts, axis=-1)
    q_a = q_a.reshape(B, T, NSA_HEADS, HEAD_DIM)
    kv_nsa = kv_nsa.reshape(B, T, 4, NSA_GROUPS, HEAD_DIM)
    kv_win = kv_win.reshape(B, T, 2, NSA_GROUPS, HEAD_DIM)
    gate_a = gate_a.reshape(B, T, NSA_HEADS, 3)
    q_b = q_b.reshape(B, T, MOBA_HEADS, HEAD_DIM)
    kv_b = kv_b.reshape(B, T, 2, MOBA_HEADS, HEAD_DIM)
    g_merge = jax.nn.sigmoid(g_merge.astype(jnp.float32)).astype(x.dtype).reshape(B, T, 2, D_MODEL)

    nsa_full = jnp.concatenate([nsa_past, kv_nsa], axis=1)
    moba_full = jnp.concatenate([moba_past, kv_b], axis=1)
    win_ctx = jnp.concatenate([win_past, kv_win], axis=1)
    win_pos0 = q_pos0 - win_past.shape[1]
    bias_a = rel_bias[:, :NSA_HEADS].reshape(N_BUCKETS, NSA_GROUPS, NSA_HPG)
    bias_b = rel_bias[:, NSA_HEADS:]

    o_a = nsa_attention(q_a, gate_a, nsa_full, win_ctx, q_pos0, win_pos0, cmp_pe, cmp_w, bias_a)
    o_b = moba_attention(q_b, moba_full, q_pos0, bias_b)
    y_a = jnp.einsum('btn,nd->btd', o_a, w_branch_a)
    y_b = jnp.einsum('btn,nd->btd', o_b, w_branch_b)
    mix = jnp.einsum('btd,de->bte', g_merge[:, :, 0] * y_a + g_merge[:, :, 1] * y_b, w_out)
    h = layer_norm(DN_ALPHA * x + mix, ln1_g, ln1_b)
    f = peer_ffn(h.reshape(B * T, D_MODEL), peer_wq, peer_keys, peer_u, peer_v).reshape(B, T, D_MODEL)
    y = layer_norm(DN_ALPHA * h + f, ln2_g, ln2_b)
    win_new = win_ctx[:, win_ctx.shape[1] - win_keep:]
    return y, kv_nsa, kv_b, win_new


def setup_inputs(seed: int = 0) -> dict:
    key = jax.random.key(seed)
    ks = jax.random.split(key, 24)
    f32 = jnp.float32
    n_pages = PAST_LEN // PAGE_SIZE
    n_pool = (DEC_BATCH * n_pages * 5) // 4
    win_len = min(WINDOW, PAST_LEN)

    def nrm(k, shape, scale):
        return jax.random.normal(k, shape, f32) * scale

    page_table = jax.random.permutation(ks[5], n_pool)[:DEC_BATCH * n_pages].reshape(DEC_BATCH, n_pages).astype(jnp.int32)
    return {
        'x_prompt': nrm(ks[0], (BATCH, SEQ, D_MODEL), 1.0),
        'x_sample': nrm(ks[1], (DEC_BATCH, DEC_SEQ, D_MODEL), 1.0),
        'cache_nsa': nrm(ks[2], (DEPTH, n_pool, PAGE_SIZE, 4, NSA_GROUPS, HEAD_DIM), 1.0),
        'cache_moba': nrm(ks[3], (DEPTH, n_pool, PAGE_SIZE, 2, MOBA_HEADS, HEAD_DIM), 1.0),
        'state_nsa_win': nrm(ks[4], (DEPTH, DEC_BATCH, win_len, 2, NSA_GROUPS, HEAD_DIM), 1.0),
        'page_table': page_table,
        'w_in': nrm(ks[6], (DEPTH, D_MODEL, N_IN), D_MODEL ** -0.5),
        'nsa_cmp_pe': nrm(ks[7], (DEPTH, 2, CMP_BLK, HEAD_DIM), 0.02),
        'nsa_cmp_w': nrm(ks[8], (DEPTH, 2, CMP_BLK, HEAD_DIM, HEAD_DIM), (CMP_BLK * HEAD_DIM) ** -0.5),
        'w_branch_a': nrm(ks[9], (DEPTH, NSA_HEADS * HEAD_DIM, D_MODEL), (NSA_HEADS * HEAD_DIM) ** -0.5),
        'w_branch_b': nrm(ks[10], (DEPTH, MOBA_HEADS * HEAD_DIM, D_MODEL), (MOBA_HEADS * HEAD_DIM) ** -0.5),
        'w_out': nrm(ks[11], (DEPTH, D_MODEL, D_MODEL), DN_BETA * D_MODEL ** -0.5),
        'ln1_g': 1.0 + nrm(ks[12], (DEPTH, D_MODEL), 0.02),
        'ln1_b': nrm(ks[13], (DEPTH, D_MODEL), 0.02),
        'peer_wq': nrm(ks[14], (DEPTH, D_MODEL, PEER_HEADS * 2 * PEER_HALF), D_MODEL ** -0.5),
        'peer_keys': nrm(ks[15], (DEPTH, PEER_HEADS, 2, PEER_N_KEYS, PEER_HALF), PEER_HALF ** -0.5),
        'peer_u': nrm(ks[16], (DEPTH, N_EXPERTS, D_MODEL), D_MODEL ** -0.5),
        'peer_v': nrm(ks[17], (DEPTH, N_EXPERTS, D_MODEL), DN_BETA),
        'ln2_g': 1.0 + nrm(ks[18], (DEPTH, D_MODEL), 0.02),
        'ln2_b': nrm(ks[19], (DEPTH, D_MODEL), 0.02),
        'rel_bias': nrm(ks[20], (N_BUCKETS, NSA_HEADS + MOBA_HEADS), 0.5),
    }


def reference(x_prompt, x_sample, cache_nsa, cache_moba, state_nsa_win, page_table, w_in, nsa_cmp_pe,
              nsa_cmp_w, w_branch_a, w_branch_b, w_out, ln1_g, ln1_b, peer_wq, peer_keys, peer_u, peer_v,
              ln2_g, ln2_b, rel_bias):
    bp, t_prompt = x_prompt.shape[:2]
    bs = x_sample.shape[0]
    past_len = page_table.shape[1] * cache_nsa.shape[2]
    win_len = state_nsa_win.shape[2]
    xp, xs = x_prompt, x_sample
    p_nsa, p_moba, p_win, s_nsa, s_moba, s_win = [], [], [], [], [], []
    for l in range(DEPTH):
        params = (w_in[l], nsa_cmp_pe[l], nsa_cmp_w[l], w_branch_a[l], w_branch_b[l], w_out[l],
                  ln1_g[l], ln1_b[l], peer_wq[l], peer_keys[l], peer_u[l], peer_v[l], ln2_g[l], ln2_b[l],
                  rel_bias)
        xp, a, b, c = decoder_layer(
            xp, 0,
            jnp.zeros((bp, 0, 4, NSA_GROUPS, HEAD_DIM), xp.dtype),
            jnp.zeros((bp, 0, 2, MOBA_HEADS, HEAD_DIM), xp.dtype),
            jnp.zeros((bp, 0, 2, NSA_GROUPS, HEAD_DIM), xp.dtype),
            min(WINDOW, t_prompt), *params)
        p_nsa.append(a)
        p_moba.append(b)
        p_win.append(c)
        nsa_past = cache_nsa[l][page_table].reshape(bs, past_len, 4, NSA_GROUPS, HEAD_DIM)
        moba_past = cache_moba[l][page_table].reshape(bs, past_len, 2, MOBA_HEADS, HEAD_DIM)
        xs, a, b, c = decoder_layer(xs, past_len, nsa_past, moba_past, state_nsa_win[l], win_len, *params)
        s_nsa.append(a)
        s_moba.append(b)
        s_win.append(c)
    return (xp, xs, jnp.stack(p_nsa), jnp.stack(p_moba), jnp.stack(p_win),
            jnp.stack(s_nsa), jnp.stack(s_moba), jnp.stack(s_win))
```

```python
import functools
import math

import jax
import jax.numpy as jnp
from jax import lax
from jax.experimental import pallas as pl
from jax.experimental.pallas import tpu as pltpu

HEAD_DIM = 64
NSA_HEADS = 8
NSA_GROUPS = 2
NSA_HPG = NSA_HEADS // NSA_GROUPS
CMP_BLK = 32
CMP_STRIDE = 16
SEL_BLK = 64
N_SEL = 16
WINDOW = 512
MOBA_HEADS = 8
MOBA_BLK = 256
MOBA_TOPK = 3
N_BUCKETS = 32
MAX_DISTANCE = 1024
PEER_HEADS = 8
PEER_N_KEYS = 128
PEER_HALF = 128
PEER_TOPK = 16
LN_EPS = 1e-5
NEG = -1e30

LANES = 128
SUBLANES = 8
VMEM_BYTES_V7X = 64 * 1024 * 1024

TQ = 128
PAGE = 128

f32 = jnp.float32
i32 = jnp.int32
_MXU = jnp.bfloat16
_HI = lax.Precision.HIGHEST


def _iota(shape, dim):
    return lax.broadcasted_iota(i32, shape, dim)


def _dot(a, b):
    return jnp.dot(a.astype(_MXU), b.astype(_MXU), preferred_element_type=f32)


def _dot_nt(a, b):
    return lax.dot_general(a.astype(_MXU), b.astype(_MXU), (((1,), (1,)), ((), ())),
                           preferred_element_type=f32)


def _dot_hi(a, b):
    return jnp.dot(a, b, precision=_HI, preferred_element_type=f32)


def _dot_nt_hi(a, b):
    return lax.dot_general(a, b, (((1,), (1,)), ((), ())), precision=_HI, preferred_element_type=f32)


def _vmem(nbytes):
    return pltpu.CompilerParams(vmem_limit_bytes=int(min(nbytes, VMEM_BYTES_V7X - (4 << 20))))


def _rank_lanes(s, n):
    width = s.shape[1]
    j = _iota(s.shape, 1)
    rank = jnp.zeros(s.shape, i32)
    for r in range(1, n):
        lo = pltpu.roll(s, r, 1)
        rank = rank + jnp.where((j >= r) & (lo >= s), 1, 0)
        hi = pltpu.roll(s, width - r, 1)
        rank = rank + jnp.where((j + r < n) & (hi > s), 1, 0)
    return rank


def _softmax_step(s, mask, v_fn, m_s, l_s, acc_s):
    s = jnp.where(mask, s, NEG)
    m_old = m_s[...]
    m_new = jnp.maximum(m_old, jnp.max(s, axis=1, keepdims=True))
    alpha = jnp.exp(m_old - m_new)
    p = jnp.where(mask, jnp.exp(s - m_new), 0.0)
    l_s[...] = alpha * l_s[...] + jnp.sum(p, axis=1, keepdims=True)
    acc_s[...] = alpha * acc_s[...] + v_fn(p)
    m_s[...] = m_new


def _softmax_reset(m_s, l_s, acc_s):
    m_s[...] = jnp.full(m_s.shape, NEG, f32)
    l_s[...] = jnp.zeros(l_s.shape, f32)
    acc_s[...] = jnp.zeros(acc_s.shape, f32)


_W_QA = NSA_HEADS * HEAD_DIM
_W_KVN = 4 * NSA_GROUPS * HEAD_DIM
_W_KVW = 2 * NSA_GROUPS * HEAD_DIM
_W_GA = 3 * NSA_HEADS
_W_QB = MOBA_HEADS * HEAD_DIM
_W_KVB = 2 * MOBA_HEADS * HEAD_DIM


def _proj_body(x_ref, w_ref, qa_ref, kvn_ref, kvw_ref, qb_ref, kvb_ref, gm_ref, ga_ref):
    xb = x_ref[...].astype(_MXU)
    col = 0
    for ref, act in ((qa_ref, None), (kvn_ref, None), (kvw_ref, None), (qb_ref, None), (kvb_ref, None),
                     (gm_ref, jax.nn.sigmoid), (ga_ref, jax.nn.sigmoid)):
        width = ref.shape[1]
        y = jnp.dot(xb, w_ref[:, col:col + width], preferred_element_type=f32)
        ref[...] = y if act is None else act(y)
        col += width


def _project(x2d, w_packed, d_model):
    n = x2d.shape[0]
    tm = 256 if n % 256 == 0 else n
    widths = (_W_QA, _W_KVN, _W_KVW, _W_QB, _W_KVB, 2 * d_model, LANES)
    n_cols = sum(widths)
    return pl.pallas_call(
        _proj_body,
        grid=(n // tm,),
        in_specs=[pl.BlockSpec((tm, d_model), lambda i: (i, 0)),
                  pl.BlockSpec((d_model, n_cols), lambda i: (0, 0))],
        out_specs=[pl.BlockSpec((tm, w), lambda i: (i, 0)) for w in widths],
        out_shape=[jax.ShapeDtypeStruct((n, w), f32) for w in widths],
        compiler_params=_vmem(2 * (tm * d_model * 4 + d_model * n_cols * 2 + tm * n_cols * 4) + (8 << 20)),
        name="proj",
    )(x2d, w_packed)


def _pack_w_in(w_in, d_model):
    c_ga = _W_QA + _W_KVN + _W_KVW
    c_qb = c_ga + _W_GA
    pad = jnp.zeros((d_model, LANES - _W_GA), w_in.dtype)
    return jnp.concatenate([w_in[:, :c_ga], w_in[:, c_qb:], w_in[:, c_ga:c_qb], pad], axis=1).astype(_MXU)


def _t5_bucket(dist):
    n = jnp.maximum(dist, 0)
    max_exact = N_BUCKETS // 2
    nf = jnp.maximum(n, 1).astype(f32)
    large = max_exact + (jnp.log(nf / max_exact) / math.log(MAX_DISTANCE / max_exact)
                         * (N_BUCKETS - max_exact)).astype(i32)
    return jnp.where(n < max_exact, n, jnp.minimum(large, N_BUCKETS - 1)).astype(i32)


def _bias_by_distance(rel_bias, max_dist):
    return rel_bias[_t5_bucket(jnp.arange(max_dist, dtype=i32))].T


def _bias_tiles(tab, n_tiles):
    k = jnp.arange(n_tiles, dtype=i32)[:, None, None]
    r = jnp.arange(TQ, dtype=i32)[None, :, None]
    c = jnp.arange(TQ, dtype=i32)[None, None, :]
    idx = jnp.maximum(k * TQ + r - c, 0)
    t = tab[:, idx]
    return jnp.transpose(t, (1, 0, 2, 3)).reshape(n_tiles, tab.shape[0] * TQ, TQ)


def _pack_cmp(cmp_pe, cmp_w):
    r = CMP_BLK // CMP_STRIDE
    w = cmp_w.reshape(2, r, CMP_STRIDE, HEAD_DIM, HEAD_DIM)
    eye = jnp.eye(NSA_GROUPS, dtype=w.dtype)
    wbig = jnp.einsum("cmrde,gh->cmrgdhe", w, eye).reshape(2, r, CMP_STRIDE * NSA_GROUPS * HEAD_DIM,
                                                          NSA_GROUPS * HEAD_DIM)
    pe = cmp_pe.reshape(2, r, CMP_STRIDE, 1, HEAD_DIM)
    pe = jnp.broadcast_to(pe, (2, r, CMP_STRIDE, NSA_GROUPS, HEAD_DIM)).reshape(2, r, 1, -1)
    return wbig.astype(_MXU), pe.astype(f32)


def _cmp_to_sel_matrix(n_rows, n_cols):
    spb = SEL_BLK // CMP_STRIDE
    n = jnp.arange(n_rows, dtype=i32)[:, None]
    j = jnp.arange(n_cols, dtype=i32)[None, :]
    m = jnp.zeros((n_rows, n_cols), f32)
    for k in range(CMP_BLK // CMP_STRIDE):
        m = m + ((n + k) // spb == j).astype(f32)
    return m


def _compress(xflat_ref, c, pe_ref, wbig_ref):
    x = xflat_ref[c]
    a = _dot(x + pe_ref[c, 0], wbig_ref[c, 0])
    b = _dot(x + pe_ref[c, 1], wbig_ref[c, 1])
    return a + pltpu.roll(b, x.shape[0] - 1, 0)


def _stack_heads_nsa(q, q8_s, tq):
    half = _iota((tq, LANES), 1) >> 6
    for h in range(NSA_HEADS):
        g = h // NSA_HPG
        blk = q[:, (h // 2) * LANES:(h // 2 + 1) * LANES]
        if h % 2 != g:
            blk = pltpu.roll(blk, HEAD_DIM, 1)
        q8_s[h * tq:(h + 1) * tq, :] = jnp.where(half == g, blk, 0.0).astype(q8_s.dtype)


def _nsa_prompt_body(qa_ref, kvn_ref, kvw_ref, ga_ref, bc_ref, bt_ref, wbig_ref, pe_ref, msel_ref, o_ref,
                     rows_s, xflat_s, kc_s, vc_s, q8_s, sel8_s, ocmp_s, osel_s, m_s, l_s, acc_s, *, seq):
    qt = pl.program_id(1)
    t0 = qt * TQ
    n_seg = seq // CMP_STRIDE
    n_cmp = n_seg - CMP_BLK // CMP_STRIDE + 1
    n_sel = -(-seq // SEL_BLK)
    nc = kc_s.shape[0]
    rows = NSA_HEADS * TQ
    scale = HEAD_DIM ** -0.5

    @pl.when(qt == 0)
    def _():
        for c in range(2):
            if nc > n_seg:
                xflat_s[c] = jnp.zeros(xflat_s.shape[1:], f32)
            rows_s[...] = kvn_ref[0, :, c * LANES:(c + 1) * LANES]
            for r in range(CMP_STRIDE):
                xflat_s[c, 0:n_seg, r * LANES:(r + 1) * LANES] = rows_s[pl.ds(r, n_seg, stride=CMP_STRIDE), :]
        kc_s[...] = _compress(xflat_s, 0, pe_ref, wbig_ref)
        vc_s[...] = _compress(xflat_s, 1, pe_ref, wbig_ref)

    _stack_heads_nsa(qa_ref[...] * scale, q8_s, TQ)
    q8 = q8_s[...]
    tpos = t0 + (_iota((rows, 1), 0) & (TQ - 1))

    ncol = _iota((rows, nc), 1)
    mask_c = (ncol < n_cmp) & (tpos >= ncol * CMP_STRIDE + (CMP_BLK - 1))
    sc = _dot_nt(q8, kc_s[...]) + bc_ref[...].reshape(rows, nc)
    sc = jnp.where(mask_c, sc, NEG)
    e = jnp.exp(sc - jnp.max(sc, axis=1, keepdims=True))
    p_c = jnp.where(mask_c, e / jnp.sum(e, axis=1, keepdims=True), 0.0)
    ocmp_s[...] = _dot(p_c, vc_s[...])

    jb = _iota((TQ, LANES), 1)
    cur = (t0 + _iota((TQ, LANES), 0)) >> 6
    valid = (jb <= cur) & (jb < n_sel)
    forced = valid & ((jb == 0) | (jb == cur) | (jb == cur - 1))
    for g in range(NSA_GROUPS):
        imp = p_c[g * NSA_HPG * TQ:(g * NSA_HPG + 1) * TQ]
        for h in range(1, NSA_HPG):
            imp = imp + p_c[(g * NSA_HPG + h) * TQ:(g * NSA_HPG + h + 1) * TQ]
        blk = _dot_hi(imp, msel_ref[...])
        score = jnp.where(forced, jnp.inf, jnp.where(valid, blk, -jnp.inf))
        sel = (valid & (_rank_lanes(score, n_sel) < N_SEL)).astype(sel8_s.dtype)
        for h in range(NSA_HPG):
            sel8_s[(g * NSA_HPG + h) * TQ:(g * NSA_HPG + h + 1) * TQ, :] = sel

    kpos0 = _iota((rows, TQ), 1)
    jrow = _iota((LANES, TQ), 0)
    jcol = _iota((LANES, TQ), 1) >> 6

    def sel_step(kt, carry):
        k = kvn_ref[0, pl.ds(kt * TQ, TQ), 2 * LANES:3 * LANES]
        v = kvn_ref[0, pl.ds(kt * TQ, TQ), 3 * LANES:4 * LANES]
        expand = (jrow == kt * (TQ // SEL_BLK) + jcol).astype(f32)
        chosen = _dot(sel8_s[...], expand) > 0.5
        mask = chosen & (tpos >= kt * TQ + kpos0)
        s = _dot_nt(q8, k) + bt_ref[qt - kt]
        _softmax_step(s, mask, lambda p: _dot(p, v), m_s, l_s, acc_s)
        return carry

    _softmax_reset(m_s, l_s, acc_s)
    lax.fori_loop(0, qt + 1, sel_step, 0)
    osel_s[...] = acc_s[...] / l_s[...]

    def win_step(kt, carry):
        k = kvw_ref[0, pl.ds(kt * TQ, TQ), 0:LANES]
        v = kvw_ref[0, pl.ds(kt * TQ, TQ), LANES:2 * LANES]
        d = tpos - (kt * TQ + kpos0)
        mask = (d >= 0) & (d <= WINDOW)
        s = _dot_nt(q8, k) + bt_ref[qt - kt]
        _softmax_step(s, mask, lambda p: _dot(p, v), m_s, l_s, acc_s)
        return carry

    _softmax_reset(m_s, l_s, acc_s)
    lax.fori_loop(jnp.maximum(qt - WINDOW // TQ, 0), qt + 1, win_step, 0)
    owin = acc_s[...] / l_s[...]

    ga = ga_ref[...]
    for h in range(NSA_HEADS):
        g = h // NSA_HPG
        r0, r1 = h * TQ, (h + 1) * TQ
        o = (ga[:, 3 * h:3 * h + 1] * ocmp_s[r0:r1, :] + ga[:, 3 * h + 1:3 * h + 2] * osel_s[r0:r1, :]
             + ga[:, 3 * h + 2:3 * h + 3] * owin[r0:r1, :])
        o_ref[:, h * HEAD_DIM:(h + 1) * HEAD_DIM] = o[:, g * HEAD_DIM:(g + 1) * HEAD_DIM]


def _nsa_prompt(qa, kvn, kvw, ga, bias_c, bias_t, wbig, pe, batch, seq):
    nq = seq // TQ
    n_seg = seq // CMP_STRIDE
    nc = max(LANES, n_seg)
    rows = NSA_HEADS * TQ
    msel = _cmp_to_sel_matrix(nc, LANES)
    seg_w = CMP_STRIDE * NSA_GROUPS * HEAD_DIM
    vm = (2 * (seq * 768 * 4 + bias_t.size * 4 + wbig.size * 2 + rows * nc * 4) + 2 * nc * seg_w * 4
          + 8 * rows * LANES * 4 + (12 << 20))
    return pl.pallas_call(
        functools.partial(_nsa_prompt_body, seq=seq),
        grid=(batch, nq),
        in_specs=[pl.BlockSpec((TQ, qa.shape[1]), lambda b, q: (b * nq + q, 0)),
                  pl.BlockSpec((1, seq, kvn.shape[2]), lambda b, q: (b, 0, 0)),
                  pl.BlockSpec((1, seq, kvw.shape[2]), lambda b, q: (b, 0, 0)),
                  pl.BlockSpec((TQ, LANES), lambda b, q: (b * nq + q, 0)),
                  pl.BlockSpec((NSA_HEADS, TQ, nc), lambda b, q: (0, q, 0)),
                  pl.BlockSpec(bias_t.shape, lambda b, q: (0, 0, 0)),
                  pl.BlockSpec(wbig.shape, lambda b, q: (0, 0, 0, 0)),
                  pl.BlockSpec(pe.shape, lambda b, q: (0, 0, 0, 0)),
                  pl.BlockSpec(msel.shape, lambda b, q: (0, 0))],
        out_specs=pl.BlockSpec((TQ, qa.shape[1]), lambda b, q: (b * nq + q, 0)),
        out_shape=jax.ShapeDtypeStruct(qa.shape, f32),
        scratch_shapes=[pltpu.VMEM((seq, LANES), f32), pltpu.VMEM((2, nc, seg_w), f32),
                        pltpu.VMEM((nc, LANES), f32), pltpu.VMEM((nc, LANES), f32),
                        pltpu.VMEM((rows, LANES), _MXU), pltpu.VMEM((rows, LANES), _MXU),
                        pltpu.VMEM((rows, LANES), f32), pltpu.VMEM((rows, LANES), f32),
                        pltpu.VMEM((rows, 1), f32), pltpu.VMEM((rows, 1), f32), pltpu.VMEM((rows, LANES), f32)],
        compiler_params=_vmem(vm),
        name="nsa_prompt",
    )(qa, kvn, kvw, ga, bias_c, bias_t, wbig, pe, msel)


def _nsa_cmp_bias(tab, seq, nc):
    t = jnp.arange(seq, dtype=i32)[:, None]
    n = jnp.arange(nc, dtype=i32)[None, :]
    return tab[:, jnp.maximum(t - (n * CMP_STRIDE + CMP_BLK - 1), 0)]


def _moba_prompt_body(qb_ref, kvb_ref, bt_ref, o_ref, kmean_s, q8_s, sel8_s, m_s, l_s, acc_s, *, seq):
    qt = pl.program_id(1)
    t0 = qt * TQ
    nb = seq // MOBA_BLK
    rows = MOBA_HEADS * TQ
    kw = MOBA_HEADS * HEAD_DIM
    scale = HEAD_DIM ** -0.5
    pairs = MOBA_HEADS // 2

    @pl.when(qt == 0)
    def _():
        kmean_s[...] = jnp.zeros(kmean_s.shape, f32)
        for n in range(nb):
            blk = kvb_ref[0, n * MOBA_BLK:(n + 1) * MOBA_BLK, 0:kw]
            kmean_s[n:n + 1, :] = jnp.sum(blk, axis=0, keepdims=True) * (1.0 / MOBA_BLK)

    q = qb_ref[...]
    half = _iota((TQ, LANES), 1) >> 6
    nblk = _iota((TQ, LANES), 1)
    own = (t0 + _iota((TQ, LANES), 0)) >> 8
    past = nblk < own
    for h in range(MOBA_HEADS):
        qh = jnp.where(half == h % 2, q[:, (h // 2) * LANES:(h // 2 + 1) * LANES], 0.0)
        q8_s[h * TQ:(h + 1) * TQ, :] = (qh * scale).astype(q8_s.dtype)
        gate = _dot_nt_hi(qh, kmean_s[:, (h // 2) * LANES:(h // 2 + 1) * LANES])
        score = jnp.where(past, gate, -jnp.inf)
        sel = (past & (_rank_lanes(score, nb) < MOBA_TOPK)) | (nblk == own)
        sel8_s[h * TQ:(h + 1) * TQ, :] = sel.astype(sel8_s.dtype)

    tpos = t0 + (_iota((rows, 1), 0) & (TQ - 1))
    kpos0 = _iota((rows, TQ), 1)
    jrow = _iota((LANES, TQ), 0)

    def step(kt, carry):
        s_parts = []
        for pr in range(pairs):
            k = kvb_ref[0, pl.ds(kt * TQ, TQ), pr * LANES:(pr + 1) * LANES]
            s_parts.append(_dot_nt(q8_s[2 * pr * TQ:2 * (pr + 1) * TQ, :], k))
        s = jnp.concatenate(s_parts, axis=0) + bt_ref[qt - kt]
        expand = (jrow == ((kt * TQ) >> 8)).astype(f32)
        chosen = _dot(sel8_s[...], expand) > 0.5
        mask = chosen & (tpos >= kt * TQ + kpos0)

        def pv(p):
            outs = []
            for pr in range(pairs):
                v = kvb_ref[0, pl.ds(kt * TQ, TQ), kw + pr * LANES:kw + (pr + 1) * LANES]
                outs.append(_dot(p[2 * pr * TQ:2 * (pr + 1) * TQ, :], v))
            return jnp.concatenate(outs, axis=0)

        _softmax_step(s, mask, pv, m_s, l_s, acc_s)
        return carry

    _softmax_reset(m_s, l_s, acc_s)
    lax.fori_loop(0, qt + 1, step, 0)
    o = acc_s[...] / l_s[...]
    for pr in range(pairs):
        o_ref[:, pr * LANES:(pr + 1) * LANES] = jnp.where(
            half == 0, o[2 * pr * TQ:(2 * pr + 1) * TQ, :], o[(2 * pr + 1) * TQ:(2 * pr + 2) * TQ, :])


def _moba_prompt(qb, kvb, bias_t, batch, seq):
    nq = seq // TQ
    rows = MOBA_HEADS * TQ
    vm = 2 * (seq * kvb.shape[2] * 4 + bias_t.size * 4) + 8 * rows * LANES * 4 + (12 << 20)
    return pl.pallas_call(
        functools.partial(_moba_prompt_body, seq=seq),
        grid=(batch, nq),
        in_specs=[pl.BlockSpec((TQ, qb.shape[1]), lambda b, q: (b * nq + q, 0)),
                  pl.BlockSpec((1, seq, kvb.shape[2]), lambda b, q: (b, 0, 0)),
                  pl.BlockSpec(bias_t.shape, lambda b, q: (0, 0, 0))],
        out_specs=pl.BlockSpec((TQ, qb.shape[1]), lambda b, q: (b * nq + q, 0)),
        out_shape=jax.ShapeDtypeStruct(qb.shape, f32),
        scratch_shapes=[pltpu.VMEM((LANES, MOBA_HEADS * HEAD_DIM), f32),
                        pltpu.VMEM((rows, LANES), _MXU), pltpu.VMEM((rows, LANES), _MXU),
                        pltpu.VMEM((rows, 1), f32), pltpu.VMEM((rows, 1), f32), pltpu.VMEM((rows, LANES), f32)],
        compiler_params=_vmem(vm),
        name="moba_prompt",
    )(qb, kvb, bias_t)


def _nsa_sample_body(pt_ref, ka_ref, vb_ref, qa_ref, kvn_ref, kvw_ref, st_ref, ga_ref, bc_ref, bs_ref, bn_ref,
                     bw_ref, wbig_ref, pe_ref, msel_ref, o_ref,
                     xflat_s, kc_s, vc_s, q8_s, sel8_s, ocmp_s, m_s, l_s, acc_s, *, n_pages):
    del pt_ref
    ph = pl.program_id(1)
    p = pl.program_id(2)
    t = n_pages * PAGE
    seg_per_page = PAGE // CMP_STRIDE
    n_seg = n_pages * seg_per_page
    n_cmp = n_seg - CMP_BLK // CMP_STRIDE + 1
    cur = t // SEL_BLK
    n_sel = cur + 1
    nc = kc_s.shape[0]
    ls = sel8_s.shape[1]
    scale = HEAD_DIM ** -0.5

    def one_key(q8, row):
        return _dot_nt(q8, jnp.broadcast_to(row, (SUBLANES, LANES)))[:, 0:1]

    @pl.when(ph == 0)
    def _():
        if nc > n_seg:
            @pl.when(p == 0)
            def _():
                xflat_s[...] = jnp.zeros(xflat_s.shape, f32)
        for c, ref in ((0, ka_ref), (1, vb_ref)):
            for r in range(CMP_STRIDE):
                xflat_s[c, pl.ds(pl.multiple_of(p * seg_per_page, SUBLANES), seg_per_page),
                        r * LANES:(r + 1) * LANES] = ref[0, pl.ds(r, seg_per_page, stride=CMP_STRIDE), :]

        @pl.when(p == n_pages - 1)
        def _():
            kc_s[...] = _compress(xflat_s, 0, pe_ref, wbig_ref)
            vc_s[...] = _compress(xflat_s, 1, pe_ref, wbig_ref)
            _stack_heads_nsa(qa_ref[0] * scale, q8_s, 1)
            q8 = q8_s[...]
            ncol = _iota((NSA_HEADS, nc), 1)
            mask_c = ncol < n_cmp
            sc = jnp.where(mask_c, _dot_nt(q8, kc_s[...]) + bc_ref[...], NEG)
            e = jnp.exp(sc - jnp.max(sc, axis=1, keepdims=True))
            p_c = jnp.where(mask_c, e / jnp.sum(e, axis=1, keepdims=True), 0.0)
            ocmp_s[...] = _dot(p_c, vc_s[...])
            same_group = ((_iota((NSA_HEADS, NSA_HEADS), 0) >> 2) == (_iota((NSA_HEADS, NSA_HEADS), 1) >> 2)).astype(f32)
            blk = _dot_hi(_dot_hi(same_group, p_c), msel_ref[...])
            jb = _iota((NSA_HEADS, ls), 1)
            valid = jb < n_sel
            forced = valid & ((jb == 0) | (jb == cur) | (jb == cur - 1))
            score = jnp.where(forced, jnp.inf, jnp.where(valid, blk, -jnp.inf))
            sel8_s[...] = (valid & (_rank_lanes(score, n_sel) < N_SEL)).astype(sel8_s.dtype)

    @pl.when(ph == 1)
    def _():
        q8 = q8_s[...]

        @pl.when(p == 0)
        def _():
            m_s[...] = one_key(q8, kvn_ref[0, :, 2 * LANES:3 * LANES]) + bn_ref[:, 0:1]
            l_s[...] = jnp.ones(l_s.shape, f32)
            v_new = kvn_ref[0, :, 3 * LANES:4 * LANES].astype(_MXU).astype(f32)
            acc_s[...] = jnp.broadcast_to(v_new, acc_s.shape)

        s = _dot_nt(q8, ka_ref[0]) + bs_ref[0]
        jrow = _iota((ls, PAGE), 0)
        jcol = _iota((ls, PAGE), 1) >> 6
        expand = (jrow == p * (PAGE // SEL_BLK) + jcol).astype(f32)
        chosen = _dot(sel8_s[...], expand) > 0.5
        _softmax_step(s, chosen, lambda pr: _dot(pr, vb_ref[0]), m_s, l_s, acc_s)

        @pl.when(p == n_pages - 1)
        def _():
            osel = acc_s[...] / l_s[...]
            sw = _dot_nt(q8, st_ref[0, :, 0:LANES]) + bw_ref[...]
            sn = one_key(q8, kvw_ref[0, :, 0:LANES]) + bn_ref[:, 0:1]
            mw = jnp.maximum(jnp.max(sw, axis=1, keepdims=True), sn)
            ew = jnp.exp(sw - mw)
            en = jnp.exp(sn - mw)
            v_new = kvw_ref[0, :, LANES:2 * LANES].astype(_MXU).astype(f32)
            owin = (_dot(ew, st_ref[0, :, LANES:2 * LANES]) + en * v_new) / (jnp.sum(ew, axis=1, keepdims=True) + en)
            ga = ga_ref[0]
            for h in range(NSA_HEADS):
                g = h // NSA_HPG
                o = (ga[:, 3 * h:3 * h + 1] * ocmp_s[h:h + 1, :] + ga[:, 3 * h + 1:3 * h + 2] * osel[h:h + 1, :]
                     + ga[:, 3 * h + 2:3 * h + 3] * owin[h:h + 1, :])
                o_ref[0, :, h * HEAD_DIM:(h + 1) * HEAD_DIM] = o[:, g * HEAD_DIM:(g + 1) * HEAD_DIM]


def _nsa_sample(page_table, cache, qa, kvn, kvw, state, ga, tab, wbig, pe):
    bs, n_pages = page_table.shape
    t = n_pages * PAGE
    win = state.shape[1]
    nc = max(LANES, n_pages * (PAGE // CMP_STRIDE))
    ls = -(-(t // SEL_BLK + 1) // LANES) * LANES
    msel = _cmp_to_sel_matrix(nc, ls)
    n = jnp.arange(nc, dtype=i32)
    bias_c = tab[:, jnp.maximum(t - (n * CMP_STRIDE + CMP_BLK - 1), 0)]
    bias_s = jnp.transpose(tab[:, t - jnp.arange(t, dtype=i32)].reshape(NSA_HEADS, n_pages, PAGE), (1, 0, 2))
    bias_n = jnp.broadcast_to(tab[:, 0:1], (NSA_HEADS, LANES))
    bias_w = tab[:, win - jnp.arange(win, dtype=i32)]
    seg_w = CMP_STRIDE * NSA_GROUPS * HEAD_DIM
    req = lambda b, ph, p, pt: (b, 0, 0)
    c2 = lambda b, ph, p, pt: (0, 0)
    c4 = lambda b, ph, p, pt: (0, 0, 0, 0)
    grid_spec = pltpu.PrefetchScalarGridSpec(
        num_scalar_prefetch=1,
        grid=(bs, 2, n_pages),
        in_specs=[pl.BlockSpec((1, PAGE, LANES), lambda b, ph, p, pt: (pt[b, p], 0, 2 * ph)),
                  pl.BlockSpec((1, PAGE, LANES), lambda b, ph, p, pt: (pt[b, p], 0, 2 * ph + 1)),
                  pl.BlockSpec((1, 1, qa.shape[2]), req), pl.BlockSpec((1, 1, kvn.shape[2]), req),
                  pl.BlockSpec((1, 1, kvw.shape[2]), req), pl.BlockSpec((1, win, state.shape[2]), req),
                  pl.BlockSpec((1, 1, LANES), req),
                  pl.BlockSpec(bias_c.shape, c2),
                  pl.BlockSpec((1, NSA_HEADS, PAGE), lambda b, ph, p, pt: (p * ph, 0, 0)),
                  pl.BlockSpec(bias_n.shape, c2), pl.BlockSpec(bias_w.shape, c2),
                  pl.BlockSpec(wbig.shape, c4), pl.BlockSpec(pe.shape, c4), pl.BlockSpec(msel.shape, c2)],
        out_specs=pl.BlockSpec((1, 1, qa.shape[2]), req),
        scratch_shapes=[pltpu.VMEM((2, nc, seg_w), f32), pltpu.VMEM((nc, LANES), f32), pltpu.VMEM((nc, LANES), f32),
                        pltpu.VMEM((NSA_HEADS, LANES), _MXU), pltpu.VMEM((NSA_HEADS, ls), _MXU),
                        pltpu.VMEM((NSA_HEADS, LANES), f32),
                        pltpu.VMEM((NSA_HEADS, 1), f32), pltpu.VMEM((NSA_HEADS, 1), f32),
                        pltpu.VMEM((NSA_HEADS, LANES), f32)])
    vm = 2 * nc * seg_w * 4 + 2 * (wbig.size * 2 + msel.size * 4 + win * 256 * 4) + (16 << 20)
    return pl.pallas_call(
        functools.partial(_nsa_sample_body, n_pages=n_pages),
        grid_spec=grid_spec,
        out_shape=jax.ShapeDtypeStruct(qa.shape, f32),
        compiler_params=_vmem(vm),
        name="nsa_sample",
    )(page_table, cache, cache, qa, kvn, kvw, state, ga, bias_c, bias_s, bias_n, bias_w, wbig, pe, msel)


def _moba_sample_body(pt_ref, k_ref, v_ref, qb_ref, kvb_ref, bs_ref, bn_ref, o_ref,
                      kmean_s, q8_s, sel8_s, m_s, l_s, acc_s, *, n_pages):
    del pt_ref
    ph = pl.program_id(1)
    p = pl.program_id(2)
    t = n_pages * PAGE
    kw = MOBA_HEADS * HEAD_DIM
    pages_per_blk = MOBA_BLK // PAGE
    own = t // MOBA_BLK
    scale = HEAD_DIM ** -0.5
    head_lane = (_iota((MOBA_HEADS, kw), 1) >> 6) == _iota((MOBA_HEADS, kw), 0)

    @pl.when(ph == 0)
    def _():
        @pl.when(p == 0)
        def _():
            kmean_s[...] = jnp.zeros(kmean_s.shape, f32)

        kmean_s[pl.ds(p // pages_per_blk, 1), :] += jnp.sum(k_ref[0], axis=0, keepdims=True)

        @pl.when(p == n_pages - 1)
        def _():
            q8 = jnp.where(head_lane, jnp.broadcast_to(qb_ref[0], (MOBA_HEADS, kw)), 0.0)
            q8_s[...] = q8
            gate = _dot_nt_hi(q8, kmean_s[...] * (1.0 / MOBA_BLK))
            nblk = _iota((MOBA_HEADS, LANES), 1)
            past = nblk < own
            score = jnp.where(past, gate, -jnp.inf)
            sel8_s[...] = (past & (_rank_lanes(score, own) < MOBA_TOPK)).astype(sel8_s.dtype)

    @pl.when(ph == 1)
    def _():
        q8 = q8_s[...] * scale

        @pl.when(p == 0)
        def _():
            k_new = jnp.broadcast_to(kvb_ref[0, :, 0:kw], (SUBLANES, kw))
            m_s[...] = _dot_nt(q8, k_new)[:, 0:1] + bn_ref[:, 0:1]
            l_s[...] = jnp.ones(l_s.shape, f32)
            acc_s[...] = jnp.broadcast_to(kvb_ref[0, :, kw:2 * kw].astype(_MXU).astype(f32), acc_s.shape)

        s = _dot_nt(q8, k_ref[0]) + bs_ref[0]
        expand = (_iota((LANES, PAGE), 0) == p // pages_per_blk).astype(f32)
        chosen = _dot(sel8_s[...], expand) > 0.5
        _softmax_step(s, chosen, lambda pr: _dot(pr, v_ref[0]), m_s, l_s, acc_s)

        @pl.when(p == n_pages - 1)
        def _():
            o8 = jnp.where(head_lane, acc_s[...] / l_s[...], 0.0)
            o_ref[0] = jnp.sum(o8, axis=0, keepdims=True)


def _moba_sample(page_table, cache, qb, kvb, tab):
    bs, n_pages = page_table.shape
    t = n_pages * PAGE
    kw = MOBA_HEADS * HEAD_DIM
    bias_s = jnp.transpose(tab[:, t - jnp.arange(t, dtype=i32)].reshape(MOBA_HEADS, n_pages, PAGE), (1, 0, 2))
    bias_n = jnp.broadcast_to(tab[:, 0:1], (MOBA_HEADS, LANES))
    req = lambda b, ph, p, pt: (b, 0, 0)
    grid_spec = pltpu.PrefetchScalarGridSpec(
        num_scalar_prefetch=1,
        grid=(bs, 2, n_pages),
        in_specs=[pl.BlockSpec((1, PAGE, kw), lambda b, ph, p, pt: (pt[b, p], 0, 0)),
                  pl.BlockSpec((1, PAGE, kw), lambda b, ph, p, pt: (pt[b, p * ph], 0, 1)),
                  pl.BlockSpec((1, 1, kw), req), pl.BlockSpec((1, 1, 2 * kw), req),
                  pl.BlockSpec((1, MOBA_HEADS, PAGE), lambda b, ph, p, pt: (p * ph, 0, 0)),
                  pl.BlockSpec(bias_n.shape, lambda b, ph, p, pt: (0, 0))],
        out_specs=pl.BlockSpec((1, 1, kw), req),
        scratch_shapes=[pltpu.VMEM((LANES, kw), f32), pltpu.VMEM((MOBA_HEADS, kw), f32),
                        pltpu.VMEM((MOBA_HEADS, LANES), _MXU),
                        pltpu.VMEM((MOBA_HEADS, 1), f32), pltpu.VMEM((MOBA_HEADS, 1), f32),
                        pltpu.VMEM((MOBA_HEADS, kw), f32)])
    return pl.pallas_call(
        functools.partial(_moba_sample_body, n_pages=n_pages),
        grid_spec=grid_spec,
        out_shape=jax.ShapeDtypeStruct(qb.shape, f32),
        compiler_params=_vmem(24 << 20),
        name="moba_sample",
    )(page_table, cache, cache, qb, kvb, bias_s, bias_n)


def _layer_norm(x, g, b):
    mu = jnp.mean(x, axis=-1, keepdims=True)
    xc = x - mu
    var = jnp.mean(xc * xc, axis=-1, keepdims=True)
    return xc * lax.rsqrt(var + LN_EPS) * g + b


def _merge_body(oa_ref, ob_ref, gm_ref, x_ref, wa_ref, wb_ref, wo_ref, g_ref, b_ref, h_ref, ht_ref, *, alpha):
    d = x_ref.shape[1]
    ya = _dot(oa_ref[...], wa_ref[...])
    yb = _dot(ob_ref[...], wb_ref[...])
    gm = gm_ref[...]
    mix = _dot(gm[:, :d] * ya + gm[:, d:] * yb, wo_ref[...])
    h = _layer_norm(alpha * x_ref[...] + mix, g_ref[...], b_ref[...])
    h_ref[...] = h
    ht_ref[...] = h.T.astype(ht_ref.dtype)


def _merge(oa, ob, gm, x2d, wa, wb, wo, g, b, alpha):
    n, d = x2d.shape
    tm = 256 if n % 256 == 0 else n
    row = lambda i: (i, 0)
    full = lambda i: (0, 0)
    vm = 2 * (tm * (oa.shape[1] + ob.shape[1] + gm.shape[1] + 2 * d) * 4 + tm * d * 2
              + (wa.size + wb.size + wo.size) * 2) + (16 << 20)
    return pl.pallas_call(
        functools.partial(_merge_body, alpha=alpha),
        grid=(n // tm,),
        in_specs=[pl.BlockSpec((tm, oa.shape[1]), row), pl.BlockSpec((tm, ob.shape[1]), row),
                  pl.BlockSpec((tm, gm.shape[1]), row), pl.BlockSpec((tm, d), row),
                  pl.BlockSpec(wa.shape, full), pl.BlockSpec(wb.shape, full), pl.BlockSpec(wo.shape, full),
                  pl.BlockSpec((1, d), full), pl.BlockSpec((1, d), full)],
        out_specs=[pl.BlockSpec((tm, d), row), pl.BlockSpec((d, tm), lambda i: (0, i))],
        out_shape=[jax.ShapeDtypeStruct((n, d), f32), jax.ShapeDtypeStruct((d, n), _MXU)],
        compiler_params=_vmem(vm),
        name="merge",
    )(oa, ob, gm, x2d, wa, wb, wo, g, b)


_NOT_TOP = 99.0
_CAND_B = tuple(PEER_TOPK // (a + 1) for a in range(SUBLANES))


def _extract_top(work_s, n_rounds, on_round):
    rows = _iota(work_s.shape, 0)

    def body(r, carry):
        w = work_s[...]
        m = jnp.max(w, axis=0, keepdims=True)
        idx = jnp.min(jnp.where(w == m, rows, work_s.shape[0]), axis=0, keepdims=True)
        hit = rows == idx
        on_round(r, m, hit)
        work_s[...] = jnp.where(hit, -jnp.inf, w)
        return carry

    lax.fori_loop(0, n_rounds, body, 0)


def _peer_select(ht_ref, wqt_ref, keys_ref, qt_s, rank_s, exp_s, sv_s, work_s, cwork_s, cand_s, cnt_s, f0_s):
    tn = ht_ref.shape[1]
    nk = PEER_N_KEYS
    qt_s[...] = jnp.dot(wqt_ref[...], ht_ref[...], preferred_element_type=f32)

    def per_half(hc, carry):
        s = _dot(keys_ref[hc], qt_s[pl.ds(pl.multiple_of(hc * PEER_HALF, PEER_HALF), PEER_HALF), :])
        work_s[...] = s
        rank_s[hc] = jnp.full((nk, tn), _NOT_TOP, f32)

        def on_round(r, m, hit):
            sv_s[hc, pl.ds(r, 1), :] = m
            rank_s[hc] = jnp.where(hit, r.astype(f32), rank_s[hc])

        _extract_top(work_s, PEER_TOPK, on_round)
        exp_s[hc] = jnp.exp(s - sv_s[hc, 0:1, :])
        return carry

    lax.fori_loop(0, 2 * PEER_HEADS, per_half, 0)

    n_cand = cand_s.shape[0]

    def per_head(h, carry):
        sv0 = sv_s[2 * h]
        sv1 = sv_s[2 * h + 1]
        e0 = jnp.exp(sv0 - sv0[0:1, :])
        e1 = jnp.exp(sv1 - sv1[0:1, :])
        brow = _iota((SUBLANES, tn), 0)
        tiles, etiles = [sv0[0:1, :] + sv1, ], [e0[0:1, :] * e1]
        for a in range(1, SUBLANES):
            ok = brow < _CAND_B[a]
            tiles.append(jnp.where(ok, sv0[a:a + 1, :] + sv1[0:SUBLANES, :], -jnp.inf))
            etiles.append(jnp.where(ok, e0[a:a + 1, :] * e1[0:SUBLANES, :], 0.0))
        tiles.append(sv0[SUBLANES:, :] + sv1[0:1, :])
        etiles.append(e0[SUBLANES:, :] * e1[0:1, :])
        cwork_s[...] = jnp.concatenate(tiles, axis=0)
        cand_s[...] = jnp.zeros(cand_s.shape, f32)

        def on_round(r, m, hit):
            cand_s[...] = jnp.where(hit, 1.0, cand_s[...])

        _extract_top(cwork_s, PEER_TOPK, on_round)
        chosen = cand_s[...]
        z = jnp.sum(chosen * jnp.concatenate(etiles, axis=0), axis=0, keepdims=True)
        cnts = [jnp.sum(chosen[0:PEER_TOPK], axis=0, keepdims=True)]
        for a in range(1, SUBLANES):
            r0 = PEER_TOPK + (a - 1) * SUBLANES
            cnts.append(jnp.sum(chosen[r0:r0 + SUBLANES], axis=0, keepdims=True))
        tail = chosen[n_cand - SUBLANES:]
        cnts += [tail[a:a + 1, :] for a in range(SUBLANES)]
        ra = rank_s[2 * h]
        c0 = jnp.zeros((nk, tn), f32)
        for a in range(PEER_TOPK):
            c0 = jnp.where(ra == float(a), cnts[a], c0)
        cnt_s[h] = c0
        f0_s[h] = exp_s[2 * h] / z
        return carry

    lax.fori_loop(0, PEER_HEADS, per_head, 0)


def _peer_body(ht_ref, h_ref, wqt_ref, keys_ref, u_ref, vt_ref, g_ref, b_ref, y_ref,
               qt_s, rank_s, exp_s, sv_s, work_s, cwork_s, cand_s, cnt_s, f0_s, acc_s, *, alpha, ti):
    e = pl.program_id(1)
    tn = ht_ref.shape[1]

    @pl.when(e == 0)
    def _():
        _peer_select(ht_ref, wqt_ref, keys_ref, qt_s, rank_s, exp_s, sv_s, work_s, cwork_s, cand_s, cnt_s, f0_s)
        acc_s[...] = jnp.zeros(acc_s.shape, f32)

    at = jnp.dot(u_ref[...], ht_ref[...], preferred_element_type=f32)
    gel = 0.5 * at * (1.0 + lax.erf(at * math.sqrt(0.5)))
    parts = []
    for ii in range(ti):
        i = e * ti + ii
        w = jnp.zeros((PEER_N_KEYS, tn), f32)
        for h in range(PEER_HEADS):
            cnt = cnt_s[h, pl.ds(i, 1), :]
            f0 = f0_s[h, pl.ds(i, 1), :]
            w = w + jnp.where(rank_s[2 * h + 1] < cnt, exp_s[2 * h + 1], 0.0) * f0
        parts.append((w * gel[ii * PEER_N_KEYS:(ii + 1) * PEER_N_KEYS, :]).astype(_MXU))
    acc_s[...] += jnp.dot(vt_ref[...], jnp.concatenate(parts, axis=0), preferred_element_type=f32)

    @pl.when(e == pl.num_programs(1) - 1)
    def _():
        y_ref[...] = _layer_norm(alpha * h_ref[...] + acc_s[...].T, g_ref[...], b_ref[...])


def _peer(ht, h, wqt, keys, u, vt, g, b, alpha):
    d, n = ht.shape
    n_exp = u.shape[0]
    tn = 512 if n % 512 == 0 else n
    ti = 4
    te = ti * PEER_N_KEYS
    n_half = 2 * PEER_HEADS
    n_cand = PEER_TOPK + (SUBLANES - 1) * SUBLANES + SUBLANES
    tok = lambda t, e: (0, t)
    full2 = lambda t, e: (0, 0)
    scratch = [pltpu.VMEM((wqt.shape[0], tn), f32),
               pltpu.VMEM((n_half, PEER_N_KEYS, tn), f32), pltpu.VMEM((n_half, PEER_N_KEYS, tn), f32),
               pltpu.VMEM((n_half, PEER_TOPK, tn), f32), pltpu.VMEM((PEER_N_KEYS, tn), f32),
               pltpu.VMEM((n_cand, tn), f32), pltpu.VMEM((n_cand, tn), f32),
               pltpu.VMEM((PEER_HEADS, PEER_N_KEYS, tn), f32), pltpu.VMEM((PEER_HEADS, PEER_N_KEYS, tn), f32),
               pltpu.VMEM((d, tn), f32)]
    vm = (wqt.shape[0] * tn * 4 + 2 * n_half * PEER_N_KEYS * tn * 4 + 2 * PEER_HEADS * PEER_N_KEYS * tn * 4
          + d * tn * 4 + 2 * (d * tn * 2 + 2 * tn * d * 4 + wqt.size * 2 + keys.size * 2 + 2 * te * d * 2)
          + (14 << 20))
    return pl.pallas_call(
        functools.partial(_peer_body, alpha=alpha, ti=ti),
        grid=(n // tn, n_exp // te),
        in_specs=[pl.BlockSpec((d, tn), tok),
                  pl.BlockSpec((tn, d), lambda t, e: (t, 0)),
                  pl.BlockSpec(wqt.shape, full2),
                  pl.BlockSpec(keys.shape, lambda t, e: (0, 0, 0)),
                  pl.BlockSpec((te, d), lambda t, e: (e, 0)),
                  pl.BlockSpec((d, te), lambda t, e: (0, e)),
                  pl.BlockSpec((1, d), full2), pl.BlockSpec((1, d), full2)],
        out_specs=pl.BlockSpec((tn, d), lambda t, e: (t, 0)),
        out_shape=jax.ShapeDtypeStruct((n, d), f32),
        scratch_shapes=scratch,
        compiler_params=_vmem(vm),
        name="peer",
    )(ht, h, wqt, keys, u, vt, g, b)


def kernel(x_prompt, x_sample, cache_nsa, cache_moba, state_nsa_win, page_table, w_in, nsa_cmp_pe, nsa_cmp_w, w_branch_a, w_branch_b, w_out, ln1_g, ln1_b, peer_wq, peer_keys, peer_u, peer_v, ln2_g, ln2_b, rel_bias):
    depth = w_in.shape[0]
    batch, seq, d = x_prompt.shape
    bs, dec_seq, _ = x_sample.shape
    n_pages = page_table.shape[1]
    past = n_pages * PAGE
    win_len = state_nsa_win.shape[2]
    assert dec_seq == 1 and cache_nsa.shape[2] == PAGE and cache_moba.shape[2] == PAGE
    assert seq % MOBA_BLK == 0 and past % MOBA_BLK == 0 and win_len == WINDOW and seq % TQ == 0
    alpha = (2 * depth) ** 0.25
    nq = seq // TQ
    n_tok = batch * seq

    tab = _bias_by_distance(rel_bias, max(seq, past + 1) + TQ)
    tab_a, tab_b = tab[:NSA_HEADS], tab[NSA_HEADS:]
    tiles_a, tiles_b = _bias_tiles(tab_a, nq), _bias_tiles(tab_b, nq)
    cmp_bias = _nsa_cmp_bias(tab_a, seq, max(LANES, seq // CMP_STRIDE))

    xp = x_prompt.reshape(n_tok, d)
    xs = x_sample.reshape(bs, d)
    outs = [[] for _ in range(6)]
    for l in range(depth):
        w_packed = _pack_w_in(w_in[l], d)
        wbig, pe = _pack_cmp(nsa_cmp_pe[l], nsa_cmp_w[l])
        wa, wb, wo = w_branch_a[l].astype(_MXU), w_branch_b[l].astype(_MXU), w_out[l].astype(_MXU)
        wqt = peer_wq[l].T.astype(_MXU)
        keys = peer_keys[l].reshape(2 * PEER_HEADS, PEER_N_KEYS, PEER_HALF).astype(_MXU)
        u = peer_u[l].astype(_MXU)
        vt = peer_v[l].T.astype(_MXU)
        g1, b1, g2, b2 = ln1_g[l][None], ln1_b[l][None], ln2_g[l][None], ln2_b[l][None]

        def ffn(oa, ob, gm, x2d):
            h, ht = _merge(oa, ob, gm, x2d, wa, wb, wo, g1, b1, alpha)
            return _peer(ht, h, wqt, keys, u, vt, g2, b2, alpha)

        qa, kvn, kvw, qb, kvb, gm, ga = _project(xp, w_packed, d)
        oa = _nsa_prompt(qa, kvn.reshape(batch, seq, -1), kvw.reshape(batch, seq, -1), ga, cmp_bias, tiles_a,
                         wbig, pe, batch, seq)
        ob = _moba_prompt(qb, kvb.reshape(batch, seq, -1), tiles_b, batch, seq)
        xp = ffn(oa, ob, gm, xp)
        outs[0].append(kvn.reshape(batch, seq, 4, NSA_GROUPS, HEAD_DIM))
        outs[1].append(kvb.reshape(batch, seq, 2, MOBA_HEADS, HEAD_DIM))
        outs[2].append(kvw.reshape(batch, seq, 2, NSA_GROUPS, HEAD_DIM)[:, seq - min(WINDOW, seq):])

        qa, kvn, kvw, qb, kvb, gm, ga = _project(xs, w_packed, d)
        r3 = lambda a: a.reshape(bs, 1, -1)
        state = state_nsa_win[l].reshape(bs, win_len, -1)
        oa = _nsa_sample(page_table, cache_nsa[l].reshape(-1, PAGE, 4 * NSA_GROUPS * HEAD_DIM), r3(qa), r3(kvn),
                         r3(kvw), state, r3(ga), tab_a, wbig, pe)
        ob = _moba_sample(page_table, cache_moba[l].reshape(-1, PAGE, 2 * MOBA_HEADS * HEAD_DIM), r3(qb), r3(kvb),
                          tab_b)
        xs = ffn(oa.reshape(bs, -1), ob.reshape(bs, -1), gm, xs)
        outs[3].append(kvn.reshape(bs, 1, 4, NSA_GROUPS, HEAD_DIM))
        outs[4].append(kvb.reshape(bs, 1, 2, MOBA_HEADS, HEAD_DIM))
        win_new = jnp.concatenate([state_nsa_win[l][:, 1:], kvw.reshape(bs, 1, 2, NSA_GROUPS, HEAD_DIM)], axis=1)
        outs[5].append(win_new)

    stacked = [jnp.stack(o) for o in outs]
    return (xp.reshape(batch, seq, d), xs.reshape(bs, 1, d), *stacked)
```

```python
import functools
import math

import jax
import jax.numpy as jnp
from jax import lax
from jax.experimental import pallas as pl
from jax.experimental.pallas import tpu as pltpu

HEAD_DIM = 64
NSA_HEADS = 8
NSA_GROUPS = 2
NSA_HPG = NSA_HEADS // NSA_GROUPS
CMP_BLK = 32
CMP_STRIDE = 16
SEL_BLK = 64
N_SEL = 16
WINDOW = 512
MOBA_HEADS = 8
MOBA_BLK = 256
MOBA_TOPK = 3
N_BUCKETS = 32
MAX_DISTANCE = 1024
PEER_HEADS = 8
PEER_N_KEYS = 128
PEER_HALF = 128
PEER_TOPK = 16
LN_EPS = 1e-5
NEG = -1e30

LANES = 128
SUBLANES = 8
VMEM_BYTES_V7X = 64 * 1024 * 1024

TQ = 128
PAGE = 128

f32 = jnp.float32
i32 = jnp.int32
_MXU = jnp.bfloat16
_HI = lax.Precision.HIGHEST


def _iota(shape, dim):
    return lax.broadcasted_iota(i32, shape, dim)


def _dot(a, b):
    return jnp.dot(a.astype(_MXU), b.astype(_MXU), preferred_element_type=f32)


def _dot_nt(a, b):
    return lax.dot_general(a.astype(_MXU), b.astype(_MXU), (((1,), (1,)), ((), ())),
                           preferred_element_type=f32)


def _dot_tn(a, b):
    return lax.dot_general(a.astype(_MXU), b.astype(_MXU), (((0,), (0,)), ((), ())), preferred_element_type=f32)


def _dot_hi(a, b):
    return jnp.dot(a, b, precision=_HI, preferred_element_type=f32)


def _dot_nt_hi(a, b):
    return lax.dot_general(a, b, (((1,), (1,)), ((), ())), precision=_HI, preferred_element_type=f32)


def _vmem(nbytes):
    return pltpu.CompilerParams(vmem_limit_bytes=int(min(nbytes, VMEM_BYTES_V7X - (4 << 20))))


def _round_up(x, m):
    return -(-x // m) * m


def _rank_lanes(s, n):
    width = s.shape[1]
    j = _iota(s.shape, 1)
    rank = jnp.zeros(s.shape, i32)
    for r in range(1, n):
        lo = pltpu.roll(s, r, 1)
        rank = rank + jnp.where((j >= r) & (lo >= s), 1, 0)
        hi = pltpu.roll(s, width - r, 1)
        rank = rank + jnp.where((j + r < n) & (hi > s), 1, 0)
    return rank


def _rank_rows(s, n):
    j = _iota(s.shape, 0)
    rank = jnp.zeros(s.shape, i32)
    for k in range(n):
        row = s[k:k + 1, :]
        rank = rank + jnp.where((row > s) | ((row == s) & (j > k)), 1, 0)
    return rank


def _softmax_reset(m_s, l_s, acc_s):
    m_s[...] = jnp.full(m_s.shape, NEG, f32)
    l_s[...] = jnp.zeros(l_s.shape, f32)
    acc_s[...] = jnp.zeros(acc_s.shape, f32)


def _softmax_step(s, mask, v, m_s, l_s, acc_s):
    s = jnp.where(mask, s, NEG)
    m_old = m_s[...]
    m_new = jnp.maximum(m_old, jnp.max(s, axis=1, keepdims=True))
    alpha = jnp.exp(m_old - m_new)
    p = jnp.where(mask, jnp.exp(s - m_new), 0.0)
    l_s[...] = alpha * l_s[...] + jnp.sum(p, axis=1, keepdims=True)
    acc_s[...] = alpha * acc_s[...] + _dot(p, v)
    m_s[...] = m_new


def _softmax_step_t(s, mask, v, m_s, l_s, acc_s):
    s = jnp.where(mask, s, NEG)
    m_old = m_s[...]
    m_new = jnp.maximum(m_old, jnp.max(s, axis=0, keepdims=True))
    alpha = jnp.exp(m_old - m_new)
    p = jnp.where(mask, jnp.exp(s - m_new), 0.0)
    l_s[...] = alpha * l_s[...] + jnp.sum(p, axis=0, keepdims=True)
    acc_s[...] = alpha * acc_s[...] + _dot_tn(v, p)
    m_s[...] = m_new


_W_QA = NSA_HEADS * HEAD_DIM
_W_KVN = 4 * NSA_GROUPS * HEAD_DIM
_W_KVW = 2 * NSA_GROUPS * HEAD_DIM
_W_GA = 3 * NSA_HEADS
_W_QB = MOBA_HEADS * HEAD_DIM
_W_KVB = 2 * MOBA_HEADS * HEAD_DIM


def _proj_body(x_ref, w_ref, qa_ref, kvn_ref, kvw_ref, qb_ref, kvb_ref, gm_ref, ga_ref):
    xb = x_ref[...].astype(_MXU)
    col = 0
    for ref, act in ((qa_ref, None), (kvn_ref, None), (kvw_ref, None), (qb_ref, None), (kvb_ref, None),
                     (gm_ref, jax.nn.sigmoid), (ga_ref, jax.nn.sigmoid)):
        width = ref.shape[1]
        y = jnp.dot(xb, w_ref[:, col:col + width], preferred_element_type=f32)
        ref[...] = y if act is None else act(y)
        col += width


def _project(x2d, w_packed, d_model):
    n = x2d.shape[0]
    tm = 256 if n % 256 == 0 else n
    widths = (_W_QA, _W_KVN, _W_KVW, _W_QB, _W_KVB, 2 * d_model, LANES)
    n_cols = sum(widths)
    return pl.pallas_call(
        _proj_body,
        grid=(n // tm,),
        in_specs=[pl.BlockSpec((tm, d_model), lambda i: (i, 0)),
                  pl.BlockSpec((d_model, n_cols), lambda i: (0, 0))],
        out_specs=[pl.BlockSpec((tm, w), lambda i: (i, 0)) for w in widths],
        out_shape=[jax.ShapeDtypeStruct((n, w), f32) for w in widths],
        compiler_params=_vmem(2 * (tm * d_model * 4 + d_model * n_cols * 2 + tm * n_cols * 4) + (8 << 20)),
        name="proj",
    )(x2d, w_packed)


def _pack_w_in(w_in, d_model):
    c_ga = _W_QA + _W_KVN + _W_KVW
    c_qb = c_ga + _W_GA
    pad = jnp.zeros((d_model, LANES - _W_GA), w_in.dtype)
    return jnp.concatenate([w_in[:, :c_ga], w_in[:, c_qb:], w_in[:, c_ga:c_qb], pad], axis=1).astype(_MXU)


def _t5_bucket(dist):
    n = jnp.maximum(dist, 0)
    max_exact = N_BUCKETS // 2
    nf = jnp.maximum(n, 1).astype(f32)
    large = max_exact + (jnp.log(nf / max_exact) / math.log(MAX_DISTANCE / max_exact)
                         * (N_BUCKETS - max_exact)).astype(i32)
    return jnp.where(n < max_exact, n, jnp.minimum(large, N_BUCKETS - 1)).astype(i32)


def _bias_by_distance(rel_bias, max_dist):
    return rel_bias[_t5_bucket(jnp.arange(max_dist, dtype=i32))].T


def _windows(v, starts, length):
    return jnp.stack([v[:, s:s + length] for s in starts], axis=1)


def _bias_tiles_t(tab, n_tiles):
    h = tab.shape[0]
    tabp = jnp.concatenate([jnp.zeros((h, TQ - 1), tab.dtype), tab], axis=1)
    g = _windows(tabp, range(TQ), n_tiles * TQ).reshape(h, TQ, n_tiles, TQ)[..., ::-1]
    return jnp.transpose(g, (2, 3, 0, 1)).reshape(n_tiles, TQ, h * TQ)


def _nsa_cmp_bias_t(tab, seq, nc):
    h = tab.shape[0]
    off = CMP_STRIDE * (nc - 1) + CMP_BLK - 1
    tabp = jnp.concatenate([jnp.zeros((h, off), tab.dtype), tab[:, :seq]], axis=1)
    w = _windows(tabp, [CMP_STRIDE * (nc - 1 - n) for n in range(nc)], seq)
    return jnp.transpose(w.reshape(h, nc, seq // TQ, TQ), (2, 1, 0, 3)).reshape(seq // TQ, nc, h * TQ)


def _pack_cmp(cmp_pe, cmp_w):
    r = CMP_BLK // CMP_STRIDE
    w = cmp_w.reshape(2, r, CMP_STRIDE, HEAD_DIM, HEAD_DIM)
    eye = jnp.eye(NSA_GROUPS, dtype=w.dtype)
    wbig = jnp.einsum("cmrde,gh->cmrgdhe", w, eye).reshape(2, r, CMP_STRIDE * NSA_GROUPS * HEAD_DIM,
                                                          NSA_GROUPS * HEAD_DIM)
    pe = cmp_pe.reshape(2, r, CMP_STRIDE, 1, HEAD_DIM)
    pe = jnp.broadcast_to(pe, (2, r, CMP_STRIDE, NSA_GROUPS, HEAD_DIM)).reshape(2, r, 1, -1)
    return wbig.astype(_MXU), pe.astype(f32)


def _cmp_to_sel_matrix(n_rows, n_cols):
    spb = SEL_BLK // CMP_STRIDE
    n = jnp.arange(n_rows, dtype=i32)[:, None]
    j = jnp.arange(n_cols, dtype=i32)[None, :]
    m = jnp.zeros((n_rows, n_cols), f32)
    for k in range(CMP_BLK // CMP_STRIDE):
        m = m + ((n + k) // spb == j).astype(f32)
    return m


def _compress(xflat_ref, c, pe_ref, wbig_ref):
    x = xflat_ref[c]
    a = _dot(x + pe_ref[c, 0], wbig_ref[c, 0])
    b = _dot(x + pe_ref[c, 1], wbig_ref[c, 1])
    return a + pltpu.roll(b, x.shape[0] - 1, 0)


def _nsa_prompt_body(qa_ref, kvn_ref, kvw_ref, ga_ref, bc_ref, bt_ref, wbig_ref, pe_ref, mselt_ref, o_ref,
                     rows_s, xflat_s, kc_s, vc_s, q8_s, sel8_s, ocmp_s, osel_s, m_s, l_s, acc_s, *, seq):
    qt = pl.program_id(1)
    t0 = qt * TQ
    n_seg = seq // CMP_STRIDE
    n_cmp = n_seg - CMP_BLK // CMP_STRIDE + 1
    n_sel = -(-seq // SEL_BLK)
    nr = _round_up(n_sel, SUBLANES)
    nc = kc_s.shape[0]
    cols = NSA_HEADS * TQ
    scale = HEAD_DIM ** -0.5

    @pl.when(qt == 0)
    def _():
        for c in range(2):
            if nc > n_seg:
                xflat_s[c] = jnp.zeros(xflat_s.shape[1:], f32)
            rows_s[...] = kvn_ref[0, :, c * LANES:(c + 1) * LANES]
            for r in range(CMP_STRIDE):
                xflat_s[c, 0:n_seg, r * LANES:(r + 1) * LANES] = rows_s[pl.ds(r, n_seg, stride=CMP_STRIDE), :]
        kc_s[...] = _compress(xflat_s, 0, pe_ref, wbig_ref)
        vc_s[...] = _compress(xflat_s, 1, pe_ref, wbig_ref)
        sel8_s[...] = jnp.zeros(sel8_s.shape, f32)

    qtr = (qa_ref[...] * scale).T
    zero = jnp.zeros((HEAD_DIM, TQ), f32)
    for h in range(NSA_HEADS):
        qh = qtr[h * HEAD_DIM:(h + 1) * HEAD_DIM, :]
        blk = jnp.concatenate([qh, zero] if h // NSA_HPG == 0 else [zero, qh], axis=0)
        q8_s[:, h * TQ:(h + 1) * TQ] = blk.astype(q8_s.dtype)
    q8 = q8_s[...]
    tpos = t0 + (_iota((1, cols), 1) & (TQ - 1))

    nrow = _iota((nc, cols), 0)
    mask_c = (nrow < n_cmp) & (tpos >= nrow * CMP_STRIDE + (CMP_BLK - 1))
    sc = jnp.where(mask_c, _dot(kc_s[...], q8) + bc_ref[0], NEG)
    e = jnp.exp(sc - jnp.max(sc, axis=0, keepdims=True))
    p_c = jnp.where(mask_c, e / jnp.sum(e, axis=0, keepdims=True), 0.0)
    ocmp_s[...] = _dot_tn(vc_s[...], p_c)

    jb = _iota((nr, TQ), 0)
    cur = (t0 + _iota((nr, TQ), 1)) >> 6
    valid = (jb <= cur) & (jb < n_sel)
    forced = valid & ((jb == 0) | (jb == cur) | (jb == cur - 1))
    for g in range(NSA_GROUPS):
        imp = p_c[:, g * NSA_HPG * TQ:(g * NSA_HPG + 1) * TQ]
        for h in range(1, NSA_HPG):
            imp = imp + p_c[:, (g * NSA_HPG + h) * TQ:(g * NSA_HPG + h + 1) * TQ]
        blk = _dot_hi(mselt_ref[...], imp)[0:nr, :]
        score = jnp.where(forced, jnp.inf, jnp.where(valid, blk, -jnp.inf))
        sel = (valid & (_rank_rows(score, n_sel) < N_SEL)).astype(f32)
        for h in range(NSA_HPG):
            sel8_s[0:nr, (g * NSA_HPG + h) * TQ:(g * NSA_HPG + h + 1) * TQ] = sel

    krow = _iota((TQ, cols), 0)
    blocks_per_tile = TQ // SEL_BLK

    def sel_step(kt, carry):
        k = kvn_ref[0, pl.ds(kt * TQ, TQ), 2 * LANES:3 * LANES]
        v = kvn_ref[0, pl.ds(kt * TQ, TQ), 3 * LANES:4 * LANES]
        chosen = sel8_s[pl.ds(kt * blocks_per_tile, 1), :]
        for i in range(1, blocks_per_tile):
            chosen = jnp.where(krow < i * SEL_BLK, chosen, sel8_s[pl.ds(kt * blocks_per_tile + i, 1), :])
        mask = (chosen > 0.5) & (tpos >= kt * TQ + krow)
        _softmax_step_t(_dot(k, q8) + bt_ref[qt - kt], mask, v, m_s, l_s, acc_s)
        return carry

    _softmax_reset(m_s, l_s, acc_s)
    lax.fori_loop(0, qt + 1, sel_step, 0)
    osel_s[...] = acc_s[...] / l_s[...]

    def win_step(kt, carry):
        k = kvw_ref[0, pl.ds(kt * TQ, TQ), 0:LANES]
        v = kvw_ref[0, pl.ds(kt * TQ, TQ), LANES:2 * LANES]
        d = tpos - (kt * TQ + krow)
        _softmax_step_t(_dot(k, q8) + bt_ref[qt - kt], (d >= 0) & (d <= WINDOW), v, m_s, l_s, acc_s)
        return carry

    _softmax_reset(m_s, l_s, acc_s)
    lax.fori_loop(jnp.maximum(qt - WINDOW // TQ, 0), qt + 1, win_step, 0)
    owin = acc_s[...] / l_s[...]

    gat = ga_ref[...].T
    pieces = []
    for h in range(NSA_HEADS):
        g = h // NSA_HPG
        c0, c1 = h * TQ, (h + 1) * TQ
        o = (gat[3 * h:3 * h + 1, :] * ocmp_s[:, c0:c1] + gat[3 * h + 1:3 * h + 2, :] * osel_s[:, c0:c1]
             + gat[3 * h + 2:3 * h + 3, :] * owin[:, c0:c1])
        pieces.append(o[g * HEAD_DIM:(g + 1) * HEAD_DIM, :])
    o_ref[...] = jnp.concatenate(pieces, axis=0).T


def _nsa_prompt(qa, kvn, kvw, ga, bias_c, bias_t, wbig, pe, batch, seq):
    nq = seq // TQ
    n_seg = seq // CMP_STRIDE
    nc = max(LANES, n_seg)
    cols = NSA_HEADS * TQ
    mselt = _cmp_to_sel_matrix(nc, LANES).T
    seg_w = CMP_STRIDE * NSA_GROUPS * HEAD_DIM
    vm = (2 * (seq * 768 * 4 + bias_t.size * 4 + wbig.size * 2 + cols * nc * 4) + 2 * nc * seg_w * 4
          + seq * LANES * 4 + 8 * cols * LANES * 4 + (12 << 20))
    return pl.pallas_call(
        functools.partial(_nsa_prompt_body, seq=seq),
        grid=(batch, nq),
        in_specs=[pl.BlockSpec((TQ, qa.shape[1]), lambda b, q: (b * nq + q, 0)),
                  pl.BlockSpec((1, seq, kvn.shape[2]), lambda b, q: (b, 0, 0)),
                  pl.BlockSpec((1, seq, kvw.shape[2]), lambda b, q: (b, 0, 0)),
                  pl.BlockSpec((TQ, LANES), lambda b, q: (b * nq + q, 0)),
                  pl.BlockSpec((1, nc, cols), lambda b, q: (q, 0, 0)),
                  pl.BlockSpec(bias_t.shape, lambda b, q: (0, 0, 0)),
                  pl.BlockSpec(wbig.shape, lambda b, q: (0, 0, 0, 0)),
                  pl.BlockSpec(pe.shape, lambda b, q: (0, 0, 0, 0)),
                  pl.BlockSpec(mselt.shape, lambda b, q: (0, 0))],
        out_specs=pl.BlockSpec((TQ, qa.shape[1]), lambda b, q: (b * nq + q, 0)),
        out_shape=jax.ShapeDtypeStruct(qa.shape, f32),
        scratch_shapes=[pltpu.VMEM((seq, LANES), f32), pltpu.VMEM((2, nc, seg_w), f32),
                        pltpu.VMEM((nc, LANES), f32), pltpu.VMEM((nc, LANES), f32),
                        pltpu.VMEM((LANES, cols), _MXU), pltpu.VMEM((LANES, cols), f32),
                        pltpu.VMEM((LANES, cols), f32), pltpu.VMEM((LANES, cols), f32),
                        pltpu.VMEM((1, cols), f32), pltpu.VMEM((1, cols), f32), pltpu.VMEM((LANES, cols), f32)],
        compiler_params=_vmem(vm),
        name="nsa_prompt",
    )(qa, kvn, kvw, ga, bias_c, bias_t, wbig, pe, mselt)


def _moba_prompt_body(qb_ref, kvb_ref, bt_ref, o_ref, kmean_s, q8_s, sel8_s, m_s, l_s, acc_s, *, seq):
    qt = pl.program_id(1)
    t0 = qt * TQ
    nb = seq // MOBA_BLK
    nr = _round_up(nb, SUBLANES)
    cols = MOBA_HEADS * TQ
    kw = MOBA_HEADS * HEAD_DIM
    scale = HEAD_DIM ** -0.5
    pairs = MOBA_HEADS // 2

    @pl.when(qt == 0)
    def _():
        kmean_s[...] = jnp.zeros(kmean_s.shape, f32)
        for n in range(nb):
            blk = kvb_ref[0, n * MOBA_BLK:(n + 1) * MOBA_BLK, 0:kw]
            kmean_s[n:n + 1, :] = jnp.sum(blk, axis=0, keepdims=True) * (1.0 / MOBA_BLK)
        sel8_s[...] = jnp.zeros(sel8_s.shape, f32)

    qtr = qb_ref[...].T
    zero = jnp.zeros((HEAD_DIM, TQ), f32)
    nblk = _iota((nr, TQ), 0)
    own = (t0 + _iota((nr, TQ), 1)) >> 8
    past = nblk < own
    for h in range(MOBA_HEADS):
        qh = qtr[h * HEAD_DIM:(h + 1) * HEAD_DIM, :]
        blk = jnp.concatenate([qh, zero] if h % 2 == 0 else [zero, qh], axis=0)
        q8_s[:, h * TQ:(h + 1) * TQ] = (blk * scale).astype(q8_s.dtype)
        gate = _dot_hi(kmean_s[:, (h // 2) * LANES:(h // 2 + 1) * LANES], blk)[0:nr, :]
        score = jnp.where(past, gate, -jnp.inf)
        sel = (past & (_rank_rows(score, nb) < MOBA_TOPK)) | (nblk == own)
        sel8_s[0:nr, h * TQ:(h + 1) * TQ] = sel.astype(f32)

    tpos = t0 + (_iota((1, cols), 1) & (TQ - 1))
    krow = _iota((TQ, cols), 0)

    def step(kt, carry):
        s_parts = []
        for pr in range(pairs):
            k = kvb_ref[0, pl.ds(kt * TQ, TQ), pr * LANES:(pr + 1) * LANES]
            s_parts.append(_dot(k, q8_s[:, 2 * pr * TQ:2 * (pr + 1) * TQ]))
        s = jnp.concatenate(s_parts, axis=1) + bt_ref[qt - kt]
        chosen = sel8_s[pl.ds((kt * TQ) >> 8, 1), :] > 0.5
        mask = chosen & (tpos >= kt * TQ + krow)
        s = jnp.where(mask, s, NEG)
        m_old = m_s[...]
        m_new = jnp.maximum(m_old, jnp.max(s, axis=0, keepdims=True))
        alpha = jnp.exp(m_old - m_new)
        p = jnp.where(mask, jnp.exp(s - m_new), 0.0)
        l_s[...] = alpha * l_s[...] + jnp.sum(p, axis=0, keepdims=True)
        pv = []
        for pr in range(pairs):
            v = kvb_ref[0, pl.ds(kt * TQ, TQ), kw + pr * LANES:kw + (pr + 1) * LANES]
            pv.append(_dot_tn(v, p[:, 2 * pr * TQ:2 * (pr + 1) * TQ]))
        acc_s[...] = alpha * acc_s[...] + jnp.concatenate(pv, axis=1)
        m_s[...] = m_new
        return carry

    _softmax_reset(m_s, l_s, acc_s)
    lax.fori_loop(0, qt + 1, step, 0)
    o = acc_s[...] / l_s[...]
    pieces = []
    for h in range(MOBA_HEADS):
        half = h % 2
        pieces.append(o[half * HEAD_DIM:(half + 1) * HEAD_DIM, h * TQ:(h + 1) * TQ])
    o_ref[...] = jnp.concatenate(pieces, axis=0).T


def _moba_prompt(qb, kvb, bias_t, batch, seq):
    nq = seq // TQ
    cols = MOBA_HEADS * TQ
    vm = 2 * (seq * kvb.shape[2] * 4 + bias_t.size * 4) + 8 * cols * LANES * 4 + (12 << 20)
    return pl.pallas_call(
        functools.partial(_moba_prompt_body, seq=seq),
        grid=(batch, nq),
        in_specs=[pl.BlockSpec((TQ, qb.shape[1]), lambda b, q: (b * nq + q, 0)),
                  pl.BlockSpec((1, seq, kvb.shape[2]), lambda b, q: (b, 0, 0)),
                  pl.BlockSpec(bias_t.shape, lambda b, q: (0, 0, 0))],
        out_specs=pl.BlockSpec((TQ, qb.shape[1]), lambda b, q: (b * nq + q, 0)),
        out_shape=jax.ShapeDtypeStruct(qb.shape, f32),
        scratch_shapes=[pltpu.VMEM((LANES, MOBA_HEADS * HEAD_DIM), f32),
                        pltpu.VMEM((LANES, cols), _MXU), pltpu.VMEM((LANES, cols), f32),
                        pltpu.VMEM((1, cols), f32), pltpu.VMEM((1, cols), f32), pltpu.VMEM((LANES, cols), f32)],
        compiler_params=_vmem(vm),
        name="moba_prompt",
    )(qb, kvb, bias_t)


def _page_group(n_pages):
    for g in (8, 4, 2):
        if n_pages % g == 0:
            return g
    raise ValueError("the page count must be even")


def _stack_heads_nsa(q, q8_s):
    half = _iota((1, LANES), 1) >> 6
    for h in range(NSA_HEADS):
        g = h // NSA_HPG
        blk = q[:, (h // 2) * LANES:(h // 2 + 1) * LANES]
        if h % 2 != g:
            blk = pltpu.roll(blk, HEAD_DIM, 1)
        q8_s[h:h + 1, :] = jnp.where(half == g, blk, 0.0).astype(q8_s.dtype)


def _nsa_sample_body(pt_ref, *refs, n_pages, group):
    del pt_ref
    ka_refs, vb_refs = refs[:group], refs[group:2 * group]
    (qa_ref, kvn_ref, kvw_ref, st_ref, ga_ref, bc_ref, bs_ref, bn_ref, bw_ref, wbig_ref, pe_ref, msel_ref, o_ref,
     xflat_s, kc_s, vc_s, q8_s, sel8_s, ocmp_s, m_s, l_s, acc_s) = refs[2 * group:]
    ph = pl.program_id(1)
    p = pl.program_id(2)
    n_steps = n_pages // group
    t = n_pages * PAGE
    seg_per_page = PAGE // CMP_STRIDE
    n_seg = n_pages * seg_per_page
    n_cmp = n_seg - CMP_BLK // CMP_STRIDE + 1
    cur = t // SEL_BLK
    n_sel = cur + 1
    nc = kc_s.shape[0]
    ls = sel8_s.shape[1]
    keys = group * PAGE
    scale = HEAD_DIM ** -0.5

    def one_key(q8, row):
        return _dot_nt(q8, jnp.broadcast_to(row, (SUBLANES, LANES)))[:, 0:1]

    @pl.when(ph == 0)
    def _():
        if nc > n_seg:
            @pl.when(p == 0)
            def _():
                xflat_s[...] = jnp.zeros(xflat_s.shape, f32)
        for c, page_refs in ((0, ka_refs), (1, vb_refs)):
            for g, ref in enumerate(page_refs):
                row0 = pl.multiple_of((p * group + g) * seg_per_page, SUBLANES)
                for r in range(CMP_STRIDE):
                    xflat_s[c, pl.ds(row0, seg_per_page), r * LANES:(r + 1) * LANES] = (
                        ref[0, pl.ds(r, seg_per_page, stride=CMP_STRIDE), :])

        @pl.when(p == n_steps - 1)
        def _():
            kc_s[...] = _compress(xflat_s, 0, pe_ref, wbig_ref)
            vc_s[...] = _compress(xflat_s, 1, pe_ref, wbig_ref)
            _stack_heads_nsa(qa_ref[0] * scale, q8_s)
            q8 = q8_s[...]
            ncol = _iota((NSA_HEADS, nc), 1)
            mask_c = ncol < n_cmp
            sc = jnp.where(mask_c, _dot_nt(q8, kc_s[...]) + bc_ref[...], NEG)
            e = jnp.exp(sc - jnp.max(sc, axis=1, keepdims=True))
            p_c = jnp.where(mask_c, e / jnp.sum(e, axis=1, keepdims=True), 0.0)
            ocmp_s[...] = _dot(p_c, vc_s[...])
            same_group = ((_iota((NSA_HEADS, NSA_HEADS), 0) >> 2) == (_iota((NSA_HEADS, NSA_HEADS), 1) >> 2)).astype(f32)
            blk = _dot_hi(_dot_hi(same_group, p_c), msel_ref[...])
            jb = _iota((NSA_HEADS, ls), 1)
            valid = jb < n_sel
            forced = valid & ((jb == 0) | (jb == cur) | (jb == cur - 1))
            score = jnp.where(forced, jnp.inf, jnp.where(valid, blk, -jnp.inf))
            sel8_s[...] = (valid & (_rank_lanes(score, n_sel) < N_SEL)).astype(sel8_s.dtype)

    @pl.when(ph == 1)
    def _():
        q8 = q8_s[...]

        @pl.when(p == 0)
        def _():
            m_s[...] = one_key(q8, kvn_ref[0, :, 2 * LANES:3 * LANES]) + bn_ref[:, 0:1]
            l_s[...] = jnp.ones(l_s.shape, f32)
            v_new = kvn_ref[0, :, 3 * LANES:4 * LANES].astype(_MXU).astype(f32)
            acc_s[...] = jnp.broadcast_to(v_new, acc_s.shape)

        k = jnp.concatenate([r[0] for r in ka_refs], axis=0)
        v = jnp.concatenate([r[0] for r in vb_refs], axis=0)
        s = _dot_nt(q8, k) + bs_ref[0]
        jrow = _iota((ls, keys), 0)
        jcol = (p * keys + _iota((ls, keys), 1)) >> 6
        chosen = _dot(sel8_s[...], (jrow == jcol).astype(f32)) > 0.5
        _softmax_step(s, chosen, v, m_s, l_s, acc_s)

        @pl.when(p == n_steps - 1)
        def _():
            osel = acc_s[...] / l_s[...]
            sw = _dot_nt(q8, st_ref[0, :, 0:LANES]) + bw_ref[...]
            sn = one_key(q8, kvw_ref[0, :, 0:LANES]) + bn_ref[:, 0:1]
            mw = jnp.maximum(jnp.max(sw, axis=1, keepdims=True), sn)
            ew = jnp.exp(sw - mw)
            en = jnp.exp(sn - mw)
            v_new = kvw_ref[0, :, LANES:2 * LANES].astype(_MXU).astype(f32)
            owin = (_dot(ew, st_ref[0, :, LANES:2 * LANES]) + en * v_new) / (jnp.sum(ew, axis=1, keepdims=True) + en)
            ga = ga_ref[0]
            for h in range(NSA_HEADS):
                g = h // NSA_HPG
                o = (ga[:, 3 * h:3 * h + 1] * ocmp_s[h:h + 1, :] + ga[:, 3 * h + 1:3 * h + 2] * osel[h:h + 1, :]
                     + ga[:, 3 * h + 2:3 * h + 3] * owin[h:h + 1, :])
                o_ref[0, :, h * HEAD_DIM:(h + 1) * HEAD_DIM] = o[:, g * HEAD_DIM:(g + 1) * HEAD_DIM]


def _nsa_sample(page_table, cache, qa, kvn, kvw, state, ga, tab, wbig, pe):
    bs, n_pages = page_table.shape
    group = _page_group(n_pages)
    n_steps = n_pages // group
    t = n_pages * PAGE
    win = state.shape[1]
    n_cmp = n_pages * (PAGE // CMP_STRIDE) - CMP_BLK // CMP_STRIDE + 1
    nc = max(LANES, n_pages * (PAGE // CMP_STRIDE))
    ls = _round_up(t // SEL_BLK + 1, LANES)
    msel = _cmp_to_sel_matrix(nc, ls)
    d_last = t - (CMP_STRIDE * (n_cmp - 1) + CMP_BLK - 1)
    bias_c = tab[:, d_last:t - CMP_BLK + 2:CMP_STRIDE][:, ::-1]
    bias_c = jnp.pad(bias_c, ((0, 0), (0, nc - n_cmp)))
    bias_s = jnp.transpose(tab[:, 1:t + 1][:, ::-1].reshape(NSA_HEADS, n_steps, group * PAGE), (1, 0, 2))
    bias_n = jnp.broadcast_to(tab[:, 0:1], (NSA_HEADS, LANES))
    bias_w = tab[:, 1:win + 1][:, ::-1]
    seg_w = CMP_STRIDE * NSA_GROUPS * HEAD_DIM
    req = lambda b, ph, p, pt: (b, 0, 0)
    c2 = lambda b, ph, p, pt: (0, 0)
    c4 = lambda b, ph, p, pt: (0, 0, 0, 0)

    def page_spec(g, lane_block):
        return pl.BlockSpec((1, PAGE, LANES), lambda b, ph, p, pt: (pt[b, p * group + g], 0, 2 * ph + lane_block))

    grid_spec = pltpu.PrefetchScalarGridSpec(
        num_scalar_prefetch=1,
        grid=(bs, 2, n_steps),
        in_specs=([page_spec(g, 0) for g in range(group)] + [page_spec(g, 1) for g in range(group)]
                  + [pl.BlockSpec((1, 1, qa.shape[2]), req), pl.BlockSpec((1, 1, kvn.shape[2]), req),
                     pl.BlockSpec((1, 1, kvw.shape[2]), req), pl.BlockSpec((1, win, state.shape[2]), req),
                     pl.BlockSpec((1, 1, LANES), req),
                     pl.BlockSpec(bias_c.shape, c2),
                     pl.BlockSpec((1, NSA_HEADS, group * PAGE), lambda b, ph, p, pt: (p * ph, 0, 0)),
                     pl.BlockSpec(bias_n.shape, c2), pl.BlockSpec(bias_w.shape, c2),
                     pl.BlockSpec(wbig.shape, c4), pl.BlockSpec(pe.shape, c4), pl.BlockSpec(msel.shape, c2)]),
        out_specs=pl.BlockSpec((1, 1, qa.shape[2]), req),
        scratch_shapes=[pltpu.VMEM((2, nc, seg_w), f32), pltpu.VMEM((nc, LANES), f32), pltpu.VMEM((nc, LANES), f32),
                        pltpu.VMEM((NSA_HEADS, LANES), _MXU), pltpu.VMEM((NSA_HEADS, ls), _MXU),
                        pltpu.VMEM((NSA_HEADS, LANES), f32),
                        pltpu.VMEM((NSA_HEADS, 1), f32), pltpu.VMEM((NSA_HEADS, 1), f32),
                        pltpu.VMEM((NSA_HEADS, LANES), f32)])
    vm = (2 * nc * seg_w * 4 + 2 * (wbig.size * 2 + msel.size * 4 + win * 256 * 4 + 2 * group * PAGE * LANES * 4)
          + (16 << 20))
    return pl.pallas_call(
        functools.partial(_nsa_sample_body, n_pages=n_pages, group=group),
        grid_spec=grid_spec,
        out_shape=jax.ShapeDtypeStruct(qa.shape, f32),
        compiler_params=_vmem(vm),
        name="nsa_sample",
    )(page_table, *([cache] * (2 * group)), qa, kvn, kvw, state, ga, bias_c, bias_s, bias_n, bias_w, wbig, pe, msel)


def _moba_sample_body(pt_ref, *refs, n_pages, group):
    del pt_ref
    k_refs, v_refs = refs[:group], refs[group:2 * group]
    qb_ref, kvb_ref, bs_ref, bn_ref, o_ref, kmean_s, q8_s, sel8_s, m_s, l_s, acc_s = refs[2 * group:]
    ph = pl.program_id(1)
    p = pl.program_id(2)
    n_steps = n_pages // group
    t = n_pages * PAGE
    kw = MOBA_HEADS * HEAD_DIM
    pages_per_blk = MOBA_BLK // PAGE
    blks_per_step = group // pages_per_blk
    own = t // MOBA_BLK
    keys = group * PAGE
    scale = HEAD_DIM ** -0.5
    head_lane = (_iota((MOBA_HEADS, kw), 1) >> 6) == _iota((MOBA_HEADS, kw), 0)

    @pl.when(ph == 0)
    def _():
        @pl.when(p == 0)
        def _():
            kmean_s[...] = jnp.zeros(kmean_s.shape, f32)

        for j in range(blks_per_step):
            tot = jnp.sum(k_refs[j * pages_per_blk][0], axis=0, keepdims=True)
            for i in range(1, pages_per_blk):
                tot = tot + jnp.sum(k_refs[j * pages_per_blk + i][0], axis=0, keepdims=True)
            kmean_s[pl.ds(p * blks_per_step + j, 1), :] = tot * (1.0 / MOBA_BLK)

        @pl.when(p == n_steps - 1)
        def _():
            q8 = jnp.where(head_lane, jnp.broadcast_to(qb_ref[0], (MOBA_HEADS, kw)), 0.0)
            q8_s[...] = q8
            gate = _dot_nt_hi(q8, kmean_s[...])
            nblk = _iota((MOBA_HEADS, LANES), 1)
            past = nblk < own
            score = jnp.where(past, gate, -jnp.inf)
            sel8_s[...] = (past & (_rank_lanes(score, own) < MOBA_TOPK)).astype(sel8_s.dtype)

    @pl.when(ph == 1)
    def _():
        q8 = q8_s[...] * scale

        @pl.when(p == 0)
        def _():
            k_new = jnp.broadcast_to(kvb_ref[0, :, 0:kw], (SUBLANES, kw))
            m_s[...] = _dot_nt(q8, k_new)[:, 0:1] + bn_ref[:, 0:1]
            l_s[...] = jnp.ones(l_s.shape, f32)
            acc_s[...] = jnp.broadcast_to(kvb_ref[0, :, kw:2 * kw].astype(_MXU).astype(f32), acc_s.shape)

        k = jnp.concatenate([r[0] for r in k_refs], axis=0)
        v = jnp.concatenate([r[0] for r in v_refs], axis=0)
        s = _dot_nt(q8, k) + bs_ref[0]
        jcol = (p * keys + _iota((LANES, keys), 1)) >> 8
        chosen = _dot(sel8_s[...], (_iota((LANES, keys), 0) == jcol).astype(f32)) > 0.5
        _softmax_step(s, chosen, v, m_s, l_s, acc_s)

        @pl.when(p == n_steps - 1)
        def _():
            o8 = jnp.where(head_lane, acc_s[...] / l_s[...], 0.0)
            o_ref[0] = jnp.sum(o8, axis=0, keepdims=True)


def _moba_sample(page_table, cache, qb, kvb, tab):
    bs, n_pages = page_table.shape
    group = _page_group(n_pages)
    n_steps = n_pages // group
    t = n_pages * PAGE
    kw = MOBA_HEADS * HEAD_DIM
    bias_s = jnp.transpose(tab[:, 1:t + 1][:, ::-1].reshape(MOBA_HEADS, n_steps, group * PAGE), (1, 0, 2))
    bias_n = jnp.broadcast_to(tab[:, 0:1], (MOBA_HEADS, LANES))
    req = lambda b, ph, p, pt: (b, 0, 0)

    def k_spec(g):
        return pl.BlockSpec((1, PAGE, kw), lambda b, ph, p, pt: (pt[b, p * group + g], 0, 0))

    def v_spec(g):
        return pl.BlockSpec((1, PAGE, kw), lambda b, ph, p, pt: (pt[b, (p * group + g) * ph], 0, 1))

    grid_spec = pltpu.PrefetchScalarGridSpec(
        num_scalar_prefetch=1,
        grid=(bs, 2, n_steps),
        in_specs=([k_spec(g) for g in range(group)] + [v_spec(g) for g in range(group)]
                  + [pl.BlockSpec((1, 1, kw), req), pl.BlockSpec((1, 1, 2 * kw), req),
                     pl.BlockSpec((1, MOBA_HEADS, group * PAGE), lambda b, ph, p, pt: (p * ph, 0, 0)),
                     pl.BlockSpec(bias_n.shape, lambda b, ph, p, pt: (0, 0))]),
        out_specs=pl.BlockSpec((1, 1, kw), req),
        scratch_shapes=[pltpu.VMEM((LANES, kw), f32), pltpu.VMEM((MOBA_HEADS, kw), f32),
                        pltpu.VMEM((MOBA_HEADS, LANES), _MXU),
                        pltpu.VMEM((MOBA_HEADS, 1), f32), pltpu.VMEM((MOBA_HEADS, 1), f32),
                        pltpu.VMEM((MOBA_HEADS, kw), f32)])
    return pl.pallas_call(
        functools.partial(_moba_sample_body, n_pages=n_pages, group=group),
        grid_spec=grid_spec,
        out_shape=jax.ShapeDtypeStruct(qb.shape, f32),
        compiler_params=_vmem(2 * 2 * group * PAGE * kw * 4 + (24 << 20)),
        name="moba_sample",
    )(page_table, *([cache] * (2 * group)), qb, kvb, bias_s, bias_n)


def _layer_norm(x, g, b):
    mu = jnp.mean(x, axis=-1, keepdims=True)
    xc = x - mu
    var = jnp.mean(xc * xc, axis=-1, keepdims=True)
    return xc * lax.rsqrt(var + LN_EPS) * g + b


def _merge_body(oa_ref, ob_ref, gm_ref, x_ref, wa_ref, wb_ref, wo_ref, g_ref, b_ref, h_ref, ht_ref, *, alpha):
    d = x_ref.shape[1]
    ya = _dot(oa_ref[...], wa_ref[...])
    yb = _dot(ob_ref[...], wb_ref[...])
    gm = gm_ref[...]
    mix = _dot(gm[:, :d] * ya + gm[:, d:] * yb, wo_ref[...])
    h = _layer_norm(alpha * x_ref[...] + mix, g_ref[...], b_ref[...])
    h_ref[...] = h
    ht_ref[...] = h.T.astype(ht_ref.dtype)


def _merge(oa, ob, gm, x2d, wa, wb, wo, g, b, alpha):
    n, d = x2d.shape
    tm = 256 if n % 256 == 0 else n
    row = lambda i: (i, 0)
    full = lambda i: (0, 0)
    vm = 2 * (tm * (oa.shape[1] + ob.shape[1] + gm.shape[1] + 2 * d) * 4 + tm * d * 2
              + (wa.size + wb.size + wo.size) * 2) + (16 << 20)
    return pl.pallas_call(
        functools.partial(_merge_body, alpha=alpha),
        grid=(n // tm,),
        in_specs=[pl.BlockSpec((tm, oa.shape[1]), row), pl.BlockSpec((tm, ob.shape[1]), row),
                  pl.BlockSpec((tm, gm.shape[1]), row), pl.BlockSpec((tm, d), row),
                  pl.BlockSpec(wa.shape, full), pl.BlockSpec(wb.shape, full), pl.BlockSpec(wo.shape, full),
                  pl.BlockSpec((1, d), full), pl.BlockSpec((1, d), full)],
        out_specs=[pl.BlockSpec((tm, d), row), pl.BlockSpec((d, tm), lambda i: (0, i))],
        out_shape=[jax.ShapeDtypeStruct((n, d), f32), jax.ShapeDtypeStruct((d, n), _MXU)],
        compiler_params=_vmem(vm),
        name="merge",
    )(oa, ob, gm, x2d, wa, wb, wo, g, b)


_NOT_TOP = 99.0
_CAND_B = tuple(PEER_TOPK // (a + 1) for a in range(SUBLANES))


def _extract_top(work_s, n_rounds, on_round):
    rows = _iota(work_s.shape, 0)

    def body(r, carry):
        w = work_s[...]
        m = jnp.max(w, axis=0, keepdims=True)
        idx = jnp.min(jnp.where(w == m, rows, work_s.shape[0]), axis=0, keepdims=True)
        hit = rows == idx
        on_round(r, m, hit)
        work_s[...] = jnp.where(hit, -jnp.inf, w)
        return carry

    lax.fori_loop(0, n_rounds, body, 0)


def _peer_select(ht_ref, wqt_ref, keys_ref, qt_s, rank_s, exp_s, sv_s, work_s, cwork_s, cand_s, cnt_s, f0_s):
    tn = ht_ref.shape[1]
    nk = PEER_N_KEYS
    qt_s[...] = jnp.dot(wqt_ref[...], ht_ref[...], preferred_element_type=f32)

    def per_half(hc, carry):
        s = _dot(keys_ref[hc], qt_s[pl.ds(pl.multiple_of(hc * PEER_HALF, PEER_HALF), PEER_HALF), :])
        work_s[...] = s
        rank_s[hc] = jnp.full((nk, tn), _NOT_TOP, f32)

        def on_round(r, m, hit):
            sv_s[hc, pl.ds(r, 1), :] = m
            rank_s[hc] = jnp.where(hit, r.astype(f32), rank_s[hc])

        _extract_top(work_s, PEER_TOPK, on_round)
        exp_s[hc] = jnp.exp(s - sv_s[hc, 0:1, :])
        return carry

    lax.fori_loop(0, 2 * PEER_HEADS, per_half, 0)

    n_cand = cand_s.shape[0]

    def per_head(h, carry):
        sv0 = sv_s[2 * h]
        sv1 = sv_s[2 * h + 1]
        e0 = jnp.exp(sv0 - sv0[0:1, :])
        e1 = jnp.exp(sv1 - sv1[0:1, :])
        brow = _iota((SUBLANES, tn), 0)
        tiles, etiles = [sv0[0:1, :] + sv1, ], [e0[0:1, :] * e1]
        for a in range(1, SUBLANES):
            ok = brow < _CAND_B[a]
            tiles.append(jnp.where(ok, sv0[a:a + 1, :] + sv1[0:SUBLANES, :], -jnp.inf))
            etiles.append(jnp.where(ok, e0[a:a + 1, :] * e1[0:SUBLANES, :], 0.0))
        tiles.append(sv0[SUBLANES:, :] + sv1[0:1, :])
        etiles.append(e0[SUBLANES:, :] * e1[0:1, :])
        cwork_s[...] = jnp.concatenate(tiles, axis=0)
        cand_s[...] = jnp.zeros(cand_s.shape, f32)

        def on_round(r, m, hit):
            cand_s[...] = jnp.where(hit, 1.0, cand_s[...])

        _extract_top(cwork_s, PEER_TOPK, on_round)
        chosen = cand_s[...]
        z = jnp.sum(chosen * jnp.concatenate(etiles, axis=0), axis=0, keepdims=True)
        cnts = [jnp.sum(chosen[0:PEER_TOPK], axis=0, keepdims=True)]
        for a in range(1, SUBLANES):
            r0 = PEER_TOPK + (a - 1) * SUBLANES
            cnts.append(jnp.sum(chosen[r0:r0 + SUBLANES], axis=0, keepdims=True))
        tail = chosen[n_cand - SUBLANES:]
        cnts += [tail[a:a + 1, :] for a in range(SUBLANES)]
        ra = rank_s[2 * h]
        c0 = jnp.zeros((nk, tn), f32)
        for a in range(PEER_TOPK):
            c0 = jnp.where(ra == float(a), cnts[a], c0)
        cnt_s[h] = c0
        f0_s[h] = exp_s[2 * h] / z
        return carry

    lax.fori_loop(0, PEER_HEADS, per_head, 0)


def _peer_body(ht_ref, h_ref, wqt_ref, keys_ref, u_ref, vt_ref, g_ref, b_ref, y_ref,
               qt_s, rank_s, exp_s, sv_s, work_s, cwork_s, cand_s, cnt_s, f0_s, acc_s, *, alpha, ti):
    e = pl.program_id(1)
    tn = ht_ref.shape[1]

    @pl.when(e == 0)
    def _():
        _peer_select(ht_ref, wqt_ref, keys_ref, qt_s, rank_s, exp_s, sv_s, work_s, cwork_s, cand_s, cnt_s, f0_s)
        acc_s[...] = jnp.zeros(acc_s.shape, f32)

    at = jnp.dot(u_ref[...], ht_ref[...], preferred_element_type=f32)
    gel = 0.5 * at * (1.0 + lax.erf(at * math.sqrt(0.5)))
    parts = []
    for ii in range(ti):
        i = e * ti + ii
        w = jnp.zeros((PEER_N_KEYS, tn), f32)
        for h in range(PEER_HEADS):
            cnt = cnt_s[h, pl.ds(i, 1), :]
            f0 = f0_s[h, pl.ds(i, 1), :]
            w = w + jnp.where(rank_s[2 * h + 1] < cnt, exp_s[2 * h + 1], 0.0) * f0
        parts.append((w * gel[ii * PEER_N_KEYS:(ii + 1) * PEER_N_KEYS, :]).astype(_MXU))
    acc_s[...] += jnp.dot(vt_ref[...], jnp.concatenate(parts, axis=0), preferred_element_type=f32)

    @pl.when(e == pl.num_programs(1) - 1)
    def _():
        y_ref[...] = _layer_norm(alpha * h_ref[...] + acc_s[...].T, g_ref[...], b_ref[...])


def _peer(ht, h, wqt, keys, u, vt, g, b, alpha):
    d, n = ht.shape
    n_exp = u.shape[0]
    tn = 512 if n % 512 == 0 else n
    ti = 4
    te = ti * PEER_N_KEYS
    n_half = 2 * PEER_HEADS
    n_cand = PEER_TOPK + (SUBLANES - 1) * SUBLANES + SUBLANES
    tok = lambda t, e: (0, t)
    full2 = lambda t, e: (0, 0)
    scratch = [pltpu.VMEM((wqt.shape[0], tn), f32),
               pltpu.VMEM((n_half, PEER_N_KEYS, tn), f32), pltpu.VMEM((n_half, PEER_N_KEYS, tn), f32),
               pltpu.VMEM((n_half, PEER_TOPK, tn), f32), pltpu.VMEM((PEER_N_KEYS, tn), f32),
               pltpu.VMEM((n_cand, tn), f32), pltpu.VMEM((n_cand, tn), f32),
               pltpu.VMEM((PEER_HEADS, PEER_N_KEYS, tn), f32), pltpu.VMEM((PEER_HEADS, PEER_N_KEYS, tn), f32),
               pltpu.VMEM((d, tn), f32)]
    vm = (wqt.shape[0] * tn * 4 + 2 * n_half * PEER_N_KEYS * tn * 4 + 2 * PEER_HEADS * PEER_N_KEYS * tn * 4
          + d * tn * 4 + 2 * (d * tn * 2 + 2 * tn * d * 4 + wqt.size * 2 + keys.size * 2 + 2 * te * d * 2)
          + (14 << 20))
    return pl.pallas_call(
        functools.partial(_peer_body, alpha=alpha, ti=ti),
        grid=(n // tn, n_exp // te),
        in_specs=[pl.BlockSpec((d, tn), tok),
                  pl.BlockSpec((tn, d), lambda t, e: (t, 0)),
                  pl.BlockSpec(wqt.shape, full2),
                  pl.BlockSpec(keys.shape, lambda t, e: (0, 0, 0)),
                  pl.BlockSpec((te, d), lambda t, e: (e, 0)),
                  pl.BlockSpec((d, te), lambda t, e: (0, e)),
                  pl.BlockSpec((1, d), full2), pl.BlockSpec((1, d), full2)],
        out_specs=pl.BlockSpec((tn, d), lambda t, e: (t, 0)),
        out_shape=jax.ShapeDtypeStruct((n, d), f32),
        scratch_shapes=scratch,
        compiler_params=_vmem(vm),
        name="peer",
    )(ht, h, wqt, keys, u, vt, g, b)


def kernel(x_prompt, x_sample, cache_nsa, cache_moba, state_nsa_win, page_table, w_in, nsa_cmp_pe, nsa_cmp_w, w_branch_a, w_branch_b, w_out, ln1_g, ln1_b, peer_wq, peer_keys, peer_u, peer_v, ln2_g, ln2_b, rel_bias):
    depth = w_in.shape[0]
    batch, seq, d = x_prompt.shape
    bs, dec_seq, _ = x_sample.shape
    n_pages = page_table.shape[1]
    past = n_pages * PAGE
    win_len = state_nsa_win.shape[2]
    assert dec_seq == 1 and cache_nsa.shape[2] == PAGE and cache_moba.shape[2] == PAGE
    assert seq % MOBA_BLK == 0 and past % MOBA_BLK == 0 and win_len == WINDOW and seq % TQ == 0
    alpha = (2 * depth) ** 0.25
    nq = seq // TQ
    n_tok = batch * seq

    tab = _bias_by_distance(rel_bias, max(seq, past + 1) + TQ)
    tab_a, tab_b = tab[:NSA_HEADS], tab[NSA_HEADS:]
    tiles_a, tiles_b = _bias_tiles_t(tab_a, nq), _bias_tiles_t(tab_b, nq)
    cmp_bias = _nsa_cmp_bias_t(tab_a, seq, max(LANES, seq // CMP_STRIDE))

    xp = x_prompt.reshape(n_tok, d)
    xs = x_sample.reshape(bs, d)
    outs = [[] for _ in range(6)]
    for l in range(depth):
        w_packed = _pack_w_in(w_in[l], d)
        wbig, pe = _pack_cmp(nsa_cmp_pe[l], nsa_cmp_w[l])
        wa, wb, wo = w_branch_a[l].astype(_MXU), w_branch_b[l].astype(_MXU), w_out[l].astype(_MXU)
        wqt = peer_wq[l].T.astype(_MXU)
        keys = peer_keys[l].reshape(2 * PEER_HEADS, PEER_N_KEYS, PEER_HALF).astype(_MXU)
        u = peer_u[l].astype(_MXU)
        vt = peer_v[l].T.astype(_MXU)
        g1, b1, g2, b2 = ln1_g[l][None], ln1_b[l][None], ln2_g[l][None], ln2_b[l][None]

        def ffn(oa, ob, gm, x2d):
            h, ht = _merge(oa, ob, gm, x2d, wa, wb, wo, g1, b1, alpha)
            return _peer(ht, h, wqt, keys, u, vt, g2, b2, alpha)

        qa, kvn, kvw, qb, kvb, gm, ga = _project(xp, w_packed, d)
        oa = _nsa_prompt(qa, kvn.reshape(batch, seq, -1), kvw.reshape(batch, seq, -1), ga, cmp_bias, tiles_a,
                         wbig, pe, batch, seq)
        ob = _moba_prompt(qb, kvb.reshape(batch, seq, -1), tiles_b, batch, seq)
        xp = ffn(oa, ob, gm, xp)
        outs[0].append(kvn.reshape(batch, seq, 4, NSA_GROUPS, HEAD_DIM))
        outs[1].append(kvb.reshape(batch, seq, 2, MOBA_HEADS, HEAD_DIM))
        outs[2].append(kvw.reshape(batch, seq, 2, NSA_GROUPS, HEAD_DIM)[:, seq - min(WINDOW, seq):])

        qa, kvn, kvw, qb, kvb, gm, ga = _project(xs, w_packed, d)
        r3 = lambda a: a.reshape(bs, 1, -1)
        state = state_nsa_win[l].reshape(bs, win_len, -1)
        oa = _nsa_sample(page_table, cache_nsa[l].reshape(-1, PAGE, 4 * NSA_GROUPS * HEAD_DIM), r3(qa), r3(kvn),
                         r3(kvw), state, r3(ga), tab_a, wbig, pe)
        ob = _moba_sample(page_table, cache_moba[l].reshape(-1, PAGE, 2 * MOBA_HEADS * HEAD_DIM), r3(qb), r3(kvb),
                          tab_b)
        xs = ffn(oa.reshape(bs, -1), ob.reshape(bs, -1), gm, xs)
        outs[3].append(kvn.reshape(bs, 1, 4, NSA_GROUPS, HEAD_DIM))
        outs[4].append(kvb.reshape(bs, 1, 2, MOBA_HEADS, HEAD_DIM))
        win_new = jnp.concatenate([state_nsa_win[l][:, 1:], kvw.reshape(bs, 1, 2, NSA_GROUPS, HEAD_DIM)], axis=1)
        outs[5].append(win_new)

    stacked = [jnp.stack(o) for o in outs]
    return (xp.reshape(batch, seq, d), xs.reshape(bs, 1, d), *stacked)
```

```python
import functools
import math

import jax
import jax.numpy as jnp
from jax import lax
from jax.experimental import pallas as pl
from jax.experimental.pallas import tpu as pltpu

HEAD_DIM = 64
NSA_HEADS = 8
NSA_GROUPS = 2
NSA_HPG = NSA_HEADS // NSA_GROUPS
CMP_BLK = 32
CMP_STRIDE = 16
SEL_BLK = 64
N_SEL = 16
WINDOW = 512
MOBA_HEADS = 8
MOBA_BLK = 256
MOBA_TOPK = 3
N_BUCKETS = 32
MAX_DISTANCE = 1024
PEER_HEADS = 8
PEER_N_KEYS = 128
PEER_HALF = 128
PEER_TOPK = 16
LN_EPS = 1e-5
NEG = -1e30

LANES = 128
SUBLANES = 8
VMEM_BYTES_V7X = 64 * 1024 * 1024

TQ = 128
PAGE = 128

f32 = jnp.float32
i32 = jnp.int32
_MXU = jnp.bfloat16
_HI = lax.Precision.HIGHEST


def _iota(shape, dim):
    return lax.broadcasted_iota(i32, shape, dim)


def _dot(a, b):
    return jnp.dot(a.astype(_MXU), b.astype(_MXU), preferred_element_type=f32)


def _dot_nt(a, b):
    return lax.dot_general(a.astype(_MXU), b.astype(_MXU), (((1,), (1,)), ((), ())),
                           preferred_element_type=f32)


def _dot_tn(a, b):
    return lax.dot_general(a.astype(_MXU), b.astype(_MXU), (((0,), (0,)), ((), ())), preferred_element_type=f32)


def _dot_hi(a, b):
    return jnp.dot(a, b, precision=_HI, preferred_element_type=f32)


def _dot_nt_hi(a, b):
    return lax.dot_general(a, b, (((1,), (1,)), ((), ())), precision=_HI, preferred_element_type=f32)


def _vmem(nbytes, flags=None):
    return pltpu.CompilerParams(vmem_limit_bytes=int(min(nbytes, VMEM_BYTES_V7X - (4 << 20))), flags=flags)


def _round_up(x, m):
    return -(-x // m) * m


def _rank_lanes(s, n):
    width = s.shape[1]
    j = _iota(s.shape, 1)
    rank = jnp.zeros(s.shape, i32)
    for r in range(1, n):
        lo = pltpu.roll(s, r, 1)
        rank = rank + jnp.where((j >= r) & (lo >= s), 1, 0)
        hi = pltpu.roll(s, width - r, 1)
        rank = rank + jnp.where((j + r < n) & (hi > s), 1, 0)
    return rank


def _rank_rows(s, n):
    j = _iota(s.shape, 0)
    rank = jnp.zeros(s.shape, i32)
    for k in range(n):
        row = s[k:k + 1, :]
        rank = rank + jnp.where((row > s) | ((row == s) & (j > k)), 1, 0)
    return rank


def _softmax_reset(m_s, l_s, acc_s):
    m_s[...] = jnp.full(m_s.shape, NEG, f32)
    l_s[...] = jnp.zeros(l_s.shape, f32)
    acc_s[...] = jnp.zeros(acc_s.shape, f32)


def _softmax_step(s, mask, v, m_s, l_s, acc_s):
    s = jnp.where(mask, s, NEG)
    m_old = m_s[...]
    m_new = jnp.maximum(m_old, jnp.max(s, axis=1, keepdims=True))
    alpha = jnp.exp(m_old - m_new)
    p = jnp.where(mask, jnp.exp(s - m_new), 0.0)
    l_s[...] = alpha * l_s[...] + jnp.sum(p, axis=1, keepdims=True)
    acc_s[...] = alpha * acc_s[...] + _dot(p, v)
    m_s[...] = m_new


def _softmax_step_t(s, mask, v, m_s, l_s, acc_s):
    s = jnp.where(mask, s, NEG)
    m_old = m_s[...]
    m_new = jnp.maximum(m_old, jnp.max(s, axis=0, keepdims=True))
    alpha = jnp.exp(m_old - m_new)
    p = jnp.where(mask, jnp.exp(s - m_new), 0.0)
    l_s[...] = alpha * l_s[...] + jnp.sum(p, axis=0, keepdims=True)
    acc_s[...] = alpha * acc_s[...] + _dot_tn(v, p)
    m_s[...] = m_new


_W_QA = NSA_HEADS * HEAD_DIM
_W_KVN = 4 * NSA_GROUPS * HEAD_DIM
_W_KVW = 2 * NSA_GROUPS * HEAD_DIM
_W_GA = 3 * NSA_HEADS
_W_QB = MOBA_HEADS * HEAD_DIM
_W_KVB = 2 * MOBA_HEADS * HEAD_DIM


def _proj_body(x_ref, w_ref, qa_ref, kvn_ref, kvw_ref, qb_ref, kvb_ref, gm_ref, ga_ref):
    xb = x_ref[...].astype(_MXU)
    col = 0
    for ref, act in ((qa_ref, None), (kvn_ref, None), (kvw_ref, None), (qb_ref, None), (kvb_ref, None),
                     (gm_ref, jax.nn.sigmoid), (ga_ref, jax.nn.sigmoid)):
        width = ref.shape[1]
        y = jnp.dot(xb, w_ref[:, col:col + width], preferred_element_type=f32)
        ref[...] = y if act is None else act(y)
        col += width


def _project(x2d, w_packed, d_model):
    n = x2d.shape[0]
    tm = 256 if n % 256 == 0 else n
    widths = (_W_QA, _W_KVN, _W_KVW, _W_QB, _W_KVB, 2 * d_model, LANES)
    n_cols = sum(widths)
    return pl.pallas_call(
        _proj_body,
        grid=(n // tm,),
        in_specs=[pl.BlockSpec((tm, d_model), lambda i: (i, 0)),
                  pl.BlockSpec((d_model, n_cols), lambda i: (0, 0))],
        out_specs=[pl.BlockSpec((tm, w), lambda i: (i, 0)) for w in widths],
        out_shape=[jax.ShapeDtypeStruct((n, w), f32) for w in widths],
        compiler_params=_vmem(2 * (tm * d_model * 4 + d_model * n_cols * 2 + tm * n_cols * 4) + (8 << 20)),
        name="proj",
    )(x2d, w_packed)


def _pack_w_in(w_in, d_model):
    c_ga = _W_QA + _W_KVN + _W_KVW
    c_qb = c_ga + _W_GA
    pad = jnp.zeros((d_model, LANES - _W_GA), w_in.dtype)
    return jnp.concatenate([w_in[:, :c_ga], w_in[:, c_qb:], w_in[:, c_ga:c_qb], pad], axis=1).astype(_MXU)


def _t5_bucket(dist):
    n = jnp.maximum(dist, 0)
    max_exact = N_BUCKETS // 2
    nf = jnp.maximum(n, 1).astype(f32)
    large = max_exact + (jnp.log(nf / max_exact) / math.log(MAX_DISTANCE / max_exact)
                         * (N_BUCKETS - max_exact)).astype(i32)
    return jnp.where(n < max_exact, n, jnp.minimum(large, N_BUCKETS - 1)).astype(i32)


def _bias_by_distance(rel_bias, max_dist):
    return rel_bias[_t5_bucket(jnp.arange(max_dist, dtype=i32))].T


def _windows(v, starts, length):
    return jnp.stack([v[:, s:s + length] for s in starts], axis=1)


def _bias_tiles_t(tab, n_tiles):
    h = tab.shape[0]
    tabp = jnp.concatenate([jnp.zeros((h, TQ - 1), tab.dtype), tab], axis=1)
    g = _windows(tabp, range(TQ), n_tiles * TQ).reshape(h, TQ, n_tiles, TQ)[..., ::-1]
    return jnp.transpose(g, (2, 3, 0, 1)).reshape(n_tiles, TQ, h * TQ)


def _nsa_cmp_bias_t(tab, seq, nc):
    h = tab.shape[0]
    off = CMP_STRIDE * (nc - 1) + CMP_BLK - 1
    tabp = jnp.concatenate([jnp.zeros((h, off), tab.dtype), tab[:, :seq]], axis=1)
    w = _windows(tabp, [CMP_STRIDE * (nc - 1 - n) for n in range(nc)], seq)
    return jnp.transpose(w.reshape(h, nc, seq // TQ, TQ), (2, 1, 0, 3)).reshape(seq // TQ, nc, h * TQ)


def _pack_cmp(cmp_pe, cmp_w):
    r = CMP_BLK // CMP_STRIDE
    w = cmp_w.reshape(2, r, CMP_STRIDE, HEAD_DIM, HEAD_DIM)
    eye = jnp.eye(NSA_GROUPS, dtype=w.dtype)
    wbig = jnp.einsum("cmrde,gh->cmrgdhe", w, eye).reshape(2, r, CMP_STRIDE * NSA_GROUPS * HEAD_DIM,
                                                          NSA_GROUPS * HEAD_DIM)
    pe = cmp_pe.reshape(2, r, CMP_STRIDE, 1, HEAD_DIM)
    pe = jnp.broadcast_to(pe, (2, r, CMP_STRIDE, NSA_GROUPS, HEAD_DIM)).reshape(2, r, 1, -1)
    return wbig.astype(_MXU), pe.astype(f32)


def _cmp_to_sel_matrix(n_rows, n_cols):
    spb = SEL_BLK // CMP_STRIDE
    n = jnp.arange(n_rows, dtype=i32)[:, None]
    j = jnp.arange(n_cols, dtype=i32)[None, :]
    m = jnp.zeros((n_rows, n_cols), f32)
    for k in range(CMP_BLK // CMP_STRIDE):
        m = m + ((n + k) // spb == j).astype(f32)
    return m


def _compress(xflat_ref, c, pe_ref, wbig_ref):
    x = xflat_ref[c]
    a = _dot(x + pe_ref[c, 0], wbig_ref[c, 0])
    b = _dot(x + pe_ref[c, 1], wbig_ref[c, 1])
    return a + pltpu.roll(b, x.shape[0] - 1, 0)


def _nsa_prompt_body(qa_ref, kvn_ref, kvw_ref, ga_ref, bc_ref, bt_ref, wbig_ref, pe_ref, mselt_ref, o_ref,
                     rows_s, xflat_s, kc_s, vc_s, q8_s, sel8_s, ocmp_s, osel_s, m_s, l_s, acc_s, *, seq):
    qt = pl.program_id(1)
    t0 = qt * TQ
    n_seg = seq // CMP_STRIDE
    n_cmp = n_seg - CMP_BLK // CMP_STRIDE + 1
    n_sel = -(-seq // SEL_BLK)
    nr = _round_up(n_sel, SUBLANES)
    nc = kc_s.shape[0]
    cols = NSA_HEADS * TQ
    scale = HEAD_DIM ** -0.5

    @pl.when(qt == 0)
    def _():
        for c in range(2):
            if nc > n_seg:
                xflat_s[c] = jnp.zeros(xflat_s.shape[1:], f32)
            rows_s[...] = kvn_ref[0, :, c * LANES:(c + 1) * LANES]
            for r in range(CMP_STRIDE):
                xflat_s[c, 0:n_seg, r * LANES:(r + 1) * LANES] = rows_s[pl.ds(r, n_seg, stride=CMP_STRIDE), :]
        kc_s[...] = _compress(xflat_s, 0, pe_ref, wbig_ref)
        vc_s[...] = _compress(xflat_s, 1, pe_ref, wbig_ref)
        sel8_s[...] = jnp.zeros(sel8_s.shape, f32)

    qtr = (qa_ref[...] * scale).T
    zero = jnp.zeros((HEAD_DIM, TQ), f32)
    for h in range(NSA_HEADS):
        qh = qtr[h * HEAD_DIM:(h + 1) * HEAD_DIM, :]
        blk = jnp.concatenate([qh, zero] if h // NSA_HPG == 0 else [zero, qh], axis=0)
        q8_s[:, h * TQ:(h + 1) * TQ] = blk.astype(q8_s.dtype)
    q8 = q8_s[...]
    tpos = t0 + (_iota((1, cols), 1) & (TQ - 1))

    nrow = _iota((nc, cols), 0)
    mask_c = (nrow < n_cmp) & (tpos >= nrow * CMP_STRIDE + (CMP_BLK - 1))
    sc = jnp.where(mask_c, _dot(kc_s[...], q8) + bc_ref[0], NEG)
    e = jnp.exp(sc - jnp.max(sc, axis=0, keepdims=True))
    p_c = jnp.where(mask_c, e / jnp.sum(e, axis=0, keepdims=True), 0.0)
    ocmp_s[...] = _dot_tn(vc_s[...], p_c)

    jb = _iota((nr, TQ), 0)
    cur = (t0 + _iota((nr, TQ), 1)) >> 6
    valid = (jb <= cur) & (jb < n_sel)
    forced = valid & ((jb == 0) | (jb == cur) | (jb == cur - 1))
    for g in range(NSA_GROUPS):
        imp = p_c[:, g * NSA_HPG * TQ:(g * NSA_HPG + 1) * TQ]
        for h in range(1, NSA_HPG):
            imp = imp + p_c[:, (g * NSA_HPG + h) * TQ:(g * NSA_HPG + h + 1) * TQ]
        blk = _dot_hi(mselt_ref[...], imp)[0:nr, :]
        score = jnp.where(forced, jnp.inf, jnp.where(valid, blk, -jnp.inf))
        sel = (valid & (_rank_rows(score, n_sel) < N_SEL)).astype(f32)
        for h in range(NSA_HPG):
            sel8_s[0:nr, (g * NSA_HPG + h) * TQ:(g * NSA_HPG + h + 1) * TQ] = sel

    krow = _iota((TQ, cols), 0)
    blocks_per_tile = TQ // SEL_BLK

    def sel_step(kt, carry):
        k = kvn_ref[0, pl.ds(kt * TQ, TQ), 2 * LANES:3 * LANES]
        v = kvn_ref[0, pl.ds(kt * TQ, TQ), 3 * LANES:4 * LANES]
        chosen = sel8_s[pl.ds(kt * blocks_per_tile, 1), :]
        for i in range(1, blocks_per_tile):
            chosen = jnp.where(krow < i * SEL_BLK, chosen, sel8_s[pl.ds(kt * blocks_per_tile + i, 1), :])
        mask = (chosen > 0.5) & (tpos >= kt * TQ + krow)
        _softmax_step_t(_dot(k, q8) + bt_ref[qt - kt], mask, v, m_s, l_s, acc_s)
        return carry

    _softmax_reset(m_s, l_s, acc_s)
    lax.fori_loop(0, qt + 1, sel_step, 0)
    osel_s[...] = acc_s[...] / l_s[...]

    def win_step(kt, carry):
        k = kvw_ref[0, pl.ds(kt * TQ, TQ), 0:LANES]
        v = kvw_ref[0, pl.ds(kt * TQ, TQ), LANES:2 * LANES]
        d = tpos - (kt * TQ + krow)
        _softmax_step_t(_dot(k, q8) + bt_ref[qt - kt], (d >= 0) & (d <= WINDOW), v, m_s, l_s, acc_s)
        return carry

    _softmax_reset(m_s, l_s, acc_s)
    lax.fori_loop(jnp.maximum(qt - WINDOW // TQ, 0), qt + 1, win_step, 0)
    owin = acc_s[...] / l_s[...]

    gat = ga_ref[...].T
    pieces = []
    for h in range(NSA_HEADS):
        g = h // NSA_HPG
        c0, c1 = h * TQ, (h + 1) * TQ
        o = (gat[3 * h:3 * h + 1, :] * ocmp_s[:, c0:c1] + gat[3 * h + 1:3 * h + 2, :] * osel_s[:, c0:c1]
             + gat[3 * h + 2:3 * h + 3, :] * owin[:, c0:c1])
        pieces.append(o[g * HEAD_DIM:(g + 1) * HEAD_DIM, :])
    o_ref[...] = jnp.concatenate(pieces, axis=0).T


def _nsa_prompt(qa, kvn, kvw, ga, bias_c, bias_t, wbig, pe, batch, seq):
    nq = seq // TQ
    n_seg = seq // CMP_STRIDE
    nc = max(LANES, n_seg)
    cols = NSA_HEADS * TQ
    mselt = _cmp_to_sel_matrix(nc, LANES).T
    seg_w = CMP_STRIDE * NSA_GROUPS * HEAD_DIM
    vm = (2 * (seq * 768 * 4 + bias_t.size * 4 + wbig.size * 2 + cols * nc * 4) + 2 * nc * seg_w * 4
          + seq * LANES * 4 + 8 * cols * LANES * 4 + (12 << 20))
    return pl.pallas_call(
        functools.partial(_nsa_prompt_body, seq=seq),
        grid=(batch, nq),
        in_specs=[pl.BlockSpec((TQ, qa.shape[1]), lambda b, q: (b * nq + q, 0)),
                  pl.BlockSpec((1, seq, kvn.shape[2]), lambda b, q: (b, 0, 0)),
                  pl.BlockSpec((1, seq, kvw.shape[2]), lambda b, q: (b, 0, 0)),
                  pl.BlockSpec((TQ, LANES), lambda b, q: (b * nq + q, 0)),
                  pl.BlockSpec((1, nc, cols), lambda b, q: (q, 0, 0)),
                  pl.BlockSpec(bias_t.shape, lambda b, q: (0, 0, 0)),
                  pl.BlockSpec(wbig.shape, lambda b, q: (0, 0, 0, 0)),
                  pl.BlockSpec(pe.shape, lambda b, q: (0, 0, 0, 0)),
                  pl.BlockSpec(mselt.shape, lambda b, q: (0, 0))],
        out_specs=pl.BlockSpec((TQ, qa.shape[1]), lambda b, q: (b * nq + q, 0)),
        out_shape=jax.ShapeDtypeStruct(qa.shape, f32),
        scratch_shapes=[pltpu.VMEM((seq, LANES), f32), pltpu.VMEM((2, nc, seg_w), f32),
                        pltpu.VMEM((nc, LANES), f32), pltpu.VMEM((nc, LANES), f32),
                        pltpu.VMEM((LANES, cols), _MXU), pltpu.VMEM((LANES, cols), f32),
                        pltpu.VMEM((LANES, cols), f32), pltpu.VMEM((LANES, cols), f32),
                        pltpu.VMEM((1, cols), f32), pltpu.VMEM((1, cols), f32), pltpu.VMEM((LANES, cols), f32)],
        compiler_params=_vmem(vm),
        name="nsa_prompt",
    )(qa, kvn, kvw, ga, bias_c, bias_t, wbig, pe, mselt)


def _moba_prompt_body(qb_ref, kvb_ref, bt_ref, o_ref, kmean_s, q8_s, sel8_s, m_s, l_s, acc_s, *, seq):
    qt = pl.program_id(1)
    t0 = qt * TQ
    nb = seq // MOBA_BLK
    nr = _round_up(nb, SUBLANES)
    cols = MOBA_HEADS * TQ
    kw = MOBA_HEADS * HEAD_DIM
    scale = HEAD_DIM ** -0.5
    pairs = MOBA_HEADS // 2

    @pl.when(qt == 0)
    def _():
        kmean_s[...] = jnp.zeros(kmean_s.shape, f32)
        for n in range(nb):
            blk = kvb_ref[0, n * MOBA_BLK:(n + 1) * MOBA_BLK, 0:kw]
            kmean_s[n:n + 1, :] = jnp.sum(blk, axis=0, keepdims=True) * (1.0 / MOBA_BLK)
        sel8_s[...] = jnp.zeros(sel8_s.shape, f32)

    qtr = qb_ref[...].T
    zero = jnp.zeros((HEAD_DIM, TQ), f32)
    nblk = _iota((nr, TQ), 0)
    own = (t0 + _iota((nr, TQ), 1)) >> 8
    past = nblk < own
    for h in range(MOBA_HEADS):
        qh = qtr[h * HEAD_DIM:(h + 1) * HEAD_DIM, :]
        blk = jnp.concatenate([qh, zero] if h % 2 == 0 else [zero, qh], axis=0)
        q8_s[:, h * TQ:(h + 1) * TQ] = (blk * scale).astype(q8_s.dtype)
        gate = _dot_hi(kmean_s[:, (h // 2) * LANES:(h // 2 + 1) * LANES], blk)[0:nr, :]
        score = jnp.where(past, gate, -jnp.inf)
        sel = (past & (_rank_rows(score, nb) < MOBA_TOPK)) | (nblk == own)
        sel8_s[0:nr, h * TQ:(h + 1) * TQ] = sel.astype(f32)

    tpos = t0 + (_iota((1, cols), 1) & (TQ - 1))
    krow = _iota((TQ, cols), 0)

    def step(kt, carry):
        s_parts = []
        for pr in range(pairs):
            k = kvb_ref[0, pl.ds(kt * TQ, TQ), pr * LANES:(pr + 1) * LANES]
            s_parts.append(_dot(k, q8_s[:, 2 * pr * TQ:2 * (pr + 1) * TQ]))
        s = jnp.concatenate(s_parts, axis=1) + bt_ref[qt - kt]
        chosen = sel8_s[pl.ds((kt * TQ) >> 8, 1), :] > 0.5
        mask = chosen & (tpos >= kt * TQ + krow)
        s = jnp.where(mask, s, NEG)
        m_old = m_s[...]
        m_new = jnp.maximum(m_old, jnp.max(s, axis=0, keepdims=True))
        alpha = jnp.exp(m_old - m_new)
        p = jnp.where(mask, jnp.exp(s - m_new), 0.0)
        l_s[...] = alpha * l_s[...] + jnp.sum(p, axis=0, keepdims=True)
        pv = []
        for pr in range(pairs):
            v = kvb_ref[0, pl.ds(kt * TQ, TQ), kw + pr * LANES:kw + (pr + 1) * LANES]
            pv.append(_dot_tn(v, p[:, 2 * pr * TQ:2 * (pr + 1) * TQ]))
        acc_s[...] = alpha * acc_s[...] + jnp.concatenate(pv, axis=1)
        m_s[...] = m_new
        return carry

    _softmax_reset(m_s, l_s, acc_s)
    lax.fori_loop(0, qt + 1, step, 0)
    o = acc_s[...] / l_s[...]
    pieces = []
    for h in range(MOBA_HEADS):
        half = h % 2
        pieces.append(o[half * HEAD_DIM:(half + 1) * HEAD_DIM, h * TQ:(h + 1) * TQ])
    o_ref[...] = jnp.concatenate(pieces, axis=0).T


def _moba_prompt(qb, kvb, bias_t, batch, seq):
    nq = seq // TQ
    cols = MOBA_HEADS * TQ
    vm = 2 * (seq * kvb.shape[2] * 4 + bias_t.size * 4) + 8 * cols * LANES * 4 + (12 << 20)
    return pl.pallas_call(
        functools.partial(_moba_prompt_body, seq=seq),
        grid=(batch, nq),
        in_specs=[pl.BlockSpec((TQ, qb.shape[1]), lambda b, q: (b * nq + q, 0)),
                  pl.BlockSpec((1, seq, kvb.shape[2]), lambda b, q: (b, 0, 0)),
                  pl.BlockSpec(bias_t.shape, lambda b, q: (0, 0, 0))],
        out_specs=pl.BlockSpec((TQ, qb.shape[1]), lambda b, q: (b * nq + q, 0)),
        out_shape=jax.ShapeDtypeStruct(qb.shape, f32),
        scratch_shapes=[pltpu.VMEM((LANES, MOBA_HEADS * HEAD_DIM), f32),
                        pltpu.VMEM((LANES, cols), _MXU), pltpu.VMEM((LANES, cols), f32),
                        pltpu.VMEM((1, cols), f32), pltpu.VMEM((1, cols), f32), pltpu.VMEM((LANES, cols), f32)],
        compiler_params=_vmem(vm),
        name="moba_prompt",
    )(qb, kvb, bias_t)


def _page_group(n_pages):
    for g in (8, 4, 2):
        if n_pages % g == 0:
            return g
    raise ValueError("the page count must be even")


def _stack_heads_nsa(q, q8_s):
    half = _iota((1, LANES), 1) >> 6
    for h in range(NSA_HEADS):
        g = h // NSA_HPG
        blk = q[:, (h // 2) * LANES:(h // 2 + 1) * LANES]
        if h % 2 != g:
            blk = pltpu.roll(blk, HEAD_DIM, 1)
        q8_s[h:h + 1, :] = jnp.where(half == g, blk, 0.0).astype(q8_s.dtype)


def _nsa_sample_body(pt_ref, *refs, n_pages, group):
    del pt_ref
    ka_refs, vb_refs = refs[:group], refs[group:2 * group]
    (qa_ref, kvn_ref, kvw_ref, st_ref, ga_ref, bc_ref, bs_ref, bn_ref, bw_ref, wbig_ref, pe_ref, msel_ref, o_ref,
     xflat_s, kc_s, vc_s, q8_s, sel8_s, ocmp_s, m_s, l_s, acc_s) = refs[2 * group:]
    ph = pl.program_id(1)
    p = pl.program_id(2)
    n_steps = n_pages // group
    t = n_pages * PAGE
    seg_per_page = PAGE // CMP_STRIDE
    n_seg = n_pages * seg_per_page
    n_cmp = n_seg - CMP_BLK // CMP_STRIDE + 1
    cur = t // SEL_BLK
    n_sel = cur + 1
    nc = kc_s.shape[0]
    ls = sel8_s.shape[1]
    keys = group * PAGE
    scale = HEAD_DIM ** -0.5

    def one_key(q8, row):
        return _dot_nt(q8, jnp.broadcast_to(row, (SUBLANES, LANES)))[:, 0:1]

    @pl.when(ph == 0)
    def _():
        if nc > n_seg:
            @pl.when(p == 0)
            def _():
                xflat_s[...] = jnp.zeros(xflat_s.shape, f32)
        for c, page_refs in ((0, ka_refs), (1, vb_refs)):
            for g, ref in enumerate(page_refs):
                row0 = pl.multiple_of((p * group + g) * seg_per_page, SUBLANES)
                for r in range(CMP_STRIDE):
                    xflat_s[c, pl.ds(row0, seg_per_page), r * LANES:(r + 1) * LANES] = (
                        ref[0, pl.ds(r, seg_per_page, stride=CMP_STRIDE), :])

        @pl.when(p == n_steps - 1)
        def _():
            kc_s[...] = _compress(xflat_s, 0, pe_ref, wbig_ref)
            vc_s[...] = _compress(xflat_s, 1, pe_ref, wbig_ref)
            _stack_heads_nsa(qa_ref[0] * scale, q8_s)
            q8 = q8_s[...]
            ncol = _iota((NSA_HEADS, nc), 1)
            mask_c = ncol < n_cmp
            sc = jnp.where(mask_c, _dot_nt(q8, kc_s[...]) + bc_ref[...], NEG)
            e = jnp.exp(sc - jnp.max(sc, axis=1, keepdims=True))
            p_c = jnp.where(mask_c, e / jnp.sum(e, axis=1, keepdims=True), 0.0)
            ocmp_s[...] = _dot(p_c, vc_s[...])
            same_group = ((_iota((NSA_HEADS, NSA_HEADS), 0) >> 2) == (_iota((NSA_HEADS, NSA_HEADS), 1) >> 2)).astype(f32)
            blk = _dot_hi(_dot_hi(same_group, p_c), msel_ref[...])
            jb = _iota((NSA_HEADS, ls), 1)
            valid = jb < n_sel
            forced = valid & ((jb == 0) | (jb == cur) | (jb == cur - 1))
            score = jnp.where(forced, jnp.inf, jnp.where(valid, blk, -jnp.inf))
            sel8_s[...] = (valid & (_rank_lanes(score, n_sel) < N_SEL)).astype(sel8_s.dtype)

    @pl.when(ph == 1)
    def _():
        q8 = q8_s[...]

        @pl.when(p == 0)
        def _():
            m_s[...] = one_key(q8, kvn_ref[0, :, 2 * LANES:3 * LANES]) + bn_ref[:, 0:1]
            l_s[...] = jnp.ones(l_s.shape, f32)
            v_new = kvn_ref[0, :, 3 * LANES:4 * LANES].astype(_MXU).astype(f32)
            acc_s[...] = jnp.broadcast_to(v_new, acc_s.shape)

        k = jnp.concatenate([r[0] for r in ka_refs], axis=0)
        v = jnp.concatenate([r[0] for r in vb_refs], axis=0)
        s = _dot_nt(q8, k) + bs_ref[0]
        jrow = _iota((ls, keys), 0)
        jcol = (p * keys + _iota((ls, keys), 1)) >> 6
        chosen = _dot(sel8_s[...], (jrow == jcol).astype(f32)) > 0.5
        _softmax_step(s, chosen, v, m_s, l_s, acc_s)

        @pl.when(p == n_steps - 1)
        def _():
            osel = acc_s[...] / l_s[...]
            sw = _dot_nt(q8, st_ref[0, :, 0:LANES]) + bw_ref[...]
            sn = one_key(q8, kvw_ref[0, :, 0:LANES]) + bn_ref[:, 0:1]
            mw = jnp.maximum(jnp.max(sw, axis=1, keepdims=True), sn)
            ew = jnp.exp(sw - mw)
            en = jnp.exp(sn - mw)
            v_new = kvw_ref[0, :, LANES:2 * LANES].astype(_MXU).astype(f32)
            owin = (_dot(ew, st_ref[0, :, LANES:2 * LANES]) + en * v_new) / (jnp.sum(ew, axis=1, keepdims=True) + en)
            ga = ga_ref[0]
            for h in range(NSA_HEADS):
                g = h // NSA_HPG
                o = (ga[:, 3 * h:3 * h + 1] * ocmp_s[h:h + 1, :] + ga[:, 3 * h + 1:3 * h + 2] * osel[h:h + 1, :]
                     + ga[:, 3 * h + 2:3 * h + 3] * owin[h:h + 1, :])
                o_ref[0, :, h * HEAD_DIM:(h + 1) * HEAD_DIM] = o[:, g * HEAD_DIM:(g + 1) * HEAD_DIM]


def _nsa_sample(page_table, cache, qa, kvn, kvw, state, ga, tab, wbig, pe):
    bs, n_pages = page_table.shape
    group = _page_group(n_pages)
    n_steps = n_pages // group
    t = n_pages * PAGE
    win = state.shape[1]
    n_cmp = n_pages * (PAGE // CMP_STRIDE) - CMP_BLK // CMP_STRIDE + 1
    nc = max(LANES, n_pages * (PAGE // CMP_STRIDE))
    ls = _round_up(t // SEL_BLK + 1, LANES)
    msel = _cmp_to_sel_matrix(nc, ls)
    d_last = t - (CMP_STRIDE * (n_cmp - 1) + CMP_BLK - 1)
    bias_c = tab[:, d_last:t - CMP_BLK + 2:CMP_STRIDE][:, ::-1]
    bias_c = jnp.pad(bias_c, ((0, 0), (0, nc - n_cmp)))
    bias_s = jnp.transpose(tab[:, 1:t + 1][:, ::-1].reshape(NSA_HEADS, n_steps, group * PAGE), (1, 0, 2))
    bias_n = jnp.broadcast_to(tab[:, 0:1], (NSA_HEADS, LANES))
    bias_w = tab[:, 1:win + 1][:, ::-1]
    seg_w = CMP_STRIDE * NSA_GROUPS * HEAD_DIM
    req = lambda b, ph, p, pt: (b, 0, 0)
    c2 = lambda b, ph, p, pt: (0, 0)
    c4 = lambda b, ph, p, pt: (0, 0, 0, 0)

    def page_spec(g, lane_block):
        return pl.BlockSpec((1, PAGE, LANES), lambda b, ph, p, pt: (pt[b, p * group + g], 0, 2 * ph + lane_block))

    grid_spec = pltpu.PrefetchScalarGridSpec(
        num_scalar_prefetch=1,
        grid=(bs, 2, n_steps),
        in_specs=([page_spec(g, 0) for g in range(group)] + [page_spec(g, 1) for g in range(group)]
                  + [pl.BlockSpec((1, 1, qa.shape[2]), req), pl.BlockSpec((1, 1, kvn.shape[2]), req),
                     pl.BlockSpec((1, 1, kvw.shape[2]), req), pl.BlockSpec((1, win, state.shape[2]), req),
                     pl.BlockSpec((1, 1, LANES), req),
                     pl.BlockSpec(bias_c.shape, c2),
                     pl.BlockSpec((1, NSA_HEADS, group * PAGE), lambda b, ph, p, pt: (p * ph, 0, 0)),
                     pl.BlockSpec(bias_n.shape, c2), pl.BlockSpec(bias_w.shape, c2),
                     pl.BlockSpec(wbig.shape, c4), pl.BlockSpec(pe.shape, c4), pl.BlockSpec(msel.shape, c2)]),
        out_specs=pl.BlockSpec((1, 1, qa.shape[2]), req),
        scratch_shapes=[pltpu.VMEM((2, nc, seg_w), f32), pltpu.VMEM((nc, LANES), f32), pltpu.VMEM((nc, LANES), f32),
                        pltpu.VMEM((NSA_HEADS, LANES), _MXU), pltpu.VMEM((NSA_HEADS, ls), _MXU),
                        pltpu.VMEM((NSA_HEADS, LANES), f32),
                        pltpu.VMEM((NSA_HEADS, 1), f32), pltpu.VMEM((NSA_HEADS, 1), f32),
                        pltpu.VMEM((NSA_HEADS, LANES), f32)])
    vm = (2 * nc * seg_w * 4 + 2 * (wbig.size * 2 + msel.size * 4 + win * 256 * 4 + 2 * group * PAGE * LANES * 4)
          + (16 << 20))
    return pl.pallas_call(
        functools.partial(_nsa_sample_body, n_pages=n_pages, group=group),
        grid_spec=grid_spec,
        out_shape=jax.ShapeDtypeStruct(qa.shape, f32),
        compiler_params=_vmem(vm),
        name="nsa_sample",
    )(page_table, *([cache] * (2 * group)), qa, kvn, kvw, state, ga, bias_c, bias_s, bias_n, bias_w, wbig, pe, msel)


def _moba_sample_body(pt_ref, *refs, n_pages, group):
    del pt_ref
    k_refs, v_refs = refs[:group], refs[group:2 * group]
    qb_ref, kvb_ref, bs_ref, bn_ref, o_ref, kmean_s, q8_s, sel8_s, m_s, l_s, acc_s = refs[2 * group:]
    ph = pl.program_id(1)
    p = pl.program_id(2)
    n_steps = n_pages // group
    t = n_pages * PAGE
    kw = MOBA_HEADS * HEAD_DIM
    pages_per_blk = MOBA_BLK // PAGE
    blks_per_step = group // pages_per_blk
    own = t // MOBA_BLK
    keys = group * PAGE
    scale = HEAD_DIM ** -0.5
    head_lane = (_iota((MOBA_HEADS, kw), 1) >> 6) == _iota((MOBA_HEADS, kw), 0)

    @pl.when(ph == 0)
    def _():
        @pl.when(p == 0)
        def _():
            kmean_s[...] = jnp.zeros(kmean_s.shape, f32)

        for j in range(blks_per_step):
            tot = jnp.sum(k_refs[j * pages_per_blk][0], axis=0, keepdims=True)
            for i in range(1, pages_per_blk):
                tot = tot + jnp.sum(k_refs[j * pages_per_blk + i][0], axis=0, keepdims=True)
            kmean_s[pl.ds(p * blks_per_step + j, 1), :] = tot * (1.0 / MOBA_BLK)

        @pl.when(p == n_steps - 1)
        def _():
            q8 = jnp.where(head_lane, jnp.broadcast_to(qb_ref[0], (MOBA_HEADS, kw)), 0.0)
            q8_s[...] = q8
            gate = _dot_nt_hi(q8, kmean_s[...])
            nblk = _iota((MOBA_HEADS, LANES), 1)
            past = nblk < own
            score = jnp.where(past, gate, -jnp.inf)
            sel8_s[...] = (past & (_rank_lanes(score, own) < MOBA_TOPK)).astype(sel8_s.dtype)

    @pl.when(ph == 1)
    def _():
        q8 = q8_s[...] * scale

        @pl.when(p == 0)
        def _():
            k_new = jnp.broadcast_to(kvb_ref[0, :, 0:kw], (SUBLANES, kw))
            m_s[...] = _dot_nt(q8, k_new)[:, 0:1] + bn_ref[:, 0:1]
            l_s[...] = jnp.ones(l_s.shape, f32)
            acc_s[...] = jnp.broadcast_to(kvb_ref[0, :, kw:2 * kw].astype(_MXU).astype(f32), acc_s.shape)

        k = jnp.concatenate([r[0] for r in k_refs], axis=0)
        v = jnp.concatenate([r[0] for r in v_refs], axis=0)
        s = _dot_nt(q8, k) + bs_ref[0]
        jcol = (p * keys + _iota((LANES, keys), 1)) >> 8
        chosen = _dot(sel8_s[...], (_iota((LANES, keys), 0) == jcol).astype(f32)) > 0.5
        _softmax_step(s, chosen, v, m_s, l_s, acc_s)

        @pl.when(p == n_steps - 1)
        def _():
            o8 = jnp.where(head_lane, acc_s[...] / l_s[...], 0.0)
            o_ref[0] = jnp.sum(o8, axis=0, keepdims=True)


def _moba_sample(page_table, cache, qb, kvb, tab):
    bs, n_pages = page_table.shape
    group = _page_group(n_pages)
    n_steps = n_pages // group
    t = n_pages * PAGE
    kw = MOBA_HEADS * HEAD_DIM
    bias_s = jnp.transpose(tab[:, 1:t + 1][:, ::-1].reshape(MOBA_HEADS, n_steps, group * PAGE), (1, 0, 2))
    bias_n = jnp.broadcast_to(tab[:, 0:1], (MOBA_HEADS, LANES))
    req = lambda b, ph, p, pt: (b, 0, 0)

    def k_spec(g):
        return pl.BlockSpec((1, PAGE, kw), lambda b, ph, p, pt: (pt[b, p * group + g], 0, 0))

    def v_spec(g):
        return pl.BlockSpec((1, PAGE, kw), lambda b, ph, p, pt: (pt[b, (p * group + g) * ph], 0, 1))

    grid_spec = pltpu.PrefetchScalarGridSpec(
        num_scalar_prefetch=1,
        grid=(bs, 2, n_steps),
        in_specs=([k_spec(g) for g in range(group)] + [v_spec(g) for g in range(group)]
                  + [pl.BlockSpec((1, 1, kw), req), pl.BlockSpec((1, 1, 2 * kw), req),
                     pl.BlockSpec((1, MOBA_HEADS, group * PAGE), lambda b, ph, p, pt: (p * ph, 0, 0)),
                     pl.BlockSpec(bias_n.shape, lambda b, ph, p, pt: (0, 0))]),
        out_specs=pl.BlockSpec((1, 1, kw), req),
        scratch_shapes=[pltpu.VMEM((LANES, kw), f32), pltpu.VMEM((MOBA_HEADS, kw), f32),
                        pltpu.VMEM((MOBA_HEADS, LANES), _MXU),
                        pltpu.VMEM((MOBA_HEADS, 1), f32), pltpu.VMEM((MOBA_HEADS, 1), f32),
                        pltpu.VMEM((MOBA_HEADS, kw), f32)])
    return pl.pallas_call(
        functools.partial(_moba_sample_body, n_pages=n_pages, group=group),
        grid_spec=grid_spec,
        out_shape=jax.ShapeDtypeStruct(qb.shape, f32),
        compiler_params=_vmem(2 * 2 * group * PAGE * kw * 4 + (24 << 20)),
        name="moba_sample",
    )(page_table, *([cache] * (2 * group)), qb, kvb, bias_s, bias_n)


def _layer_norm(x, g, b):
    mu = jnp.mean(x, axis=-1, keepdims=True)
    xc = x - mu
    var = jnp.mean(xc * xc, axis=-1, keepdims=True)
    return xc * lax.rsqrt(var + LN_EPS) * g + b


def _merge_body(oa_ref, ob_ref, gm_ref, x_ref, wa_ref, wb_ref, wo_ref, g_ref, b_ref, h_ref, ht_ref, *, alpha):
    d = x_ref.shape[1]
    ya = _dot(oa_ref[...], wa_ref[...])
    yb = _dot(ob_ref[...], wb_ref[...])
    gm = gm_ref[...]
    mix = _dot(gm[:, :d] * ya + gm[:, d:] * yb, wo_ref[...])
    h = _layer_norm(alpha * x_ref[...] + mix, g_ref[...], b_ref[...])
    h_ref[...] = h
    ht_ref[...] = h.T.astype(ht_ref.dtype)


def _merge(oa, ob, gm, x2d, wa, wb, wo, g, b, alpha):
    n, d = x2d.shape
    tm = 256 if n % 256 == 0 else n
    row = lambda i: (i, 0)
    full = lambda i: (0, 0)
    vm = 2 * (tm * (oa.shape[1] + ob.shape[1] + gm.shape[1] + 2 * d) * 4 + tm * d * 2
              + (wa.size + wb.size + wo.size) * 2) + (16 << 20)
    return pl.pallas_call(
        functools.partial(_merge_body, alpha=alpha),
        grid=(n // tm,),
        in_specs=[pl.BlockSpec((tm, oa.shape[1]), row), pl.BlockSpec((tm, ob.shape[1]), row),
                  pl.BlockSpec((tm, gm.shape[1]), row), pl.BlockSpec((tm, d), row),
                  pl.BlockSpec(wa.shape, full), pl.BlockSpec(wb.shape, full), pl.BlockSpec(wo.shape, full),
                  pl.BlockSpec((1, d), full), pl.BlockSpec((1, d), full)],
        out_specs=[pl.BlockSpec((tm, d), row), pl.BlockSpec((d, tm), lambda i: (0, i))],
        out_shape=[jax.ShapeDtypeStruct((n, d), f32), jax.ShapeDtypeStruct((d, n), _MXU)],
        compiler_params=_vmem(vm),
        name="merge",
    )(oa, ob, gm, x2d, wa, wb, wo, g, b)


_NOT_TOP = 99.0
_CAND_B = tuple(PEER_TOPK // (a + 1) for a in range(SUBLANES))


def _extract_top(work_s, n_rounds, on_round):
    rows = _iota(work_s.shape, 0)

    def body(r, carry):
        w = work_s[...]
        m = jnp.max(w, axis=0, keepdims=True)
        idx = jnp.min(jnp.where(w == m, rows, work_s.shape[0]), axis=0, keepdims=True)
        hit = rows == idx
        on_round(r, m, hit)
        work_s[...] = jnp.where(hit, -jnp.inf, w)
        return carry

    lax.fori_loop(0, n_rounds, body, 0)


def _peer_select(ht_ref, wqt_ref, keys_ref, qt_s, rank_s, exp_s, sv_s, work_s, cwork_s, cand_s, cnt_s, f0_s,
                 rk1_s, e1_s):
    tn = ht_ref.shape[1]
    nk = PEER_N_KEYS
    qt_s[...] = jnp.dot(wqt_ref[...], ht_ref[...], preferred_element_type=f32)

    def per_half(hc, carry):
        s = _dot(keys_ref[hc], qt_s[pl.ds(pl.multiple_of(hc * PEER_HALF, PEER_HALF), PEER_HALF), :])
        work_s[...] = s
        rank_s[hc] = jnp.full((nk, tn), _NOT_TOP, f32)

        def on_round(r, m, hit):
            sv_s[hc, pl.ds(r, 1), :] = m
            rank_s[hc] = jnp.where(hit, jnp.asarray(r, f32), rank_s[hc])

        _extract_top(work_s, PEER_TOPK, on_round)
        exp_s[hc] = jnp.exp(s - sv_s[hc, 0:1, :])
        return carry

    lax.fori_loop(0, 2 * PEER_HEADS, per_half, 0)

    n_cand = cand_s.shape[0]

    def per_head(h, carry):
        sv0 = sv_s[2 * h]
        sv1 = sv_s[2 * h + 1]
        e0 = jnp.exp(sv0 - sv0[0:1, :])
        e1 = jnp.exp(sv1 - sv1[0:1, :])
        brow = _iota((SUBLANES, tn), 0)
        tiles, etiles = [sv0[0:1, :] + sv1, ], [e0[0:1, :] * e1]
        for a in range(1, SUBLANES):
            ok = brow < _CAND_B[a]
            tiles.append(jnp.where(ok, sv0[a:a + 1, :] + sv1[0:SUBLANES, :], -jnp.inf))
            etiles.append(jnp.where(ok, e0[a:a + 1, :] * e1[0:SUBLANES, :], 0.0))
        tiles.append(sv0[SUBLANES:, :] + sv1[0:1, :])
        etiles.append(e0[SUBLANES:, :] * e1[0:1, :])
        cwork_s[...] = jnp.concatenate(tiles, axis=0)
        cand_s[...] = jnp.zeros(cand_s.shape, f32)

        def on_round(r, m, hit):
            cand_s[...] = jnp.where(hit, 1.0, cand_s[...])

        _extract_top(cwork_s, PEER_TOPK, on_round)
        chosen = cand_s[...]
        z = jnp.sum(chosen * jnp.concatenate(etiles, axis=0), axis=0, keepdims=True)
        cnts = [jnp.sum(chosen[0:PEER_TOPK], axis=0, keepdims=True)]
        for a in range(1, SUBLANES):
            r0 = PEER_TOPK + (a - 1) * SUBLANES
            cnts.append(jnp.sum(chosen[r0:r0 + SUBLANES], axis=0, keepdims=True))
        tail = chosen[n_cand - SUBLANES:]
        cnts += [tail[a:a + 1, :] for a in range(SUBLANES)]
        ra = rank_s[2 * h]
        c0 = jnp.zeros((nk, tn), f32)
        for a in range(PEER_TOPK):
            c0 = jnp.where(ra == float(a), cnts[a], c0)
        cnt_s[h] = c0
        f0_s[h] = exp_s[2 * h] / z
        rk1_s[h] = rank_s[2 * h + 1].astype(rk1_s.dtype)
        e1_s[h] = exp_s[2 * h + 1].astype(e1_s.dtype)
        return carry

    lax.fori_loop(0, PEER_HEADS, per_head, 0)


def _gelu_t(u_tile, ht):
    at = jnp.dot(u_tile, ht, preferred_element_type=f32)
    return (0.5 * at * (1.0 + lax.erf(at * math.sqrt(0.5)))).astype(_MXU)


def _peer_body(ht_ref, h_ref, wqt_ref, keys_ref, u0_ref, un_ref, vt_ref, g_ref, b_ref, y_ref,
               qt_s, rank_s, exp_s, sv_s, work_s, cwork_s, cand_s, cnt_s, f0_s, rk1_s, e1_s, gel_s, wc_s, acc_s,
               *, alpha, ti):
    e = pl.program_id(1)
    last = pl.num_programs(1) - 1
    tn = ht_ref.shape[1]

    @pl.when(e == 0)
    def _():
        _peer_select(ht_ref, wqt_ref, keys_ref, qt_s, rank_s, exp_s, sv_s, work_s, cwork_s, cand_s, cnt_s, f0_s,
                     rk1_s, e1_s)
        acc_s[...] = jnp.zeros(acc_s.shape, f32)
        gel_s[0] = _gelu_t(u0_ref[...], ht_ref[...])

    lane_chunk = min(tn, LANES)
    for ii in range(ti):
        i = e * ti + ii
        r0, r1 = ii * PEER_N_KEYS, (ii + 1) * PEER_N_KEYS
        cnt = [cnt_s[h, pl.ds(i, 1), :].astype(_MXU) for h in range(PEER_HEADS)]
        f0 = [f0_s[h, pl.ds(i, 1), :].astype(_MXU) for h in range(PEER_HEADS)]
        for c0 in range(0, tn, lane_chunk):
            cs = slice(c0, c0 + lane_chunk)
            w = jnp.zeros((PEER_N_KEYS, lane_chunk), _MXU)
            for h in range(PEER_HEADS):
                w = w + jnp.where(rk1_s[h, :, cs] < cnt[h][:, cs], e1_s[h, :, cs], jnp.zeros((), _MXU)) * f0[h][:, cs]
            wc_s[r0:r1, cs] = w * gel_s[e % 2, r0:r1, cs]
        if ii % 2 == 1:
            k0, k1 = (ii - 1) * PEER_N_KEYS, (ii + 1) * PEER_N_KEYS
            acc_s[...] += jnp.dot(vt_ref[:, k0:k1], wc_s[k0:k1, :], preferred_element_type=f32)
    gel_s[(e + 1) % 2] = _gelu_t(un_ref[...], ht_ref[...])

    @pl.when(e == last)
    def _():
        y_ref[...] = _layer_norm(alpha * h_ref[...] + acc_s[...].T, g_ref[...], b_ref[...])


def _peer(ht, h, wqt, keys, u, vt, g, b, alpha):
    d, n = ht.shape
    n_exp = u.shape[0]
    tn = 512 if n % 512 == 0 else n
    ti = 4
    te = ti * PEER_N_KEYS
    n_half = 2 * PEER_HEADS
    n_cand = PEER_TOPK + (SUBLANES - 1) * SUBLANES + SUBLANES
    tok = lambda t, e: (0, t)
    full2 = lambda t, e: (0, 0)
    scratch = [pltpu.VMEM((wqt.shape[0], tn), f32),
               pltpu.VMEM((n_half, PEER_N_KEYS, tn), f32), pltpu.VMEM((n_half, PEER_N_KEYS, tn), f32),
               pltpu.VMEM((n_half, PEER_TOPK, tn), f32), pltpu.VMEM((PEER_N_KEYS, tn), f32),
               pltpu.VMEM((n_cand, tn), f32), pltpu.VMEM((n_cand, tn), f32),
               pltpu.VMEM((PEER_HEADS, PEER_N_KEYS, tn), f32), pltpu.VMEM((PEER_HEADS, PEER_N_KEYS, tn), f32),
               pltpu.VMEM((PEER_HEADS, PEER_N_KEYS, tn), _MXU), pltpu.VMEM((PEER_HEADS, PEER_N_KEYS, tn), _MXU),
               pltpu.VMEM((2, te, tn), _MXU), pltpu.VMEM((te, tn), _MXU), pltpu.VMEM((d, tn), f32)]
    vm = (wqt.shape[0] * tn * 4 + 2 * n_half * PEER_N_KEYS * tn * 4 + 2 * PEER_HEADS * PEER_N_KEYS * tn * 6
          + 3 * te * tn * 2 + d * tn * 4
          + 2 * (d * tn * 2 + 2 * tn * d * 4 + wqt.size * 2 + keys.size * 2 + 3 * te * d * 2) + (14 << 20))
    n_tiles = n_exp // te
    return pl.pallas_call(
        functools.partial(_peer_body, alpha=alpha, ti=ti),
        grid=(n // tn, n_tiles),
        in_specs=[pl.BlockSpec((d, tn), tok),
                  pl.BlockSpec((tn, d), lambda t, e: (t, 0)),
                  pl.BlockSpec(wqt.shape, full2),
                  pl.BlockSpec(keys.shape, lambda t, e: (0, 0, 0)),
                  pl.BlockSpec((te, d), full2),
                  pl.BlockSpec((te, d), lambda t, e: (jnp.minimum(e + 1, n_tiles - 1), 0)),
                  pl.BlockSpec((d, te), lambda t, e: (0, e)),
                  pl.BlockSpec((1, d), full2), pl.BlockSpec((1, d), full2)],
        out_specs=pl.BlockSpec((tn, d), lambda t, e: (t, 0)),
        out_shape=jax.ShapeDtypeStruct((n, d), f32),
        scratch_shapes=scratch,
        compiler_params=_vmem(vm),
        name="peer",
    )(ht, h, wqt, keys, u, u, vt, g, b)


def kernel(x_prompt, x_sample, cache_nsa, cache_moba, state_nsa_win, page_table, w_in, nsa_cmp_pe, nsa_cmp_w, w_branch_a, w_branch_b, w_out, ln1_g, ln1_b, peer_wq, peer_keys, peer_u, peer_v, ln2_g, ln2_b, rel_bias):
    depth = w_in.shape[0]
    batch, seq, d = x_prompt.shape
    bs, dec_seq, _ = x_sample.shape
    n_pages = page_table.shape[1]
    past = n_pages * PAGE
    win_len = state_nsa_win.shape[2]
    assert dec_seq == 1 and cache_nsa.shape[2] == PAGE and cache_moba.shape[2] == PAGE
    assert seq % MOBA_BLK == 0 and past % MOBA_BLK == 0 and win_len == WINDOW and seq % TQ == 0
    alpha = (2 * depth) ** 0.25
    nq = seq // TQ
    n_tok = batch * seq

    tab = _bias_by_distance(rel_bias, max(seq, past + 1) + TQ)
    tab_a, tab_b = tab[:NSA_HEADS], tab[NSA_HEADS:]
    tiles_a, tiles_b = _bias_tiles_t(tab_a, nq), _bias_tiles_t(tab_b, nq)
    cmp_bias = _nsa_cmp_bias_t(tab_a, seq, max(LANES, seq // CMP_STRIDE))

    xp = x_prompt.reshape(n_tok, d)
    xs = x_sample.reshape(bs, d)
    outs = [[] for _ in range(6)]
    for l in range(depth):
        w_packed = _pack_w_in(w_in[l], d)
        wbig, pe = _pack_cmp(nsa_cmp_pe[l], nsa_cmp_w[l])
        wa, wb, wo = w_branch_a[l].astype(_MXU), w_branch_b[l].astype(_MXU), w_out[l].astype(_MXU)
        wqt = peer_wq[l].T.astype(_MXU)
        keys = peer_keys[l].reshape(2 * PEER_HEADS, PEER_N_KEYS, PEER_HALF).astype(_MXU)
        u = peer_u[l].astype(_MXU)
        vt = peer_v[l].T.astype(_MXU)
        g1, b1, g2, b2 = ln1_g[l][None], ln1_b[l][None], ln2_g[l][None], ln2_b[l][None]

        def ffn(oa, ob, gm, x2d):
            h, ht = _merge(oa, ob, gm, x2d, wa, wb, wo, g1, b1, alpha)
            return _peer(ht, h, wqt, keys, u, vt, g2, b2, alpha)

        qa, kvn, kvw, qb, kvb, gm, ga = _project(xp, w_packed, d)
        oa = _nsa_prompt(qa, kvn.reshape(batch, seq, -1), kvw.reshape(batch, seq, -1), ga, cmp_bias, tiles_a,
                         wbig, pe, batch, seq)
        ob = _moba_prompt(qb, kvb.reshape(batch, seq, -1), tiles_b, batch, seq)
        xp = ffn(oa, ob, gm, xp)
        outs[0].append(kvn.reshape(batch, seq, 4, NSA_GROUPS, HEAD_DIM))
        outs[1].append(kvb.reshape(batch, seq, 2, MOBA_HEADS, HEAD_DIM))
        outs[2].append(kvw.reshape(batch, seq, 2, NSA_GROUPS, HEAD_DIM)[:, seq - min(WINDOW, seq):])

        qa, kvn, kvw, qb, kvb, gm, ga = _project(xs, w_packed, d)
        r3 = lambda a: a.reshape(bs, 1, -1)
        state = state_nsa_win[l].reshape(bs, win_len, -1)
        oa = _nsa_sample(page_table, cache_nsa[l].reshape(-1, PAGE, 4 * NSA_GROUPS * HEAD_DIM), r3(qa), r3(kvn),
                         r3(kvw), state, r3(ga), tab_a, wbig, pe)
        ob = _moba_sample(page_table, cache_moba[l].reshape(-1, PAGE, 2 * MOBA_HEADS * HEAD_DIM), r3(qb), r3(kvb),
                          tab_b)
        xs = ffn(oa.reshape(bs, -1), ob.reshape(bs, -1), gm, xs)
        outs[3].append(kvn.reshape(bs, 1, 4, NSA_GROUPS, HEAD_DIM))
        outs[4].append(kvb.reshape(bs, 1, 2, MOBA_HEADS, HEAD_DIM))
        win_new = jnp.concatenate([state_nsa_win[l][:, 1:], kvw.reshape(bs, 1, 2, NSA_GROUPS, HEAD_DIM)], axis=1)
        outs[5].append(win_new)

    stacked = [jnp.stack(o) for o in outs]
    return (xp.reshape(batch, seq, d), xs.reshape(bs, 1, d), *stacked)
```

```python
import functools
import math

import jax
import jax.numpy as jnp
from jax import lax
from jax.experimental import pallas as pl
from jax.experimental.pallas import tpu as pltpu

HEAD_DIM = 64
NSA_HEADS = 8
NSA_GROUPS = 2
NSA_HPG = NSA_HEADS // NSA_GROUPS
CMP_BLK = 32
CMP_STRIDE = 16
SEL_BLK = 64
N_SEL = 16
WINDOW = 512
MOBA_HEADS = 8
MOBA_BLK = 256
MOBA_TOPK = 3
N_BUCKETS = 32
MAX_DISTANCE = 1024
PEER_HEADS = 8
PEER_N_KEYS = 128
PEER_HALF = 128
PEER_TOPK = 16
LN_EPS = 1e-5
NEG = -1e30

LANES = 128
SUBLANES = 8
VMEM_BYTES_V7X = 64 * 1024 * 1024

TQ = 128
PAGE = 128

f32 = jnp.float32
i32 = jnp.int32
_MXU = jnp.bfloat16
_HI = lax.Precision.HIGHEST


def _iota(shape, dim):
    return lax.broadcasted_iota(i32, shape, dim)


def _dot(a, b):
    return jnp.dot(a.astype(_MXU), b.astype(_MXU), preferred_element_type=f32)


def _dot_nt(a, b):
    return lax.dot_general(a.astype(_MXU), b.astype(_MXU), (((1,), (1,)), ((), ())),
                           preferred_element_type=f32)


def _dot_tn(a, b):
    return lax.dot_general(a.astype(_MXU), b.astype(_MXU), (((0,), (0,)), ((), ())), preferred_element_type=f32)


def _dot_hi(a, b):
    return jnp.dot(a, b, precision=_HI, preferred_element_type=f32)


def _dot_nt_hi(a, b):
    return lax.dot_general(a, b, (((1,), (1,)), ((), ())), precision=_HI, preferred_element_type=f32)


def _vmem(nbytes, flags=None):
    return pltpu.CompilerParams(vmem_limit_bytes=int(min(nbytes, VMEM_BYTES_V7X - (4 << 20))), flags=flags)


def _round_up(x, m):
    return -(-x // m) * m


def _rank_lanes(s, n):
    width = s.shape[1]
    j = _iota(s.shape, 1)
    rank = jnp.zeros(s.shape, i32)
    for r in range(1, n):
        lo = pltpu.roll(s, r, 1)
        rank = rank + jnp.where((j >= r) & (lo >= s), 1, 0)
        hi = pltpu.roll(s, width - r, 1)
        rank = rank + jnp.where((j + r < n) & (hi > s), 1, 0)
    return rank


def _rank_rows(s, n):
    j = _iota(s.shape, 0)
    rank = jnp.zeros(s.shape, i32)
    for k in range(n):
        row = s[k:k + 1, :]
        rank = rank + jnp.where((row > s) | ((row == s) & (j > k)), 1, 0)
    return rank


def _softmax_reset(m_s, l_s, acc_s):
    m_s[...] = jnp.full(m_s.shape, NEG, f32)
    l_s[...] = jnp.zeros(l_s.shape, f32)
    acc_s[...] = jnp.zeros(acc_s.shape, f32)


def _softmax_step(s, mask, v, m_s, l_s, acc_s):
    s = jnp.where(mask, s, NEG)
    m_old = m_s[...]
    m_new = jnp.maximum(m_old, jnp.max(s, axis=1, keepdims=True))
    alpha = jnp.exp(m_old - m_new)
    p = jnp.where(mask, jnp.exp(s - m_new), 0.0)
    l_s[...] = alpha * l_s[...] + jnp.sum(p, axis=1, keepdims=True)
    acc_s[...] = alpha * acc_s[...] + _dot(p, v)
    m_s[...] = m_new


def _softmax_step_t(s, mask, v, m_s, l_s, acc_s):
    s = jnp.where(mask, s, NEG)
    m_old = m_s[...]
    m_new = jnp.maximum(m_old, jnp.max(s, axis=0, keepdims=True))
    alpha = jnp.exp(m_old - m_new)
    p = jnp.where(mask, jnp.exp(s - m_new), 0.0)
    l_s[...] = alpha * l_s[...] + jnp.sum(p, axis=0, keepdims=True)
    acc_s[...] = alpha * acc_s[...] + _dot_tn(v, p)
    m_s[...] = m_new


_W_QA = NSA_HEADS * HEAD_DIM
_W_KVN = 4 * NSA_GROUPS * HEAD_DIM
_W_KVW = 2 * NSA_GROUPS * HEAD_DIM
_W_GA = 3 * NSA_HEADS
_W_QB = MOBA_HEADS * HEAD_DIM
_W_KVB = 2 * MOBA_HEADS * HEAD_DIM


def _proj_body(x_ref, w_ref, qa_ref, kvn_ref, kvw_ref, qb_ref, kvb_ref, gm_ref, ga_ref):
    xb = x_ref[...].astype(_MXU)
    col = 0
    for ref, act in ((qa_ref, None), (kvn_ref, None), (kvw_ref, None), (qb_ref, None), (kvb_ref, None),
                     (gm_ref, jax.nn.sigmoid), (ga_ref, jax.nn.sigmoid)):
        width = ref.shape[1]
        y = jnp.dot(xb, w_ref[:, col:col + width], preferred_element_type=f32)
        ref[...] = y if act is None else act(y)
        col += width


def _project(x2d, w_packed, d_model):
    n = x2d.shape[0]
    tm = 256 if n % 256 == 0 else n
    widths = (_W_QA, _W_KVN, _W_KVW, _W_QB, _W_KVB, 2 * d_model, LANES)
    n_cols = sum(widths)
    return pl.pallas_call(
        _proj_body,
        grid=(n // tm,),
        in_specs=[pl.BlockSpec((tm, d_model), lambda i: (i, 0)),
                  pl.BlockSpec((d_model, n_cols), lambda i: (0, 0))],
        out_specs=[pl.BlockSpec((tm, w), lambda i: (i, 0)) for w in widths],
        out_shape=[jax.ShapeDtypeStruct((n, w), f32) for w in widths],
        compiler_params=_vmem(2 * (tm * d_model * 4 + d_model * n_cols * 2 + tm * n_cols * 4) + (8 << 20)),
        name="proj",
    )(x2d, w_packed)


def _pack_w_in(w_in, d_model):
    c_ga = _W_QA + _W_KVN + _W_KVW
    c_qb = c_ga + _W_GA
    pad = jnp.zeros((d_model, LANES - _W_GA), w_in.dtype)
    return jnp.concatenate([w_in[:, :c_ga], w_in[:, c_qb:], w_in[:, c_ga:c_qb], pad], axis=1).astype(_MXU)


def _t5_bucket(dist):
    n = jnp.maximum(dist, 0)
    max_exact = N_BUCKETS // 2
    nf = jnp.maximum(n, 1).astype(f32)
    large = max_exact + (jnp.log(nf / max_exact) / math.log(MAX_DISTANCE / max_exact)
                         * (N_BUCKETS - max_exact)).astype(i32)
    return jnp.where(n < max_exact, n, jnp.minimum(large, N_BUCKETS - 1)).astype(i32)


def _bias_by_distance(rel_bias, max_dist):
    return rel_bias[_t5_bucket(jnp.arange(max_dist, dtype=i32))].T


def _windows(v, starts, length):
    return jnp.stack([v[:, s:s + length] for s in starts], axis=1)


def _bias_tiles_t(tab, n_tiles):
    h = tab.shape[0]
    tabp = jnp.concatenate([jnp.zeros((h, TQ - 1), tab.dtype), tab], axis=1)
    g = _windows(tabp, range(TQ), n_tiles * TQ).reshape(h, TQ, n_tiles, TQ)[..., ::-1]
    return jnp.transpose(g, (2, 3, 0, 1)).reshape(n_tiles, TQ, h * TQ)


def _nsa_cmp_bias_t(tab, seq, nc):
    h = tab.shape[0]
    off = CMP_STRIDE * (nc - 1) + CMP_BLK - 1
    tabp = jnp.concatenate([jnp.zeros((h, off), tab.dtype), tab[:, :seq]], axis=1)
    w = _windows(tabp, [CMP_STRIDE * (nc - 1 - n) for n in range(nc)], seq)
    return jnp.transpose(w.reshape(h, nc, seq // TQ, TQ), (2, 1, 0, 3)).reshape(seq // TQ, nc, h * TQ)


def _pack_cmp(cmp_pe, cmp_w):
    r = CMP_BLK // CMP_STRIDE
    w = cmp_w.reshape(2, r, CMP_STRIDE, HEAD_DIM, HEAD_DIM)
    eye = jnp.eye(NSA_GROUPS, dtype=w.dtype)
    wbig = jnp.einsum("cmrde,gh->cmrgdhe", w, eye).reshape(2, r, CMP_STRIDE * NSA_GROUPS * HEAD_DIM,
                                                          NSA_GROUPS * HEAD_DIM)
    pe = cmp_pe.reshape(2, r, CMP_STRIDE, 1, HEAD_DIM)
    pe = jnp.broadcast_to(pe, (2, r, CMP_STRIDE, NSA_GROUPS, HEAD_DIM)).reshape(2, r, 1, -1)
    return wbig.astype(_MXU), pe.astype(f32)


def _cmp_to_sel_matrix(n_rows, n_cols):
    spb = SEL_BLK // CMP_STRIDE
    n = jnp.arange(n_rows, dtype=i32)[:, None]
    j = jnp.arange(n_cols, dtype=i32)[None, :]
    m = jnp.zeros((n_rows, n_cols), f32)
    for k in range(CMP_BLK // CMP_STRIDE):
        m = m + ((n + k) // spb == j).astype(f32)
    return m


def _compress(xflat_ref, c, pe_ref, wbig_ref):
    x = xflat_ref[c]
    a = _dot(x + pe_ref[c, 0], wbig_ref[c, 0])
    b = _dot(x + pe_ref[c, 1], wbig_ref[c, 1])
    return a + pltpu.roll(b, x.shape[0] - 1, 0)


def _nsa_prompt_body(qa_ref, kvn_ref, kvw_ref, ga_ref, bc_ref, bt_ref, wbig_ref, pe_ref, mselt_ref, o_ref,
                     rows_s, xflat_s, kc_s, vc_s, q8_s, sel8_s, ocmp_s, osel_s, m_s, l_s, acc_s, *, seq):
    qt = pl.program_id(1)
    t0 = qt * TQ
    n_seg = seq // CMP_STRIDE
    n_cmp = n_seg - CMP_BLK // CMP_STRIDE + 1
    n_sel = -(-seq // SEL_BLK)
    nr = _round_up(n_sel, SUBLANES)
    nc = kc_s.shape[0]
    cols = NSA_HEADS * TQ
    scale = HEAD_DIM ** -0.5

    @pl.when(qt == 0)
    def _():
        for c in range(2):
            if nc > n_seg:
                xflat_s[c] = jnp.zeros(xflat_s.shape[1:], f32)
            rows_s[...] = kvn_ref[0, :, c * LANES:(c + 1) * LANES]
            for r in range(CMP_STRIDE):
                xflat_s[c, 0:n_seg, r * LANES:(r + 1) * LANES] = rows_s[pl.ds(r, n_seg, stride=CMP_STRIDE), :]
        kc_s[...] = _compress(xflat_s, 0, pe_ref, wbig_ref)
        vc_s[...] = _compress(xflat_s, 1, pe_ref, wbig_ref)
        sel8_s[...] = jnp.zeros(sel8_s.shape, f32)

    qtr = (qa_ref[...] * scale).T
    zero = jnp.zeros((HEAD_DIM, TQ), f32)
    for h in range(NSA_HEADS):
        qh = qtr[h * HEAD_DIM:(h + 1) * HEAD_DIM, :]
        blk = jnp.concatenate([qh, zero] if h // NSA_HPG == 0 else [zero, qh], axis=0)
        q8_s[:, h * TQ:(h + 1) * TQ] = blk.astype(q8_s.dtype)
    q8 = q8_s[...]
    tpos = t0 + (_iota((1, cols), 1) & (TQ - 1))

    nrow = _iota((nc, cols), 0)
    mask_c = (nrow < n_cmp) & (tpos >= nrow * CMP_STRIDE + (CMP_BLK - 1))
    sc = jnp.where(mask_c, _dot(kc_s[...], q8) + bc_ref[0], NEG)
    e = jnp.exp(sc - jnp.max(sc, axis=0, keepdims=True))
    p_c = jnp.where(mask_c, e / jnp.sum(e, axis=0, keepdims=True), 0.0)
    ocmp_s[...] = _dot_tn(vc_s[...], p_c)

    jb = _iota((nr, TQ), 0)
    cur = (t0 + _iota((nr, TQ), 1)) >> 6
    valid = (jb <= cur) & (jb < n_sel)
    forced = valid & ((jb == 0) | (jb == cur) | (jb == cur - 1))
    for g in range(NSA_GROUPS):
        imp = p_c[:, g * NSA_HPG * TQ:(g * NSA_HPG + 1) * TQ]
        for h in range(1, NSA_HPG):
            imp = imp + p_c[:, (g * NSA_HPG + h) * TQ:(g * NSA_HPG + h + 1) * TQ]
        blk = _dot_hi(mselt_ref[...], imp)[0:nr, :]
        score = jnp.where(forced, jnp.inf, jnp.where(valid, blk, -jnp.inf))
        sel = (valid & (_rank_rows(score, n_sel) < N_SEL)).astype(f32)
        for h in range(NSA_HPG):
            sel8_s[0:nr, (g * NSA_HPG + h) * TQ:(g * NSA_HPG + h + 1) * TQ] = sel

    krow = _iota((TQ, cols), 0)
    blocks_per_tile = TQ // SEL_BLK

    def sel_step(kt, carry):
        k = kvn_ref[0, pl.ds(kt * TQ, TQ), 2 * LANES:3 * LANES]
        v = kvn_ref[0, pl.ds(kt * TQ, TQ), 3 * LANES:4 * LANES]
        chosen = sel8_s[pl.ds(kt * blocks_per_tile, 1), :]
        for i in range(1, blocks_per_tile):
            chosen = jnp.where(krow < i * SEL_BLK, chosen, sel8_s[pl.ds(kt * blocks_per_tile + i, 1), :])
        mask = (chosen > 0.5) & (tpos >= kt * TQ + krow)
        _softmax_step_t(_dot(k, q8) + bt_ref[qt - kt], mask, v, m_s, l_s, acc_s)
        return carry

    _softmax_reset(m_s, l_s, acc_s)
    lax.fori_loop(0, qt + 1, sel_step, 0)
    osel_s[...] = acc_s[...] / l_s[...]

    def win_step(kt, carry):
        k = kvw_ref[0, pl.ds(kt * TQ, TQ), 0:LANES]
        v = kvw_ref[0, pl.ds(kt * TQ, TQ), LANES:2 * LANES]
        d = tpos - (kt * TQ + krow)
        _softmax_step_t(_dot(k, q8) + bt_ref[qt - kt], (d >= 0) & (d <= WINDOW), v, m_s, l_s, acc_s)
        return carry

    _softmax_reset(m_s, l_s, acc_s)
    lax.fori_loop(jnp.maximum(qt - WINDOW // TQ, 0), qt + 1, win_step, 0)
    owin = acc_s[...] / l_s[...]

    gat = ga_ref[...].T
    pieces = []
    for h in range(NSA_HEADS):
        g = h // NSA_HPG
        c0, c1 = h * TQ, (h + 1) * TQ
        o = (gat[3 * h:3 * h + 1, :] * ocmp_s[:, c0:c1] + gat[3 * h + 1:3 * h + 2, :] * osel_s[:, c0:c1]
             + gat[3 * h + 2:3 * h + 3, :] * owin[:, c0:c1])
        pieces.append(o[g * HEAD_DIM:(g + 1) * HEAD_DIM, :])
    o_ref[...] = jnp.concatenate(pieces, axis=0).T


def _nsa_prompt(qa, kvn, kvw, ga, bias_c, bias_t, wbig, pe, batch, seq):
    nq = seq // TQ
    n_seg = seq // CMP_STRIDE
    nc = max(LANES, n_seg)
    cols = NSA_HEADS * TQ
    mselt = _cmp_to_sel_matrix(nc, LANES).T
    seg_w = CMP_STRIDE * NSA_GROUPS * HEAD_DIM
    vm = (2 * (seq * 768 * 4 + bias_t.size * 4 + wbig.size * 2 + cols * nc * 4) + 2 * nc * seg_w * 4
          + seq * LANES * 4 + 8 * cols * LANES * 4 + (12 << 20))
    return pl.pallas_call(
        functools.partial(_nsa_prompt_body, seq=seq),
        grid=(batch, nq),
        in_specs=[pl.BlockSpec((TQ, qa.shape[1]), lambda b, q: (b * nq + q, 0)),
                  pl.BlockSpec((1, seq, kvn.shape[2]), lambda b, q: (b, 0, 0)),
                  pl.BlockSpec((1, seq, kvw.shape[2]), lambda b, q: (b, 0, 0)),
                  pl.BlockSpec((TQ, LANES), lambda b, q: (b * nq + q, 0)),
                  pl.BlockSpec((1, nc, cols), lambda b, q: (q, 0, 0)),
                  pl.BlockSpec(bias_t.shape, lambda b, q: (0, 0, 0)),
                  pl.BlockSpec(wbig.shape, lambda b, q: (0, 0, 0, 0)),
                  pl.BlockSpec(pe.shape, lambda b, q: (0, 0, 0, 0)),
                  pl.BlockSpec(mselt.shape, lambda b, q: (0, 0))],
        out_specs=pl.BlockSpec((TQ, qa.shape[1]), lambda b, q: (b * nq + q, 0)),
        out_shape=jax.ShapeDtypeStruct(qa.shape, f32),
        scratch_shapes=[pltpu.VMEM((seq, LANES), f32), pltpu.VMEM((2, nc, seg_w), f32),
                        pltpu.VMEM((nc, LANES), f32), pltpu.VMEM((nc, LANES), f32),
                        pltpu.VMEM((LANES, cols), _MXU), pltpu.VMEM((LANES, cols), f32),
                        pltpu.VMEM((LANES, cols), f32), pltpu.VMEM((LANES, cols), f32),
                        pltpu.VMEM((1, cols), f32), pltpu.VMEM((1, cols), f32), pltpu.VMEM((LANES, cols), f32)],
        compiler_params=_vmem(vm),
        name="nsa_prompt",
    )(qa, kvn, kvw, ga, bias_c, bias_t, wbig, pe, mselt)


def _moba_prompt_body(qb_ref, kvb_ref, bt_ref, o_ref, kmean_s, q8_s, sel8_s, m_s, l_s, acc_s, *, seq):
    qt = pl.program_id(1)
    t0 = qt * TQ
    nb = seq // MOBA_BLK
    nr = _round_up(nb, SUBLANES)
    cols = MOBA_HEADS * TQ
    kw = MOBA_HEADS * HEAD_DIM
    scale = HEAD_DIM ** -0.5
    pairs = MOBA_HEADS // 2

    @pl.when(qt == 0)
    def _():
        kmean_s[...] = jnp.zeros(kmean_s.shape, f32)
        for n in range(nb):
            blk = kvb_ref[0, n * MOBA_BLK:(n + 1) * MOBA_BLK, 0:kw]
            kmean_s[n:n + 1, :] = jnp.sum(blk, axis=0, keepdims=True) * (1.0 / MOBA_BLK)
        sel8_s[...] = jnp.zeros(sel8_s.shape, f32)

    qtr = qb_ref[...].T
    zero = jnp.zeros((HEAD_DIM, TQ), f32)
    nblk = _iota((nr, TQ), 0)
    own = (t0 + _iota((nr, TQ), 1)) >> 8
    past = nblk < own
    for h in range(MOBA_HEADS):
        qh = qtr[h * HEAD_DIM:(h + 1) * HEAD_DIM, :]
        blk = jnp.concatenate([qh, zero] if h % 2 == 0 else [zero, qh], axis=0)
        q8_s[:, h * TQ:(h + 1) * TQ] = (blk * scale).astype(q8_s.dtype)
        gate = _dot_hi(kmean_s[:, (h // 2) * LANES:(h // 2 + 1) * LANES], blk)[0:nr, :]
        score = jnp.where(past, gate, -jnp.inf)
        sel = (past & (_rank_rows(score, nb) < MOBA_TOPK)) | (nblk == own)
        sel8_s[0:nr, h * TQ:(h + 1) * TQ] = sel.astype(f32)

    tpos = t0 + (_iota((1, cols), 1) & (TQ - 1))
    krow = _iota((TQ, cols), 0)

    def step(kt, carry):
        s_parts = []
        for pr in range(pairs):
            k = kvb_ref[0, pl.ds(kt * TQ, TQ), pr * LANES:(pr + 1) * LANES]
            s_parts.append(_dot(k, q8_s[:, 2 * pr * TQ:2 * (pr + 1) * TQ]))
        s = jnp.concatenate(s_parts, axis=1) + bt_ref[qt - kt]
        chosen = sel8_s[pl.ds((kt * TQ) >> 8, 1), :] > 0.5
        mask = chosen & (tpos >= kt * TQ + krow)
        s = jnp.where(mask, s, NEG)
        m_old = m_s[...]
        m_new = jnp.maximum(m_old, jnp.max(s, axis=0, keepdims=True))
        alpha = jnp.exp(m_old - m_new)
        p = jnp.where(mask, jnp.exp(s - m_new), 0.0)
        l_s[...] = alpha * l_s[...] + jnp.sum(p, axis=0, keepdims=True)
        pv = []
        for pr in range(pairs):
            v = kvb_ref[0, pl.ds(kt * TQ, TQ), kw + pr * LANES:kw + (pr + 1) * LANES]
            pv.append(_dot_tn(v, p[:, 2 * pr * TQ:2 * (pr + 1) * TQ]))
        acc_s[...] = alpha * acc_s[...] + jnp.concatenate(pv, axis=1)
        m_s[...] = m_new
        return carry

    _softmax_reset(m_s, l_s, acc_s)
    lax.fori_loop(0, qt + 1, step, 0)
    o = acc_s[...] / l_s[...]
    pieces = []
    for h in range(MOBA_HEADS):
        half = h % 2
        pieces.append(o[half * HEAD_DIM:(half + 1) * HEAD_DIM, h * TQ:(h + 1) * TQ])
    o_ref[...] = jnp.concatenate(pieces, axis=0).T


def _moba_prompt(qb, kvb, bias_t, batch, seq):
    nq = seq // TQ
    cols = MOBA_HEADS * TQ
    vm = 2 * (seq * kvb.shape[2] * 4 + bias_t.size * 4) + 8 * cols * LANES * 4 + (12 << 20)
    return pl.pallas_call(
        functools.partial(_moba_prompt_body, seq=seq),
        grid=(batch, nq),
        in_specs=[pl.BlockSpec((TQ, qb.shape[1]), lambda b, q: (b * nq + q, 0)),
                  pl.BlockSpec((1, seq, kvb.shape[2]), lambda b, q: (b, 0, 0)),
                  pl.BlockSpec(bias_t.shape, lambda b, q: (0, 0, 0))],
        out_specs=pl.BlockSpec((TQ, qb.shape[1]), lambda b, q: (b * nq + q, 0)),
        out_shape=jax.ShapeDtypeStruct(qb.shape, f32),
        scratch_shapes=[pltpu.VMEM((LANES, MOBA_HEADS * HEAD_DIM), f32),
                        pltpu.VMEM((LANES, cols), _MXU), pltpu.VMEM((LANES, cols), f32),
                        pltpu.VMEM((1, cols), f32), pltpu.VMEM((1, cols), f32), pltpu.VMEM((LANES, cols), f32)],
        compiler_params=_vmem(vm),
        name="moba_prompt",
    )(qb, kvb, bias_t)


def _page_group(n_pages):
    for g in (8, 4, 2):
        if n_pages % g == 0:
            return g
    raise ValueError("the page count must be even")


def _stack_heads_nsa(q, q8_s):
    half = _iota((1, LANES), 1) >> 6
    for h in range(NSA_HEADS):
        g = h // NSA_HPG
        blk = q[:, (h // 2) * LANES:(h // 2 + 1) * LANES]
        if h % 2 != g:
            blk = pltpu.roll(blk, HEAD_DIM, 1)
        q8_s[h:h + 1, :] = jnp.where(half == g, blk, 0.0).astype(q8_s.dtype)


def _nsa_sample_body(pt_ref, *refs, n_pages, group):
    del pt_ref
    ka_refs, vb_refs = refs[:group], refs[group:2 * group]
    (qa_ref, kvn_ref, kvw_ref, st_ref, ga_ref, bc_ref, bs_ref, bn_ref, bw_ref, wbig_ref, pe_ref, msel_ref, o_ref,
     xflat_s, kc_s, vc_s, q8_s, sel8_s, ocmp_s, m_s, l_s, acc_s) = refs[2 * group:]
    ph = pl.program_id(1)
    p = pl.program_id(2)
    n_steps = n_pages // group
    t = n_pages * PAGE
    seg_per_page = PAGE // CMP_STRIDE
    n_seg = n_pages * seg_per_page
    n_cmp = n_seg - CMP_BLK // CMP_STRIDE + 1
    cur = t // SEL_BLK
    n_sel = cur + 1
    nc = kc_s.shape[0]
    ls = sel8_s.shape[1]
    keys = group * PAGE
    scale = HEAD_DIM ** -0.5

    def one_key(q8, row):
        return _dot_nt(q8, jnp.broadcast_to(row, (SUBLANES, LANES)))[:, 0:1]

    @pl.when(ph == 0)
    def _():
        if nc > n_seg:
            @pl.when(p == 0)
            def _():
                xflat_s[...] = jnp.zeros(xflat_s.shape, f32)
        for c, page_refs in ((0, ka_refs), (1, vb_refs)):
            for g, ref in enumerate(page_refs):
                row0 = pl.multiple_of((p * group + g) * seg_per_page, SUBLANES)
                for r in range(CMP_STRIDE):
                    xflat_s[c, pl.ds(row0, seg_per_page), r * LANES:(r + 1) * LANES] = (
                        ref[0, pl.ds(r, seg_per_page, stride=CMP_STRIDE), :])

        @pl.when(p == n_steps - 1)
        def _():
            kc_s[...] = _compress(xflat_s, 0, pe_ref, wbig_ref)
            vc_s[...] = _compress(xflat_s, 1, pe_ref, wbig_ref)
            _stack_heads_nsa(qa_ref[0] * scale, q8_s)
            q8 = q8_s[...]
            ncol = _iota((NSA_HEADS, nc), 1)
            mask_c = ncol < n_cmp
            sc = jnp.where(mask_c, _dot_nt(q8, kc_s[...]) + bc_ref[...], NEG)
            e = jnp.exp(sc - jnp.max(sc, axis=1, keepdims=True))
            p_c = jnp.where(mask_c, e / jnp.sum(e, axis=1, keepdims=True), 0.0)
            ocmp_s[...] = _dot(p_c, vc_s[...])
            same_group = ((_iota((NSA_HEADS, NSA_HEADS), 0) >> 2) == (_iota((NSA_HEADS, NSA_HEADS), 1) >> 2)).astype(f32)
            blk = _dot_hi(_dot_hi(same_group, p_c), msel_ref[...])
            jb = _iota((NSA_HEADS, ls), 1)
            valid = jb < n_sel
            forced = valid & ((jb == 0) | (jb == cur) | (jb == cur - 1))
            score = jnp.where(forced, jnp.inf, jnp.where(valid, blk, -jnp.inf))
            sel8_s[...] = (valid & (_rank_lanes(score, n_sel) < N_SEL)).astype(sel8_s.dtype)

    @pl.when(ph == 1)
    def _():
        q8 = q8_s[...]

        @pl.when(p == 0)
        def _():
            m_s[...] = one_key(q8, kvn_ref[0, :, 2 * LANES:3 * LANES]) + bn_ref[:, 0:1]
            l_s[...] = jnp.ones(l_s.shape, f32)
            v_new = kvn_ref[0, :, 3 * LANES:4 * LANES].astype(_MXU).astype(f32)
            acc_s[...] = jnp.broadcast_to(v_new, acc_s.shape)

        k = jnp.concatenate([r[0] for r in ka_refs], axis=0)
        v = jnp.concatenate([r[0] for r in vb_refs], axis=0)
        s = _dot_nt(q8, k) + bs_ref[0]
        jrow = _iota((ls, keys), 0)
        jcol = (p * keys + _iota((ls, keys), 1)) >> 6
        chosen = _dot(sel8_s[...], (jrow == jcol).astype(f32)) > 0.5
        _softmax_step(s, chosen, v, m_s, l_s, acc_s)

        @pl.when(p == n_steps - 1)
        def _():
            osel = acc_s[...] / l_s[...]
            sw = _dot_nt(q8, st_ref[0, :, 0:LANES]) + bw_ref[...]
            sn = one_key(q8, kvw_ref[0, :, 0:LANES]) + bn_ref[:, 0:1]
            mw = jnp.maximum(jnp.max(sw, axis=1, keepdims=True), sn)
            ew = jnp.exp(sw - mw)
            en = jnp.exp(sn - mw)
            v_new = kvw_ref[0, :, LANES:2 * LANES].astype(_MXU).astype(f32)
            owin = (_dot(ew, st_ref[0, :, LANES:2 * LANES]) + en * v_new) / (jnp.sum(ew, axis=1, keepdims=True) + en)
            ga = ga_ref[0]
            for h in range(NSA_HEADS):
                g = h // NSA_HPG
                o = (ga[:, 3 * h:3 * h + 1] * ocmp_s[h:h + 1, :] + ga[:, 3 * h + 1:3 * h + 2] * osel[h:h + 1, :]
                     + ga[:, 3 * h + 2:3 * h + 3] * owin[h:h + 1, :])
                o_ref[0, :, h * HEAD_DIM:(h + 1) * HEAD_DIM] = o[:, g * HEAD_DIM:(g + 1) * HEAD_DIM]


def _nsa_sample(page_table, cache, qa, kvn, kvw, state, ga, tab, wbig, pe):
    bs, n_pages = page_table.shape
    group = _page_group(n_pages)
    n_steps = n_pages // group
    t = n_pages * PAGE
    win = state.shape[1]
    n_cmp = n_pages * (PAGE // CMP_STRIDE) - CMP_BLK // CMP_STRIDE + 1
    nc = max(LANES, n_pages * (PAGE // CMP_STRIDE))
    ls = _round_up(t // SEL_BLK + 1, LANES)
    msel = _cmp_to_sel_matrix(nc, ls)
    d_last = t - (CMP_STRIDE * (n_cmp - 1) + CMP_BLK - 1)
    bias_c = tab[:, d_last:t - CMP_BLK + 2:CMP_STRIDE][:, ::-1]
    bias_c = jnp.pad(bias_c, ((0, 0), (0, nc - n_cmp)))
    bias_s = jnp.transpose(tab[:, 1:t + 1][:, ::-1].reshape(NSA_HEADS, n_steps, group * PAGE), (1, 0, 2))
    bias_n = jnp.broadcast_to(tab[:, 0:1], (NSA_HEADS, LANES))
    bias_w = tab[:, 1:win + 1][:, ::-1]
    seg_w = CMP_STRIDE * NSA_GROUPS * HEAD_DIM
    req = lambda b, ph, p, pt: (b, 0, 0)
    c2 = lambda b, ph, p, pt: (0, 0)
    c4 = lambda b, ph, p, pt: (0, 0, 0, 0)

    def page_spec(g, lane_block):
        return pl.BlockSpec((1, PAGE, LANES), lambda b, ph, p, pt: (pt[b, p * group + g], 0, 2 * ph + lane_block))

    grid_spec = pltpu.PrefetchScalarGridSpec(
        num_scalar_prefetch=1,
        grid=(bs, 2, n_steps),
        in_specs=([page_spec(g, 0) for g in range(group)] + [page_spec(g, 1) for g in range(group)]
                  + [pl.BlockSpec((1, 1, qa.shape[2]), req), pl.BlockSpec((1, 1, kvn.shape[2]), req),
                     pl.BlockSpec((1, 1, kvw.shape[2]), req), pl.BlockSpec((1, win, state.shape[2]), req),
                     pl.BlockSpec((1, 1, LANES), req),
                     pl.BlockSpec(bias_c.shape, c2),
                     pl.BlockSpec((1, NSA_HEADS, group * PAGE), lambda b, ph, p, pt: (p * ph, 0, 0)),
                     pl.BlockSpec(bias_n.shape, c2), pl.BlockSpec(bias_w.shape, c2),
                     pl.BlockSpec(wbig.shape, c4), pl.BlockSpec(pe.shape, c4), pl.BlockSpec(msel.shape, c2)]),
        out_specs=pl.BlockSpec((1, 1, qa.shape[2]), req),
        scratch_shapes=[pltpu.VMEM((2, nc, seg_w), f32), pltpu.VMEM((nc, LANES), f32), pltpu.VMEM((nc, LANES), f32),
                        pltpu.VMEM((NSA_HEADS, LANES), _MXU), pltpu.VMEM((NSA_HEADS, ls), _MXU),
                        pltpu.VMEM((NSA_HEADS, LANES), f32),
                        pltpu.VMEM((NSA_HEADS, 1), f32), pltpu.VMEM((NSA_HEADS, 1), f32),
                        pltpu.VMEM((NSA_HEADS, LANES), f32)])
    vm = (2 * nc * seg_w * 4 + 2 * (wbig.size * 2 + msel.size * 4 + win * 256 * 4 + 2 * group * PAGE * LANES * 4)
          + (16 << 20))
    return pl.pallas_call(
        functools.partial(_nsa_sample_body, n_pages=n_pages, group=group),
        grid_spec=grid_spec,
        out_shape=jax.ShapeDtypeStruct(qa.shape, f32),
        compiler_params=_vmem(vm),
        name="nsa_sample",
    )(page_table, *([cache] * (2 * group)), qa, kvn, kvw, state, ga, bias_c, bias_s, bias_n, bias_w, wbig, pe, msel)


def _moba_sample_body(pt_ref, *refs, n_pages, group):
    del pt_ref
    k_refs, v_refs = refs[:group], refs[group:2 * group]
    qb_ref, kvb_ref, bs_ref, bn_ref, o_ref, kmean_s, q8_s, sel8_s, m_s, l_s, acc_s = refs[2 * group:]
    ph = pl.program_id(1)
    p = pl.program_id(2)
    n_steps = n_pages // group
    t = n_pages * PAGE
    kw = MOBA_HEADS * HEAD_DIM
    pages_per_blk = MOBA_BLK // PAGE
    blks_per_step = group // pages_per_blk
    own = t // MOBA_BLK
    keys = group * PAGE
    scale = HEAD_DIM ** -0.5
    head_lane = (_iota((MOBA_HEADS, kw), 1) >> 6) == _iota((MOBA_HEADS, kw), 0)

    @pl.when(ph == 0)
    def _():
        @pl.when(p == 0)
        def _():
            kmean_s[...] = jnp.zeros(kmean_s.shape, f32)

        for j in range(blks_per_step):
            tot = jnp.sum(k_refs[j * pages_per_blk][0], axis=0, keepdims=True)
            for i in range(1, pages_per_blk):
                tot = tot + jnp.sum(k_refs[j * pages_per_blk + i][0], axis=0, keepdims=True)
            kmean_s[pl.ds(p * blks_per_step + j, 1), :] = tot * (1.0 / MOBA_BLK)

        @pl.when(p == n_steps - 1)
        def _():
            q8 = jnp.where(head_lane, jnp.broadcast_to(qb_ref[0], (MOBA_HEADS, kw)), 0.0)
            q8_s[...] = q8
            gate = _dot_nt_hi(q8, kmean_s[...])
            nblk = _iota((MOBA_HEADS, LANES), 1)
            past = nblk < own
            score = jnp.where(past, gate, -jnp.inf)
            sel8_s[...] = (past & (_rank_lanes(score, own) < MOBA_TOPK)).astype(sel8_s.dtype)

    @pl.when(ph == 1)
    def _():
        q8 = q8_s[...] * scale

        @pl.when(p == 0)
        def _():
            k_new = jnp.broadcast_to(kvb_ref[0, :, 0:kw], (SUBLANES, kw))
            m_s[...] = _dot_nt(q8, k_new)[:, 0:1] + bn_ref[:, 0:1]
            l_s[...] = jnp.ones(l_s.shape, f32)
            acc_s[...] = jnp.broadcast_to(kvb_ref[0, :, kw:2 * kw].astype(_MXU).astype(f32), acc_s.shape)

        k = jnp.concatenate([r[0] for r in k_refs], axis=0)
        v = jnp.concatenate([r[0] for r in v_refs], axis=0)
        s = _dot_nt(q8, k) + bs_ref[0]
        jcol = (p * keys + _iota((LANES, keys), 1)) >> 8
        chosen = _dot(sel8_s[...], (_iota((LANES, keys), 0) == jcol).astype(f32)) > 0.5
        _softmax_step(s, chosen, v, m_s, l_s, acc_s)

        @pl.when(p == n_steps - 1)
        def _():
            o8 = jnp.where(head_lane, acc_s[...] / l_s[...], 0.0)
            o_ref[0] = jnp.sum(o8, axis=0, keepdims=True)


def _moba_sample(page_table, cache, qb, kvb, tab):
    bs, n_pages = page_table.shape
    group = _page_group(n_pages)
    n_steps = n_pages // group
    t = n_pages * PAGE
    kw = MOBA_HEADS * HEAD_DIM
    bias_s = jnp.transpose(tab[:, 1:t + 1][:, ::-1].reshape(MOBA_HEADS, n_steps, group * PAGE), (1, 0, 2))
    bias_n = jnp.broadcast_to(tab[:, 0:1], (MOBA_HEADS, LANES))
    req = lambda b, ph, p, pt: (b, 0, 0)

    def k_spec(g):
        return pl.BlockSpec((1, PAGE, kw), lambda b, ph, p, pt: (pt[b, p * group + g], 0, 0))

    def v_spec(g):
        return pl.BlockSpec((1, PAGE, kw), lambda b, ph, p, pt: (pt[b, (p * group + g) * ph], 0, 1))

    grid_spec = pltpu.PrefetchScalarGridSpec(
        num_scalar_prefetch=1,
        grid=(bs, 2, n_steps),
        in_specs=([k_spec(g) for g in range(group)] + [v_spec(g) for g in range(group)]
                  + [pl.BlockSpec((1, 1, kw), req), pl.BlockSpec((1, 1, 2 * kw), req),
                     pl.BlockSpec((1, MOBA_HEADS, group * PAGE), lambda b, ph, p, pt: (p * ph, 0, 0)),
                     pl.BlockSpec(bias_n.shape, lambda b, ph, p, pt: (0, 0))]),
        out_specs=pl.BlockSpec((1, 1, kw), req),
        scratch_shapes=[pltpu.VMEM((LANES, kw), f32), pltpu.VMEM((MOBA_HEADS, kw), f32),
                        pltpu.VMEM((MOBA_HEADS, LANES), _MXU),
                        pltpu.VMEM((MOBA_HEADS, 1), f32), pltpu.VMEM((MOBA_HEADS, 1), f32),
                        pltpu.VMEM((MOBA_HEADS, kw), f32)])
    return pl.pallas_call(
        functools.partial(_moba_sample_body, n_pages=n_pages, group=group),
        grid_spec=grid_spec,
        out_shape=jax.ShapeDtypeStruct(qb.shape, f32),
        compiler_params=_vmem(2 * 2 * group * PAGE * kw * 4 + (24 << 20)),
        name="moba_sample",
    )(page_table, *([cache] * (2 * group)), qb, kvb, bias_s, bias_n)


def _layer_norm(x, g, b):
    mu = jnp.mean(x, axis=-1, keepdims=True)
    xc = x - mu
    var = jnp.mean(xc * xc, axis=-1, keepdims=True)
    return xc * lax.rsqrt(var + LN_EPS) * g + b


def _merge_body(oa_ref, ob_ref, gm_ref, x_ref, wa_ref, wb_ref, wo_ref, g_ref, b_ref, h_ref, ht_ref, *, alpha):
    d = x_ref.shape[1]
    ya = _dot(oa_ref[...], wa_ref[...])
    yb = _dot(ob_ref[...], wb_ref[...])
    gm = gm_ref[...]
    mix = _dot(gm[:, :d] * ya + gm[:, d:] * yb, wo_ref[...])
    h = _layer_norm(alpha * x_ref[...] + mix, g_ref[...], b_ref[...])
    h_ref[...] = h
    ht_ref[...] = h.T.astype(ht_ref.dtype)


def _merge(oa, ob, gm, x2d, wa, wb, wo, g, b, alpha):
    n, d = x2d.shape
    tm = 256 if n % 256 == 0 else n
    row = lambda i: (i, 0)
    full = lambda i: (0, 0)
    vm = 2 * (tm * (oa.shape[1] + ob.shape[1] + gm.shape[1] + 2 * d) * 4 + tm * d * 2
              + (wa.size + wb.size + wo.size) * 2) + (16 << 20)
    return pl.pallas_call(
        functools.partial(_merge_body, alpha=alpha),
        grid=(n // tm,),
        in_specs=[pl.BlockSpec((tm, oa.shape[1]), row), pl.BlockSpec((tm, ob.shape[1]), row),
                  pl.BlockSpec((tm, gm.shape[1]), row), pl.BlockSpec((tm, d), row),
                  pl.BlockSpec(wa.shape, full), pl.BlockSpec(wb.shape, full), pl.BlockSpec(wo.shape, full),
                  pl.BlockSpec((1, d), full), pl.BlockSpec((1, d), full)],
        out_specs=[pl.BlockSpec((tm, d), row), pl.BlockSpec((d, tm), lambda i: (0, i))],
        out_shape=[jax.ShapeDtypeStruct((n, d), f32), jax.ShapeDtypeStruct((d, n), _MXU)],
        compiler_params=_vmem(vm),
        name="merge",
    )(oa, ob, gm, x2d, wa, wb, wo, g, b)


_NOT_TOP = 99.0
_CAND_B = tuple(PEER_TOPK // (a + 1) for a in range(SUBLANES))


def _extract_top(work_s, n_rounds, on_round, one_per_round):
    rows = _iota(work_s.shape, 0)

    def body(r, carry):
        w = work_s[...]
        m = jnp.max(w, axis=0, keepdims=True)
        hit = w == m
        if one_per_round:
            hit = rows == jnp.min(jnp.where(hit, rows, work_s.shape[0]), axis=0, keepdims=True)
        on_round(r, m, hit)
        work_s[...] = jnp.where(hit, -jnp.inf, w)
        return carry

    lax.fori_loop(0, n_rounds, body, 0)


def _top_rounds(work_s, init, on_round, n_hit):
    init()
    _extract_top(work_s, PEER_TOPK, on_round, one_per_round=False)
    bad = jnp.max(jnp.abs(n_hit() - float(PEER_TOPK)))

    @pl.when(bad > 0.5)
    def _():
        init()
        _extract_top(work_s, PEER_TOPK, on_round, one_per_round=True)


def _peer_select(ht_ref, wqt_ref, keys_ref, qt_s, rank_s, exp_s, sv_s, work_s, cwork_s, cand_s, cnt_s, f0_s,
                 rk1_s, e1_s):
    tn = ht_ref.shape[1]
    nk = PEER_N_KEYS
    qt_s[...] = jnp.dot(wqt_ref[...], ht_ref[...], preferred_element_type=f32)

    def per_half(hc, carry):
        def scores():
            return _dot(keys_ref[hc], qt_s[pl.ds(pl.multiple_of(hc * PEER_HALF, PEER_HALF), PEER_HALF), :])

        def init():
            work_s[...] = scores()
            rank_s[hc] = jnp.full((nk, tn), _NOT_TOP, f32)

        def on_round(r, m, hit):
            sv_s[hc, pl.ds(r, 1), :] = m
            rank_s[hc] = jnp.where(hit, jnp.asarray(r, f32), rank_s[hc])

        def n_hit():
            return jnp.sum(jnp.where(rank_s[hc] < _NOT_TOP, 1.0, 0.0), axis=0, keepdims=True)

        _top_rounds(work_s, init, on_round, n_hit)
        exp_s[hc] = jnp.exp(scores() - sv_s[hc, 0:1, :])
        return carry

    lax.fori_loop(0, 2 * PEER_HEADS, per_half, 0)

    n_cand = cand_s.shape[0]

    def per_head(h, carry):
        sv0 = sv_s[2 * h]
        sv1 = sv_s[2 * h + 1]
        e0 = jnp.exp(sv0 - sv0[0:1, :])
        e1 = jnp.exp(sv1 - sv1[0:1, :])
        brow = _iota((SUBLANES, tn), 0)
        tiles, etiles = [sv0[0:1, :] + sv1, ], [e0[0:1, :] * e1]
        for a in range(1, SUBLANES):
            ok = brow < _CAND_B[a]
            tiles.append(jnp.where(ok, sv0[a:a + 1, :] + sv1[0:SUBLANES, :], -jnp.inf))
            etiles.append(jnp.where(ok, e0[a:a + 1, :] * e1[0:SUBLANES, :], 0.0))
        tiles.append(sv0[SUBLANES:, :] + sv1[0:1, :])
        etiles.append(e0[SUBLANES:, :] * e1[0:1, :])
        cands = jnp.concatenate(tiles, axis=0)

        def init():
            cwork_s[...] = cands
            cand_s[...] = jnp.zeros(cand_s.shape, f32)

        def on_round(r, m, hit):
            cand_s[...] = jnp.where(hit, 1.0, cand_s[...])

        _top_rounds(cwork_s, init, on_round, lambda: jnp.sum(cand_s[...], axis=0, keepdims=True))
        chosen = cand_s[...]
        z = jnp.sum(chosen * jnp.concatenate(etiles, axis=0), axis=0, keepdims=True)
        cnts = [jnp.sum(chosen[0:PEER_TOPK], axis=0, keepdims=True)]
        for a in range(1, SUBLANES):
            r0 = PEER_TOPK + (a - 1) * SUBLANES
            cnts.append(jnp.sum(chosen[r0:r0 + SUBLANES], axis=0, keepdims=True))
        tail = chosen[n_cand - SUBLANES:]
        cnts += [tail[a:a + 1, :] for a in range(SUBLANES)]
        ra = rank_s[2 * h]
        c0 = jnp.zeros((nk, tn), f32)
        for a in range(PEER_TOPK):
            c0 = jnp.where(ra == float(a), cnts[a], c0)
        cnt_s[h] = c0
        f0_s[h] = exp_s[2 * h] / z
        rk1_s[h] = rank_s[2 * h + 1].astype(rk1_s.dtype)
        e1_s[h] = exp_s[2 * h + 1].astype(e1_s.dtype)
        return carry

    lax.fori_loop(0, PEER_HEADS, per_head, 0)


def _gelu_t(u_tile, ht):
    at = jnp.dot(u_tile, ht, preferred_element_type=f32)
    return (0.5 * at * (1.0 + lax.erf(at * math.sqrt(0.5)))).astype(_MXU)


def _peer_body(ht_ref, h_ref, wqt_ref, keys_ref, u0_ref, un_ref, vt_ref, g_ref, b_ref, y_ref,
               qt_s, rank_s, exp_s, sv_s, work_s, cwork_s, cand_s, cnt_s, f0_s, rk1_s, e1_s, gel_s, wc_s, acc_s,
               *, alpha, ti):
    e = pl.program_id(1)
    last = pl.num_programs(1) - 1
    tn = ht_ref.shape[1]

    @pl.when(e == 0)
    def _():
        _peer_select(ht_ref, wqt_ref, keys_ref, qt_s, rank_s, exp_s, sv_s, work_s, cwork_s, cand_s, cnt_s, f0_s,
                     rk1_s, e1_s)
        acc_s[...] = jnp.zeros(acc_s.shape, f32)
        gel_s[0] = _gelu_t(u0_ref[...], ht_ref[...])

    for ii in range(ti):
        i = e * ti + ii
        r0, r1 = ii * PEER_N_KEYS, (ii + 1) * PEER_N_KEYS
        w = jnp.zeros((PEER_N_KEYS, tn), _MXU)
        for h in range(PEER_HEADS):
            cnt = cnt_s[h, pl.ds(i, 1), :].astype(_MXU)
            f0 = f0_s[h, pl.ds(i, 1), :].astype(_MXU)
            w = w + jnp.where(rk1_s[h] < cnt, e1_s[h], jnp.zeros((), _MXU)) * f0
        wc_s[r0:r1, :] = w * gel_s[e % 2, r0:r1, :]
        if ii % 2 == 1:
            k0, k1 = (ii - 1) * PEER_N_KEYS, (ii + 1) * PEER_N_KEYS
            acc_s[...] += jnp.dot(vt_ref[:, k0:k1], wc_s[k0:k1, :], preferred_element_type=f32)
    gel_s[(e + 1) % 2] = _gelu_t(un_ref[...], ht_ref[...])

    @pl.when(e == last)
    def _():
        y_ref[...] = _layer_norm(alpha * h_ref[...] + acc_s[...].T, g_ref[...], b_ref[...])


def _peer(ht, h, wqt, keys, u, vt, g, b, alpha):
    d, n = ht.shape
    n_exp = u.shape[0]
    tn = 512 if n % 512 == 0 else n
    ti = 4
    te = ti * PEER_N_KEYS
    n_half = 2 * PEER_HEADS
    n_cand = PEER_TOPK + (SUBLANES - 1) * SUBLANES + SUBLANES
    tok = lambda t, e: (0, t)
    full2 = lambda t, e: (0, 0)
    scratch = [pltpu.VMEM((wqt.shape[0], tn), f32),
               pltpu.VMEM((n_half, PEER_N_KEYS, tn), f32), pltpu.VMEM((n_half, PEER_N_KEYS, tn), f32),
               pltpu.VMEM((n_half, PEER_TOPK, tn), f32), pltpu.VMEM((PEER_N_KEYS, tn), f32),
               pltpu.VMEM((n_cand, tn), f32), pltpu.VMEM((n_cand, tn), f32),
               pltpu.VMEM((PEER_HEADS, PEER_N_KEYS, tn), f32), pltpu.VMEM((PEER_HEADS, PEER_N_KEYS, tn), f32),
               pltpu.VMEM((PEER_HEADS, PEER_N_KEYS, tn), _MXU), pltpu.VMEM((PEER_HEADS, PEER_N_KEYS, tn), _MXU),
               pltpu.VMEM((2, te, tn), _MXU), pltpu.VMEM((te, tn), _MXU), pltpu.VMEM((d, tn), f32)]
    vm = (wqt.shape[0] * tn * 4 + 2 * n_half * PEER_N_KEYS * tn * 4 + 2 * PEER_HEADS * PEER_N_KEYS * tn * 6
          + 3 * te * tn * 2 + d * tn * 4
          + 2 * (d * tn * 2 + 2 * tn * d * 4 + wqt.size * 2 + keys.size * 2 + 3 * te * d * 2) + (14 << 20))
    n_tiles = n_exp // te
    return pl.pallas_call(
        functools.partial(_peer_body, alpha=alpha, ti=ti),
        grid=(n // tn, n_tiles),
        in_specs=[pl.BlockSpec((d, tn), tok),
                  pl.BlockSpec((tn, d), lambda t, e: (t, 0)),
                  pl.BlockSpec(wqt.shape, full2),
                  pl.BlockSpec(keys.shape, lambda t, e: (0, 0, 0)),
                  pl.BlockSpec((te, d), full2),
                  pl.BlockSpec((te, d), lambda t, e: (jnp.minimum(e + 1, n_tiles - 1), 0)),
                  pl.BlockSpec((d, te), lambda t, e: (0, e)),
                  pl.BlockSpec((1, d), full2), pl.BlockSpec((1, d), full2)],
        out_specs=pl.BlockSpec((tn, d), lambda t, e: (t, 0)),
        out_shape=jax.ShapeDtypeStruct((n, d), f32),
        scratch_shapes=scratch,
        compiler_params=_vmem(vm),
        name="peer",
    )(ht, h, wqt, keys, u, u, vt, g, b)


def kernel(x_prompt, x_sample, cache_nsa, cache_moba, state_nsa_win, page_table, w_in, nsa_cmp_pe, nsa_cmp_w, w_branch_a, w_branch_b, w_out, ln1_g, ln1_b, peer_wq, peer_keys, peer_u, peer_v, ln2_g, ln2_b, rel_bias):
    depth = w_in.shape[0]
    batch, seq, d = x_prompt.shape
    bs, dec_seq, _ = x_sample.shape
    n_pages = page_table.shape[1]
    past = n_pages * PAGE
    win_len = state_nsa_win.shape[2]
    assert dec_seq == 1 and cache_nsa.shape[2] == PAGE and cache_moba.shape[2] == PAGE
    assert seq % MOBA_BLK == 0 and past % MOBA_BLK == 0 and win_len == WINDOW and seq % TQ == 0
    alpha = (2 * depth) ** 0.25
    nq = seq // TQ
    n_tok = batch * seq

    tab = _bias_by_distance(rel_bias, max(seq, past + 1) + TQ)
    tab_a, tab_b = tab[:NSA_HEADS], tab[NSA_HEADS:]
    tiles_a, tiles_b = _bias_tiles_t(tab_a, nq), _bias_tiles_t(tab_b, nq)
    cmp_bias = _nsa_cmp_bias_t(tab_a, seq, max(LANES, seq // CMP_STRIDE))

    xp = x_prompt.reshape(n_tok, d)
    xs = x_sample.reshape(bs, d)
    outs = [[] for _ in range(6)]
    for l in range(depth):
        w_packed = _pack_w_in(w_in[l], d)
        wbig, pe = _pack_cmp(nsa_cmp_pe[l], nsa_cmp_w[l])
        wa, wb, wo = w_branch_a[l].astype(_MXU), w_branch_b[l].astype(_MXU), w_out[l].astype(_MXU)
        wqt = peer_wq[l].T.astype(_MXU)
        keys = peer_keys[l].reshape(2 * PEER_HEADS, PEER_N_KEYS, PEER_HALF).astype(_MXU)
        u = peer_u[l].astype(_MXU)
        vt = peer_v[l].T.astype(_MXU)
        g1, b1, g2, b2 = ln1_g[l][None], ln1_b[l][None], ln2_g[l][None], ln2_b[l][None]

        def ffn(oa, ob, gm, x2d):
            h, ht = _merge(oa, ob, gm, x2d, wa, wb, wo, g1, b1, alpha)
            return _peer(ht, h, wqt, keys, u, vt, g2, b2, alpha)

        qa, kvn, kvw, qb, kvb, gm, ga = _project(xp, w_packed, d)
        oa = _nsa_prompt(qa, kvn.reshape(batch, seq, -1), kvw.reshape(batch, seq, -1), ga, cmp_bias, tiles_a,
                         wbig, pe, batch, seq)
        ob = _moba_prompt(qb, kvb.reshape(batch, seq, -1), tiles_b, batch, seq)
        xp = ffn(oa, ob, gm, xp)
        outs[0].append(kvn.reshape(batch, seq, 4, NSA_GROUPS, HEAD_DIM))
        outs[1].append(kvb.reshape(batch, seq, 2, MOBA_HEADS, HEAD_DIM))
        outs[2].append(kvw.reshape(batch, seq, 2, NSA_GROUPS, HEAD_DIM)[:, seq - min(WINDOW, seq):])

        qa, kvn, kvw, qb, kvb, gm, ga = _project(xs, w_packed, d)
        r3 = lambda a: a.reshape(bs, 1, -1)
        state = state_nsa_win[l].reshape(bs, win_len, -1)
        oa = _nsa_sample(page_table, cache_nsa[l].reshape(-1, PAGE, 4 * NSA_GROUPS * HEAD_DIM), r3(qa), r3(kvn),
                         r3(kvw), state, r3(ga), tab_a, wbig, pe)
        ob = _moba_sample(page_table, cache_moba[l].reshape(-1, PAGE, 2 * MOBA_HEADS * HEAD_DIM), r3(qb), r3(kvb),
                          tab_b)
        xs = ffn(oa.reshape(bs, -1), ob.reshape(bs, -1), gm, xs)
        outs[3].append(kvn.reshape(bs, 1, 4, NSA_GROUPS, HEAD_DIM))
        outs[4].append(kvb.reshape(bs, 1, 2, MOBA_HEADS, HEAD_DIM))
        win_new = jnp.concatenate([state_nsa_win[l][:, 1:], kvw.reshape(bs, 1, 2, NSA_GROUPS, HEAD_DIM)], axis=1)
        outs[5].append(win_new)

    stacked = [jnp.stack(o) for o in outs]
    return (xp.reshape(batch, seq, d), xs.reshape(bs, 1, d), *stacked)
```

```python
import functools
import math

import jax
import jax.numpy as jnp
from jax import lax
from jax.experimental import pallas as pl
from jax.experimental.pallas import tpu as pltpu

HEAD_DIM = 64
NSA_HEADS = 8
NSA_GROUPS = 2
NSA_HPG = NSA_HEADS // NSA_GROUPS
CMP_BLK = 32
CMP_STRIDE = 16
SEL_BLK = 64
N_SEL = 16
WINDOW = 512
MOBA_HEADS = 8
MOBA_BLK = 256
MOBA_TOPK = 3
N_BUCKETS = 32
MAX_DISTANCE = 1024
PEER_HEADS = 8
PEER_N_KEYS = 128
PEER_HALF = 128
PEER_TOPK = 16
LN_EPS = 1e-5
NEG = -1e30

LANES = 128
SUBLANES = 8
VMEM_BYTES_V7X = 64 * 1024 * 1024

TQ = 128
PAGE = 128

f32 = jnp.float32
i32 = jnp.int32
_MXU = jnp.bfloat16
_HI = lax.Precision.HIGHEST


def _iota(shape, dim):
    return lax.broadcasted_iota(i32, shape, dim)


def _dot(a, b):
    return jnp.dot(a.astype(_MXU), b.astype(_MXU), preferred_element_type=f32)


def _dot_nt(a, b):
    return lax.dot_general(a.astype(_MXU), b.astype(_MXU), (((1,), (1,)), ((), ())),
                           preferred_element_type=f32)


def _dot_tn(a, b):
    return lax.dot_general(a.astype(_MXU), b.astype(_MXU), (((0,), (0,)), ((), ())), preferred_element_type=f32)


def _dot_hi(a, b):
    return jnp.dot(a, b, precision=_HI, preferred_element_type=f32)


def _dot_nt_hi(a, b):
    return lax.dot_general(a, b, (((1,), (1,)), ((), ())), precision=_HI, preferred_element_type=f32)


def _vmem(nbytes, flags=None):
    return pltpu.CompilerParams(vmem_limit_bytes=int(min(nbytes, VMEM_BYTES_V7X - (4 << 20))), flags=flags)


def _round_up(x, m):
    return -(-x // m) * m


def _rank_lanes(s, n):
    width = s.shape[1]
    j = _iota(s.shape, 1)
    rank = jnp.zeros(s.shape, i32)
    for r in range(1, n):
        lo = pltpu.roll(s, r, 1)
        rank = rank + jnp.where((j >= r) & (lo >= s), 1, 0)
        hi = pltpu.roll(s, width - r, 1)
        rank = rank + jnp.where((j + r < n) & (hi > s), 1, 0)
    return rank


def _rank_rows(s, n):
    j = _iota(s.shape, 0)
    rank = jnp.zeros(s.shape, i32)
    for k in range(n):
        row = s[k:k + 1, :]
        rank = rank + jnp.where((row > s) | ((row == s) & (j > k)), 1, 0)
    return rank


def _softmax_reset(m_s, l_s, acc_s):
    m_s[...] = jnp.full(m_s.shape, NEG, f32)
    l_s[...] = jnp.zeros(l_s.shape, f32)
    acc_s[...] = jnp.zeros(acc_s.shape, f32)


def _softmax_step(s, mask, v, m_s, l_s, acc_s):
    s = jnp.where(mask, s, NEG)
    m_old = m_s[...]
    m_new = jnp.maximum(m_old, jnp.max(s, axis=1, keepdims=True))
    alpha = jnp.exp(m_old - m_new)
    p = jnp.where(mask, jnp.exp(s - m_new), 0.0)
    l_s[...] = alpha * l_s[...] + jnp.sum(p, axis=1, keepdims=True)
    acc_s[...] = alpha * acc_s[...] + _dot(p, v)
    m_s[...] = m_new


def _softmax_step_t(s, mask, v, m_s, l_s, acc_s):
    s = jnp.where(mask, s, NEG)
    m_old = m_s[...]
    m_new = jnp.maximum(m_old, jnp.max(s, axis=0, keepdims=True))
    alpha = jnp.exp(m_old - m_new)
    p = jnp.where(mask, jnp.exp(s - m_new), 0.0)
    l_s[...] = alpha * l_s[...] + jnp.sum(p, axis=0, keepdims=True)
    acc_s[...] = alpha * acc_s[...] + _dot_tn(v, p)
    m_s[...] = m_new


_W_QA = NSA_HEADS * HEAD_DIM
_W_KVN = 4 * NSA_GROUPS * HEAD_DIM
_W_KVW = 2 * NSA_GROUPS * HEAD_DIM
_W_GA = 3 * NSA_HEADS
_W_QB = MOBA_HEADS * HEAD_DIM
_W_KVB = 2 * MOBA_HEADS * HEAD_DIM


def _proj_body(x_ref, w_ref, qa_ref, kvn_ref, kvw_ref, qb_ref, kvb_ref, gm_ref, ga_ref, *kv_t_refs):
    xb = x_ref[...].astype(_MXU)
    col = 0
    kvn_t, kvw_t, kvb_t = kv_t_refs if kv_t_refs else (None, None, None)
    for ref, act, t_ref in ((qa_ref, None, None), (kvn_ref, None, kvn_t), (kvw_ref, None, kvw_t),
                            (qb_ref, None, None), (kvb_ref, None, kvb_t),
                            (gm_ref, jax.nn.sigmoid, None), (ga_ref, jax.nn.sigmoid, None)):
        width = ref.shape[1]
        y = jnp.dot(xb, w_ref[:, col:col + width], preferred_element_type=f32)
        ref[...] = y if act is None else act(y)
        if t_ref is not None:
            t_ref[0] = y.T
        col += width


def _project(x2d, w_packed, d_model, seq=None):
    n = x2d.shape[0]
    tm = 256 if n % 256 == 0 else n
    widths = (_W_QA, _W_KVN, _W_KVW, _W_QB, _W_KVB, 2 * d_model, LANES)
    n_cols = sum(widths)
    out_specs = [pl.BlockSpec((tm, w), lambda i: (i, 0)) for w in widths]
    out_shape = [jax.ShapeDtypeStruct((n, w), f32) for w in widths]
    if seq is not None:
        tiles = seq // tm
        for w in (_W_KVN, _W_KVW, _W_KVB):
            out_specs.append(pl.BlockSpec((1, w, tm), lambda i: (i // tiles, 0, i % tiles)))
            out_shape.append(jax.ShapeDtypeStruct((n // seq, w, seq), f32))
    return pl.pallas_call(
        _proj_body,
        grid=(n // tm,),
        in_specs=[pl.BlockSpec((tm, d_model), lambda i: (i, 0)),
                  pl.BlockSpec((d_model, n_cols), lambda i: (0, 0))],
        out_specs=out_specs,
        out_shape=out_shape,
        compiler_params=_vmem(2 * (tm * d_model * 4 + d_model * n_cols * 2 + 2 * tm * n_cols * 4) + (8 << 20)),
        name="proj",
    )(x2d, w_packed)


def _pack_w_in(w_in, d_model):
    c_ga = _W_QA + _W_KVN + _W_KVW
    c_qb = c_ga + _W_GA
    pad = jnp.zeros((d_model, LANES - _W_GA), w_in.dtype)
    return jnp.concatenate([w_in[:, :c_ga], w_in[:, c_qb:], w_in[:, c_ga:c_qb], pad], axis=1).astype(_MXU)


def _t5_bucket(dist):
    n = jnp.maximum(dist, 0)
    max_exact = N_BUCKETS // 2
    nf = jnp.maximum(n, 1).astype(f32)
    large = max_exact + (jnp.log(nf / max_exact) / math.log(MAX_DISTANCE / max_exact)
                         * (N_BUCKETS - max_exact)).astype(i32)
    return jnp.where(n < max_exact, n, jnp.minimum(large, N_BUCKETS - 1)).astype(i32)


def _bias_by_distance(rel_bias, max_dist):
    return rel_bias[_t5_bucket(jnp.arange(max_dist, dtype=i32))].T


def _windows(v, starts, length):
    return jnp.stack([v[:, s:s + length] for s in starts], axis=1)


def _bias_tiles_t(tab, n_tiles):
    h = tab.shape[0]
    tabp = jnp.concatenate([jnp.zeros((h, TQ - 1), tab.dtype), tab], axis=1)
    g = _windows(tabp, range(TQ), n_tiles * TQ).reshape(h, TQ, n_tiles, TQ)[..., ::-1]
    return jnp.transpose(g, (2, 3, 0, 1)).reshape(n_tiles, TQ, h * TQ)


def _nsa_cmp_bias_t(tab, seq, nc):
    h = tab.shape[0]
    off = CMP_STRIDE * (nc - 1) + CMP_BLK - 1
    tabp = jnp.concatenate([jnp.zeros((h, off), tab.dtype), tab[:, :seq]], axis=1)
    w = _windows(tabp, [CMP_STRIDE * (nc - 1 - n) for n in range(nc)], seq)
    return jnp.transpose(w.reshape(h, nc, seq // TQ, TQ), (2, 1, 0, 3)).reshape(seq // TQ, nc, h * TQ)


def _pack_cmp(cmp_pe, cmp_w):
    r = CMP_BLK // CMP_STRIDE
    w = cmp_w.reshape(2, r, CMP_STRIDE, HEAD_DIM, HEAD_DIM)
    eye = jnp.eye(NSA_GROUPS, dtype=w.dtype)
    wbig = jnp.einsum("cmrde,gh->cmrgdhe", w, eye).reshape(2, r, CMP_STRIDE * NSA_GROUPS * HEAD_DIM,
                                                          NSA_GROUPS * HEAD_DIM)
    pe = cmp_pe.reshape(2, r, CMP_STRIDE, 1, HEAD_DIM)
    pe = jnp.broadcast_to(pe, (2, r, CMP_STRIDE, NSA_GROUPS, HEAD_DIM)).reshape(2, r, 1, -1)
    return wbig.astype(_MXU), pe.astype(f32)


def _cmp_to_sel_matrix(n_rows, n_cols):
    spb = SEL_BLK // CMP_STRIDE
    n = jnp.arange(n_rows, dtype=i32)[:, None]
    j = jnp.arange(n_cols, dtype=i32)[None, :]
    m = jnp.zeros((n_rows, n_cols), f32)
    for k in range(CMP_BLK // CMP_STRIDE):
        m = m + ((n + k) // spb == j).astype(f32)
    return m


def _compress(xflat_ref, c, pe_ref, wbig_ref):
    x = xflat_ref[c]
    a = _dot(x + pe_ref[c, 0], wbig_ref[c, 0])
    b = _dot(x + pe_ref[c, 1], wbig_ref[c, 1])
    return a + pltpu.roll(b, x.shape[0] - 1, 0)


def _nsa_prompt_body(qa_ref, kvn_ref, kvw_ref, ga_ref, bc_ref, bt_ref, wbig_ref, pe_ref, mselt_ref, o_ref,
                     rows_s, xflat_s, kc_s, vc_s, q8_s, sel8_s, ocmp_s, osel_s, m_s, l_s, acc_s, *, seq):
    qt = pl.program_id(1)
    t0 = qt * TQ
    n_seg = seq // CMP_STRIDE
    n_cmp = n_seg - CMP_BLK // CMP_STRIDE + 1
    n_sel = -(-seq // SEL_BLK)
    nr = _round_up(n_sel, SUBLANES)
    nc = kc_s.shape[0]
    cols = NSA_HEADS * TQ
    scale = HEAD_DIM ** -0.5

    @pl.when(qt == 0)
    def _():
        for c in range(2):
            if nc > n_seg:
                xflat_s[c] = jnp.zeros(xflat_s.shape[1:], f32)
            rows_s[...] = kvn_ref[0, :, c * LANES:(c + 1) * LANES]
            for r in range(CMP_STRIDE):
                xflat_s[c, 0:n_seg, r * LANES:(r + 1) * LANES] = rows_s[pl.ds(r, n_seg, stride=CMP_STRIDE), :]
        kc_s[...] = _compress(xflat_s, 0, pe_ref, wbig_ref)
        vc_s[...] = _compress(xflat_s, 1, pe_ref, wbig_ref)
        sel8_s[...] = jnp.zeros(sel8_s.shape, f32)

    qtr = (qa_ref[...] * scale).T
    zero = jnp.zeros((HEAD_DIM, TQ), f32)
    for h in range(NSA_HEADS):
        qh = qtr[h * HEAD_DIM:(h + 1) * HEAD_DIM, :]
        blk = jnp.concatenate([qh, zero] if h // NSA_HPG == 0 else [zero, qh], axis=0)
        q8_s[:, h * TQ:(h + 1) * TQ] = blk.astype(q8_s.dtype)
    q8 = q8_s[...]
    tpos = t0 + (_iota((1, cols), 1) & (TQ - 1))

    nrow = _iota((nc, cols), 0)
    mask_c = (nrow < n_cmp) & (tpos >= nrow * CMP_STRIDE + (CMP_BLK - 1))
    sc = jnp.where(mask_c, _dot(kc_s[...], q8) + bc_ref[0], NEG)
    e = jnp.exp(sc - jnp.max(sc, axis=0, keepdims=True))
    p_c = jnp.where(mask_c, e / jnp.sum(e, axis=0, keepdims=True), 0.0)
    ocmp_s[...] = _dot_tn(vc_s[...], p_c)

    jb = _iota((nr, TQ), 0)
    cur = (t0 + _iota((nr, TQ), 1)) >> 6
    valid = (jb <= cur) & (jb < n_sel)
    forced = valid & ((jb == 0) | (jb == cur) | (jb == cur - 1))
    for g in range(NSA_GROUPS):
        imp = p_c[:, g * NSA_HPG * TQ:(g * NSA_HPG + 1) * TQ]
        for h in range(1, NSA_HPG):
            imp = imp + p_c[:, (g * NSA_HPG + h) * TQ:(g * NSA_HPG + h + 1) * TQ]
        blk = _dot_hi(mselt_ref[...], imp)[0:nr, :]
        score = jnp.where(forced, jnp.inf, jnp.where(valid, blk, -jnp.inf))
        sel = (valid & (_rank_rows(score, n_sel) < N_SEL)).astype(f32)
        for h in range(NSA_HPG):
            sel8_s[0:nr, (g * NSA_HPG + h) * TQ:(g * NSA_HPG + h + 1) * TQ] = sel

    krow = _iota((TQ, cols), 0)
    blocks_per_tile = TQ // SEL_BLK

    def sel_step(kt, carry):
        k = kvn_ref[0, pl.ds(kt * TQ, TQ), 2 * LANES:3 * LANES]
        v = kvn_ref[0, pl.ds(kt * TQ, TQ), 3 * LANES:4 * LANES]
        chosen = sel8_s[pl.ds(kt * blocks_per_tile, 1), :]
        for i in range(1, blocks_per_tile):
            chosen = jnp.where(krow < i * SEL_BLK, chosen, sel8_s[pl.ds(kt * blocks_per_tile + i, 1), :])
        mask = (chosen > 0.5) & (tpos >= kt * TQ + krow)
        _softmax_step_t(_dot(k, q8) + bt_ref[qt - kt], mask, v, m_s, l_s, acc_s)
        return carry

    _softmax_reset(m_s, l_s, acc_s)
    lax.fori_loop(0, qt + 1, sel_step, 0)
    osel_s[...] = acc_s[...] / l_s[...]

    def win_step(kt, carry):
        k = kvw_ref[0, pl.ds(kt * TQ, TQ), 0:LANES]
        v = kvw_ref[0, pl.ds(kt * TQ, TQ), LANES:2 * LANES]
        d = tpos - (kt * TQ + krow)
        _softmax_step_t(_dot(k, q8) + bt_ref[qt - kt], (d >= 0) & (d <= WINDOW), v, m_s, l_s, acc_s)
        return carry

    _softmax_reset(m_s, l_s, acc_s)
    lax.fori_loop(jnp.maximum(qt - WINDOW // TQ, 0), qt + 1, win_step, 0)
    owin = acc_s[...] / l_s[...]

    gat = ga_ref[...].T
    pieces = []
    for h in range(NSA_HEADS):
        g = h // NSA_HPG
        c0, c1 = h * TQ, (h + 1) * TQ
        o = (gat[3 * h:3 * h + 1, :] * ocmp_s[:, c0:c1] + gat[3 * h + 1:3 * h + 2, :] * osel_s[:, c0:c1]
             + gat[3 * h + 2:3 * h + 3, :] * owin[:, c0:c1])
        pieces.append(o[g * HEAD_DIM:(g + 1) * HEAD_DIM, :])
    o_ref[...] = jnp.concatenate(pieces, axis=0).T


def _nsa_prompt(qa, kvn, kvw, ga, bias_c, bias_t, wbig, pe, batch, seq):
    nq = seq // TQ
    n_seg = seq // CMP_STRIDE
    nc = max(LANES, n_seg)
    cols = NSA_HEADS * TQ
    mselt = _cmp_to_sel_matrix(nc, LANES).T
    seg_w = CMP_STRIDE * NSA_GROUPS * HEAD_DIM
    vm = (2 * (seq * 768 * 4 + bias_t.size * 4 + wbig.size * 2 + cols * nc * 4) + 2 * nc * seg_w * 4
          + seq * LANES * 4 + 8 * cols * LANES * 4 + (12 << 20))
    return pl.pallas_call(
        functools.partial(_nsa_prompt_body, seq=seq),
        grid=(batch, nq),
        in_specs=[pl.BlockSpec((TQ, qa.shape[1]), lambda b, q: (b * nq + q, 0)),
                  pl.BlockSpec((1, seq, kvn.shape[2]), lambda b, q: (b, 0, 0)),
                  pl.BlockSpec((1, seq, kvw.shape[2]), lambda b, q: (b, 0, 0)),
                  pl.BlockSpec((TQ, LANES), lambda b, q: (b * nq + q, 0)),
                  pl.BlockSpec((1, nc, cols), lambda b, q: (q, 0, 0)),
                  pl.BlockSpec(bias_t.shape, lambda b, q: (0, 0, 0)),
                  pl.BlockSpec(wbig.shape, lambda b, q: (0, 0, 0, 0)),
                  pl.BlockSpec(pe.shape, lambda b, q: (0, 0, 0, 0)),
                  pl.BlockSpec(mselt.shape, lambda b, q: (0, 0))],
        out_specs=pl.BlockSpec((TQ, qa.shape[1]), lambda b, q: (b * nq + q, 0)),
        out_shape=jax.ShapeDtypeStruct(qa.shape, f32),
        scratch_shapes=[pltpu.VMEM((seq, LANES), f32), pltpu.VMEM((2, nc, seg_w), f32),
                        pltpu.VMEM((nc, LANES), f32), pltpu.VMEM((nc, LANES), f32),
                        pltpu.VMEM((LANES, cols), _MXU), pltpu.VMEM((LANES, cols), f32),
                        pltpu.VMEM((LANES, cols), f32), pltpu.VMEM((LANES, cols), f32),
                        pltpu.VMEM((1, cols), f32), pltpu.VMEM((1, cols), f32), pltpu.VMEM((LANES, cols), f32)],
        compiler_params=_vmem(vm),
        name="nsa_prompt",
    )(qa, kvn, kvw, ga, bias_c, bias_t, wbig, pe, mselt)


def _moba_prompt_body(qb_ref, kvb_ref, bt_ref, o_ref, kmean_s, q8_s, sel8_s, m_s, l_s, acc_s, *, seq):
    qt = pl.program_id(1)
    t0 = qt * TQ
    nb = seq // MOBA_BLK
    nr = _round_up(nb, SUBLANES)
    cols = MOBA_HEADS * TQ
    kw = MOBA_HEADS * HEAD_DIM
    scale = HEAD_DIM ** -0.5
    pairs = MOBA_HEADS // 2

    @pl.when(qt == 0)
    def _():
        kmean_s[...] = jnp.zeros(kmean_s.shape, f32)
        for n in range(nb):
            blk = kvb_ref[0, n * MOBA_BLK:(n + 1) * MOBA_BLK, 0:kw]
            kmean_s[n:n + 1, :] = jnp.sum(blk, axis=0, keepdims=True) * (1.0 / MOBA_BLK)
        sel8_s[...] = jnp.zeros(sel8_s.shape, f32)

    qtr = qb_ref[...].T
    zero = jnp.zeros((HEAD_DIM, TQ), f32)
    nblk = _iota((nr, TQ), 0)
    own = (t0 + _iota((nr, TQ), 1)) >> 8
    past = nblk < own
    for h in range(MOBA_HEADS):
        qh = qtr[h * HEAD_DIM:(h + 1) * HEAD_DIM, :]
        blk = jnp.concatenate([qh, zero] if h % 2 == 0 else [zero, qh], axis=0)
        q8_s[:, h * TQ:(h + 1) * TQ] = (blk * scale).astype(q8_s.dtype)
        gate = _dot_hi(kmean_s[:, (h // 2) * LANES:(h // 2 + 1) * LANES], blk)[0:nr, :]
        score = jnp.where(past, gate, -jnp.inf)
        sel = (past & (_rank_rows(score, nb) < MOBA_TOPK)) | (nblk == own)
        sel8_s[0:nr, h * TQ:(h + 1) * TQ] = sel.astype(f32)

    tpos = t0 + (_iota((1, cols), 1) & (TQ - 1))
    krow = _iota((TQ, cols), 0)

    def step(kt, carry):
        s_parts = []
        for pr in range(pairs):
            k = kvb_ref[0, pl.ds(kt * TQ, TQ), pr * LANES:(pr + 1) * LANES]
            s_parts.append(_dot(k, q8_s[:, 2 * pr * TQ:2 * (pr + 1) * TQ]))
        s = jnp.concatenate(s_parts, axis=1) + bt_ref[qt - kt]
        chosen = sel8_s[pl.ds((kt * TQ) >> 8, 1), :] > 0.5
        mask = chosen & (tpos >= kt * TQ + krow)
        s = jnp.where(mask, s, NEG)
        m_old = m_s[...]
        m_new = jnp.maximum(m_old, jnp.max(s, axis=0, keepdims=True))
        alpha = jnp.exp(m_old - m_new)
        p = jnp.where(mask, jnp.exp(s - m_new), 0.0)
        l_s[...] = alpha * l_s[...] + jnp.sum(p, axis=0, keepdims=True)
        pv = []
        for pr in range(pairs):
            v = kvb_ref[0, pl.ds(kt * TQ, TQ), kw + pr * LANES:kw + (pr + 1) * LANES]
            pv.append(_dot_tn(v, p[:, 2 * pr * TQ:2 * (pr + 1) * TQ]))
        acc_s[...] = alpha * acc_s[...] + jnp.concatenate(pv, axis=1)
        m_s[...] = m_new
        return carry

    _softmax_reset(m_s, l_s, acc_s)
    lax.fori_loop(0, qt + 1, step, 0)
    o = acc_s[...] / l_s[...]
    pieces = []
    for h in range(MOBA_HEADS):
        half = h % 2
        pieces.append(o[half * HEAD_DIM:(half + 1) * HEAD_DIM, h * TQ:(h + 1) * TQ])
    o_ref[...] = jnp.concatenate(pieces, axis=0).T


def _moba_prompt(qb, kvb, bias_t, batch, seq):
    nq = seq // TQ
    cols = MOBA_HEADS * TQ
    vm = 2 * (seq * kvb.shape[2] * 4 + bias_t.size * 4) + 8 * cols * LANES * 4 + (12 << 20)
    return pl.pallas_call(
        functools.partial(_moba_prompt_body, seq=seq),
        grid=(batch, nq),
        in_specs=[pl.BlockSpec((TQ, qb.shape[1]), lambda b, q: (b * nq + q, 0)),
                  pl.BlockSpec((1, seq, kvb.shape[2]), lambda b, q: (b, 0, 0)),
                  pl.BlockSpec(bias_t.shape, lambda b, q: (0, 0, 0))],
        out_specs=pl.BlockSpec((TQ, qb.shape[1]), lambda b, q: (b * nq + q, 0)),
        out_shape=jax.ShapeDtypeStruct(qb.shape, f32),
        scratch_shapes=[pltpu.VMEM((LANES, MOBA_HEADS * HEAD_DIM), f32),
                        pltpu.VMEM((LANES, cols), _MXU), pltpu.VMEM((LANES, cols), f32),
                        pltpu.VMEM((1, cols), f32), pltpu.VMEM((1, cols), f32), pltpu.VMEM((LANES, cols), f32)],
        compiler_params=_vmem(vm),
        name="moba_prompt",
    )(qb, kvb, bias_t)


def _page_group(n_pages):
    for g in (8, 4, 2):
        if n_pages % g == 0:
            return g
    raise ValueError("the page count must be even")


def _stack_heads_nsa(q, q8_s):
    half = _iota((1, LANES), 1) >> 6
    for h in range(NSA_HEADS):
        g = h // NSA_HPG
        blk = q[:, (h // 2) * LANES:(h // 2 + 1) * LANES]
        if h % 2 != g:
            blk = pltpu.roll(blk, HEAD_DIM, 1)
        q8_s[h:h + 1, :] = jnp.where(half == g, blk, 0.0).astype(q8_s.dtype)


def _softmax_step_vt(s, mask, vt, m_s, l_s, acc_s):
    s = jnp.where(mask, s, NEG)
    m_old = m_s[...]
    m_new = jnp.maximum(m_old, jnp.max(s, axis=1, keepdims=True))
    alpha = jnp.exp(m_old - m_new)
    p = jnp.where(mask, jnp.exp(s - m_new), 0.0)
    l_s[...] = alpha * l_s[...] + jnp.sum(p, axis=1, keepdims=True)
    acc_s[...] = alpha * acc_s[...] + _dot_nt(p, vt)
    m_s[...] = m_new


def _nsa_sample_body(pt_ref, *refs, n_pages, group):
    del pt_ref
    page_refs = refs[:group]
    (qa_ref, kvn_ref, kvw_ref, st_ref, ga_ref, bc_ref, bs_ref, bn_ref, bw_ref, wbig_ref, pe_ref, msel_ref, o_ref,
     rows_s, xflat_s, kc_s, vc_s, q8_s, sel8_s, ocmp_s, m_s, l_s, acc_s) = refs[group:]
    ph = pl.program_id(1)
    p = pl.program_id(2)
    n_steps = n_pages // group
    t = n_pages * PAGE
    seg_per_page = PAGE // CMP_STRIDE
    n_seg = n_pages * seg_per_page
    n_cmp = n_seg - CMP_BLK // CMP_STRIDE + 1
    cur = t // SEL_BLK
    n_sel = cur + 1
    nc = kc_s.shape[0]
    ls = sel8_s.shape[1]
    keys = group * PAGE
    gd = NSA_GROUPS * HEAD_DIM
    scale = HEAD_DIM ** -0.5

    def one_key(q8, row):
        return _dot_nt(q8, jnp.broadcast_to(row, (SUBLANES, LANES)))[:, 0:1]

    @pl.when(ph == 0)
    def _():
        if nc > n_seg:
            @pl.when(p == 0)
            def _():
                xflat_s[...] = jnp.zeros(xflat_s.shape, f32)
        for c in range(2):
            for g, ref in enumerate(page_refs):
                rows_s[...] = ref[0, c].reshape(gd, PAGE).T
                row0 = pl.multiple_of((p * group + g) * seg_per_page, SUBLANES)
                for r in range(CMP_STRIDE):
                    xflat_s[c, pl.ds(row0, seg_per_page), r * LANES:(r + 1) * LANES] = (
                        rows_s[pl.ds(r, seg_per_page, stride=CMP_STRIDE), :])

        @pl.when(p == n_steps - 1)
        def _():
            kc_s[...] = _compress(xflat_s, 0, pe_ref, wbig_ref)
            vc_s[...] = _compress(xflat_s, 1, pe_ref, wbig_ref)
            _stack_heads_nsa(qa_ref[0] * scale, q8_s)
            q8 = q8_s[...]
            ncol = _iota((NSA_HEADS, nc), 1)
            mask_c = ncol < n_cmp
            sc = jnp.where(mask_c, _dot_nt(q8, kc_s[...]) + bc_ref[...], NEG)
            e = jnp.exp(sc - jnp.max(sc, axis=1, keepdims=True))
            p_c = jnp.where(mask_c, e / jnp.sum(e, axis=1, keepdims=True), 0.0)
            ocmp_s[...] = _dot(p_c, vc_s[...])
            same_group = ((_iota((NSA_HEADS, NSA_HEADS), 0) >> 2) == (_iota((NSA_HEADS, NSA_HEADS), 1) >> 2)).astype(f32)
            blk = _dot_hi(_dot_hi(same_group, p_c), msel_ref[...])
            jb = _iota((NSA_HEADS, ls), 1)
            valid = jb < n_sel
            forced = valid & ((jb == 0) | (jb == cur) | (jb == cur - 1))
            score = jnp.where(forced, jnp.inf, jnp.where(valid, blk, -jnp.inf))
            sel8_s[...] = (valid & (_rank_lanes(score, n_sel) < N_SEL)).astype(sel8_s.dtype)

    @pl.when(ph == 1)
    def _():
        q8 = q8_s[...]

        @pl.when(p == 0)
        def _():
            m_s[...] = one_key(q8, kvn_ref[0, :, 2 * LANES:3 * LANES]) + bn_ref[:, 0:1]
            l_s[...] = jnp.ones(l_s.shape, f32)
            v_new = kvn_ref[0, :, 3 * LANES:4 * LANES].astype(_MXU).astype(f32)
            acc_s[...] = jnp.broadcast_to(v_new, acc_s.shape)

        kt = jnp.concatenate([r[0, 0].reshape(gd, PAGE) for r in page_refs], axis=1)
        vt = jnp.concatenate([r[0, 1].reshape(gd, PAGE) for r in page_refs], axis=1)
        s = _dot(q8, kt) + bs_ref[0]
        jrow = _iota((ls, keys), 0)
        jcol = (p * keys + _iota((ls, keys), 1)) >> 6
        chosen = _dot(sel8_s[...], (jrow == jcol).astype(f32)) > 0.5
        _softmax_step_vt(s, chosen, vt, m_s, l_s, acc_s)

        @pl.when(p == n_steps - 1)
        def _():
            osel = acc_s[...] / l_s[...]
            win = st_ref.shape[4]
            sw = _dot(q8, st_ref[0, 0].reshape(gd, win)) + bw_ref[...]
            sn = one_key(q8, kvw_ref[0, :, 0:LANES]) + bn_ref[:, 0:1]
            mw = jnp.maximum(jnp.max(sw, axis=1, keepdims=True), sn)
            ew = jnp.exp(sw - mw)
            en = jnp.exp(sn - mw)
            v_new = kvw_ref[0, :, LANES:2 * LANES].astype(_MXU).astype(f32)
            owin = ((_dot_nt(ew, st_ref[0, 1].reshape(gd, win)) + en * v_new)
                    / (jnp.sum(ew, axis=1, keepdims=True) + en))
            ga = ga_ref[0]
            for h in range(NSA_HEADS):
                g = h // NSA_HPG
                o = (ga[:, 3 * h:3 * h + 1] * ocmp_s[h:h + 1, :] + ga[:, 3 * h + 1:3 * h + 2] * osel[h:h + 1, :]
                     + ga[:, 3 * h + 2:3 * h + 3] * owin[h:h + 1, :])
                o_ref[0, :, h * HEAD_DIM:(h + 1) * HEAD_DIM] = o[:, g * HEAD_DIM:(g + 1) * HEAD_DIM]


def _nsa_sample(page_table, cache_t, qa, kvn, kvw, state_t, ga, tab, wbig, pe):
    bs, n_pages = page_table.shape
    group = _page_group(n_pages)
    n_steps = n_pages // group
    t = n_pages * PAGE
    win = state_t.shape[4]
    n_cmp = n_pages * (PAGE // CMP_STRIDE) - CMP_BLK // CMP_STRIDE + 1
    nc = max(LANES, n_pages * (PAGE // CMP_STRIDE))
    ls = _round_up(t // SEL_BLK + 1, LANES)
    msel = _cmp_to_sel_matrix(nc, ls)
    d_last = t - (CMP_STRIDE * (n_cmp - 1) + CMP_BLK - 1)
    bias_c = tab[:, d_last:t - CMP_BLK + 2:CMP_STRIDE][:, ::-1]
    bias_c = jnp.pad(bias_c, ((0, 0), (0, nc - n_cmp)))
    bias_s = jnp.transpose(tab[:, 1:t + 1][:, ::-1].reshape(NSA_HEADS, n_steps, group * PAGE), (1, 0, 2))
    bias_n = jnp.broadcast_to(tab[:, 0:1], (NSA_HEADS, LANES))
    bias_w = tab[:, 1:win + 1][:, ::-1]
    seg_w = CMP_STRIDE * NSA_GROUPS * HEAD_DIM
    req = lambda b, ph, p, pt: (b, 0, 0)
    c2 = lambda b, ph, p, pt: (0, 0)
    c4 = lambda b, ph, p, pt: (0, 0, 0, 0)

    def page_spec(g):
        return pl.BlockSpec((1, 2, NSA_GROUPS, HEAD_DIM, PAGE),
                            lambda b, ph, p, pt: (pt[b, p * group + g], ph, 0, 0, 0))

    grid_spec = pltpu.PrefetchScalarGridSpec(
        num_scalar_prefetch=1,
        grid=(bs, 2, n_steps),
        in_specs=([page_spec(g) for g in range(group)]
                  + [pl.BlockSpec((1, 1, qa.shape[2]), req), pl.BlockSpec((1, 1, kvn.shape[2]), req),
                     pl.BlockSpec((1, 1, kvw.shape[2]), req),
                     pl.BlockSpec((1,) + state_t.shape[1:], lambda b, ph, p, pt: (b, 0, 0, 0, 0)),
                     pl.BlockSpec((1, 1, LANES), req),
                     pl.BlockSpec(bias_c.shape, c2),
                     pl.BlockSpec((1, NSA_HEADS, group * PAGE), lambda b, ph, p, pt: (p * ph, 0, 0)),
                     pl.BlockSpec(bias_n.shape, c2), pl.BlockSpec(bias_w.shape, c2),
                     pl.BlockSpec(wbig.shape, c4), pl.BlockSpec(pe.shape, c4), pl.BlockSpec(msel.shape, c2)]),
        out_specs=pl.BlockSpec((1, 1, qa.shape[2]), req),
        scratch_shapes=[pltpu.VMEM((PAGE, LANES), f32),
                        pltpu.VMEM((2, nc, seg_w), f32), pltpu.VMEM((nc, LANES), f32), pltpu.VMEM((nc, LANES), f32),
                        pltpu.VMEM((NSA_HEADS, LANES), _MXU), pltpu.VMEM((NSA_HEADS, ls), _MXU),
                        pltpu.VMEM((NSA_HEADS, LANES), f32),
                        pltpu.VMEM((NSA_HEADS, 1), f32), pltpu.VMEM((NSA_HEADS, 1), f32),
                        pltpu.VMEM((NSA_HEADS, LANES), f32)])
    vm = (2 * nc * seg_w * 4 + 2 * (wbig.size * 2 + msel.size * 4 + win * 256 * 4 + 2 * group * PAGE * LANES * 4)
          + (16 << 20))
    return pl.pallas_call(
        functools.partial(_nsa_sample_body, n_pages=n_pages, group=group),
        grid_spec=grid_spec,
        out_shape=jax.ShapeDtypeStruct(qa.shape, f32),
        compiler_params=_vmem(vm),
        name="nsa_sample",
    )(page_table, *([cache_t] * group), qa, kvn, kvw, state_t, ga, bias_c, bias_s, bias_n, bias_w, wbig, pe, msel)


def _moba_sample_body(pt_ref, *refs, n_pages, group):
    del pt_ref
    page_refs = refs[:group]
    qb_ref, kvb_ref, bs_ref, bn_ref, o_ref, gate_s, mx_s, sum_s, acc_s = refs[group:]
    p = pl.program_id(1)
    n_steps = n_pages // group
    t = n_pages * PAGE
    kw = MOBA_HEADS * HEAD_DIM
    pages_per_blk = MOBA_BLK // PAGE
    blks_per_step = group // pages_per_blk
    own = t // MOBA_BLK
    scale = HEAD_DIM ** -0.5
    head_lane = (_iota((MOBA_HEADS, kw), 1) >> 6) == _iota((MOBA_HEADS, kw), 0)
    q8 = jnp.where(head_lane, jnp.broadcast_to(qb_ref[0], (MOBA_HEADS, kw)), 0.0)
    lane = _iota((MOBA_HEADS, LANES), 1)

    @pl.when(p == 0)
    def _():
        gate_s[...] = jnp.zeros(gate_s.shape, f32)
        mx_s[...] = jnp.full(mx_s.shape, NEG, f32)
        sum_s[...] = jnp.zeros(sum_s.shape, f32)

    for j in range(blks_per_step):
        blk = p * blks_per_step + j
        pages = page_refs[j * pages_per_blk:(j + 1) * pages_per_blk]
        kt = jnp.concatenate([r[0, 0].reshape(kw, PAGE) for r in pages], axis=1)
        vt = jnp.concatenate([r[0, 1].reshape(kw, PAGE) for r in pages], axis=1)
        raw = _dot_hi(q8, kt)
        s = raw * scale + bs_ref[0, :, j * MOBA_BLK:(j + 1) * MOBA_BLK]
        m = jnp.max(s, axis=1, keepdims=True)
        pr = jnp.exp(s - m)
        here = lane == blk
        gate_s[...] = jnp.where(here, jnp.sum(raw, axis=1, keepdims=True) * (1.0 / MOBA_BLK), gate_s[...])
        mx_s[...] = jnp.where(here, m, mx_s[...])
        sum_s[...] = jnp.where(here, jnp.sum(pr, axis=1, keepdims=True), sum_s[...])
        acc_s[blk] = _dot_nt(pr, vt)

    @pl.when(p == n_steps - 1)
    def _():
        past = lane < own
        score = jnp.where(past, gate_s[...], -jnp.inf)
        sel = past & (_rank_lanes(score, own) < MOBA_TOPK)
        s_new = jnp.sum(q8 * kvb_ref[0, :, 0:kw], axis=1, keepdims=True) * scale + bn_ref[:, 0:1]
        v_new = kvb_ref[0, :, kw:2 * kw].astype(_MXU).astype(f32)
        top = jnp.maximum(jnp.max(jnp.where(sel, mx_s[...], NEG), axis=1, keepdims=True), s_new)
        wts = jnp.where(sel, jnp.exp(mx_s[...] - top), 0.0)
        w_new = jnp.exp(s_new - top)
        den = jnp.sum(wts * sum_s[...], axis=1, keepdims=True) + w_new
        num = w_new * v_new
        for j in range(own):
            num = num + wts[:, j:j + 1] * acc_s[j]
        o8 = jnp.where(head_lane, num / den, 0.0)
        o_ref[0] = jnp.sum(o8, axis=0, keepdims=True)


def _moba_sample(page_table, cache_t, qb, kvb, tab):
    bs, n_pages = page_table.shape
    group = _page_group(n_pages)
    n_steps = n_pages // group
    t = n_pages * PAGE
    kw = MOBA_HEADS * HEAD_DIM
    n_blk = t // MOBA_BLK
    assert n_blk <= LANES
    bias_s = jnp.transpose(tab[:, 1:t + 1][:, ::-1].reshape(MOBA_HEADS, n_steps, group * PAGE), (1, 0, 2))
    bias_n = jnp.broadcast_to(tab[:, 0:1], (MOBA_HEADS, LANES))
    req = lambda b, p, pt: (b, 0, 0)

    def page_spec(g):
        return pl.BlockSpec((1, 2, MOBA_HEADS, HEAD_DIM, PAGE), lambda b, p, pt: (pt[b, p * group + g], 0, 0, 0, 0))

    grid_spec = pltpu.PrefetchScalarGridSpec(
        num_scalar_prefetch=1,
        grid=(bs, n_steps),
        in_specs=([page_spec(g) for g in range(group)]
                  + [pl.BlockSpec((1, 1, kw), req), pl.BlockSpec((1, 1, 2 * kw), req),
                     pl.BlockSpec((1, MOBA_HEADS, group * PAGE), lambda b, p, pt: (p, 0, 0)),
                     pl.BlockSpec(bias_n.shape, lambda b, p, pt: (0, 0))]),
        out_specs=pl.BlockSpec((1, 1, kw), req),
        scratch_shapes=[pltpu.VMEM((MOBA_HEADS, LANES), f32), pltpu.VMEM((MOBA_HEADS, LANES), f32),
                        pltpu.VMEM((MOBA_HEADS, LANES), f32), pltpu.VMEM((n_blk, MOBA_HEADS, kw), f32)])
    return pl.pallas_call(
        functools.partial(_moba_sample_body, n_pages=n_pages, group=group),
        grid_spec=grid_spec,
        out_shape=jax.ShapeDtypeStruct(qb.shape, f32),
        compiler_params=_vmem(2 * group * PAGE * 2 * kw * 4 + (24 << 20)),
        name="moba_sample",
    )(page_table, *([cache_t] * group), qb, kvb, bias_s, bias_n)


def _layer_norm(x, g, b):
    mu = jnp.mean(x, axis=-1, keepdims=True)
    xc = x - mu
    var = jnp.mean(xc * xc, axis=-1, keepdims=True)
    return xc * lax.rsqrt(var + LN_EPS) * g + b


def _merge_body(oa_ref, ob_ref, gm_ref, x_ref, wa_ref, wb_ref, wo_ref, g_ref, b_ref, h_ref, ht_ref, *, alpha):
    d = x_ref.shape[1]
    ya = _dot(oa_ref[...], wa_ref[...])
    yb = _dot(ob_ref[...], wb_ref[...])
    gm = gm_ref[...]
    mix = _dot(gm[:, :d] * ya + gm[:, d:] * yb, wo_ref[...])
    h = _layer_norm(alpha * x_ref[...] + mix, g_ref[...], b_ref[...])
    h_ref[...] = h
    ht_ref[...] = h.T.astype(ht_ref.dtype)


def _merge(oa, ob, gm, x2d, wa, wb, wo, g, b, alpha):
    n, d = x2d.shape
    tm = 256 if n % 256 == 0 else n
    row = lambda i: (i, 0)
    full = lambda i: (0, 0)
    vm = 2 * (tm * (oa.shape[1] + ob.shape[1] + gm.shape[1] + 2 * d) * 4 + tm * d * 2
              + (wa.size + wb.size + wo.size) * 2) + (16 << 20)
    return pl.pallas_call(
        functools.partial(_merge_body, alpha=alpha),
        grid=(n // tm,),
        in_specs=[pl.BlockSpec((tm, oa.shape[1]), row), pl.BlockSpec((tm, ob.shape[1]), row),
                  pl.BlockSpec((tm, gm.shape[1]), row), pl.BlockSpec((tm, d), row),
                  pl.BlockSpec(wa.shape, full), pl.BlockSpec(wb.shape, full), pl.BlockSpec(wo.shape, full),
                  pl.BlockSpec((1, d), full), pl.BlockSpec((1, d), full)],
        out_specs=[pl.BlockSpec((tm, d), row), pl.BlockSpec((d, tm), lambda i: (0, i))],
        out_shape=[jax.ShapeDtypeStruct((n, d), f32), jax.ShapeDtypeStruct((d, n), _MXU)],
        compiler_params=_vmem(vm),
        name="merge",
    )(oa, ob, gm, x2d, wa, wb, wo, g, b)


_NOT_TOP = 99.0
_CAND_B = tuple(PEER_TOPK // (a + 1) for a in range(SUBLANES))


def _extract_top(work_s, n_rounds, on_round, one_per_round):
    rows = _iota(work_s.shape, 0)

    def body(r, carry):
        w = work_s[...]
        m = jnp.max(w, axis=0, keepdims=True)
        hit = w == m
        if one_per_round:
            hit = rows == jnp.min(jnp.where(hit, rows, work_s.shape[0]), axis=0, keepdims=True)
        on_round(r, m, hit)
        work_s[...] = jnp.where(hit, -jnp.inf, w)
        return carry

    lax.fori_loop(0, n_rounds, body, 0)


def _top_rounds(work_s, init, on_round, n_hit):
    init()
    _extract_top(work_s, PEER_TOPK, on_round, one_per_round=False)
    bad = jnp.max(jnp.abs(n_hit() - float(PEER_TOPK)))

    @pl.when(bad > 0.5)
    def _():
        init()
        _extract_top(work_s, PEER_TOPK, on_round, one_per_round=True)


def _peer_select(ht_ref, wqt_ref, keys_ref, qt_s, rank_s, exp_s, sv_s, work_s, cwork_s, cand_s, cnt_s, f0_s,
                 rk1_s, e1_s):
    tn = ht_ref.shape[1]
    nk = PEER_N_KEYS
    qt_s[...] = jnp.dot(wqt_ref[...], ht_ref[...], preferred_element_type=f32)

    def per_half(hc, carry):
        def scores():
            return _dot(keys_ref[hc], qt_s[pl.ds(pl.multiple_of(hc * PEER_HALF, PEER_HALF), PEER_HALF), :])

        def init():
            work_s[...] = scores()
            rank_s[hc] = jnp.full((nk, tn), _NOT_TOP, f32)

        def on_round(r, m, hit):
            sv_s[hc, pl.ds(r, 1), :] = m
            rank_s[hc] = jnp.where(hit, jnp.asarray(r, f32), rank_s[hc])

        def n_hit():
            return jnp.sum(jnp.where(rank_s[hc] < _NOT_TOP, 1.0, 0.0), axis=0, keepdims=True)

        _top_rounds(work_s, init, on_round, n_hit)
        exp_s[hc] = jnp.exp(scores() - sv_s[hc, 0:1, :])
        return carry

    lax.fori_loop(0, 2 * PEER_HEADS, per_half, 0)

    n_cand = cand_s.shape[0]

    def per_head(h, carry):
        sv0 = sv_s[2 * h]
        sv1 = sv_s[2 * h + 1]
        e0 = jnp.exp(sv0 - sv0[0:1, :])
        e1 = jnp.exp(sv1 - sv1[0:1, :])
        brow = _iota((SUBLANES, tn), 0)
        tiles, etiles = [sv0[0:1, :] + sv1, ], [e0[0:1, :] * e1]
        for a in range(1, SUBLANES):
            ok = brow < _CAND_B[a]
            tiles.append(jnp.where(ok, sv0[a:a + 1, :] + sv1[0:SUBLANES, :], -jnp.inf))
            etiles.append(jnp.where(ok, e0[a:a + 1, :] * e1[0:SUBLANES, :], 0.0))
        tiles.append(sv0[SUBLANES:, :] + sv1[0:1, :])
        etiles.append(e0[SUBLANES:, :] * e1[0:1, :])
        cands = jnp.concatenate(tiles, axis=0)

        def init():
            cwork_s[...] = cands
            cand_s[...] = jnp.zeros(cand_s.shape, f32)

        def on_round(r, m, hit):
            cand_s[...] = jnp.where(hit, 1.0, cand_s[...])

        _top_rounds(cwork_s, init, on_round, lambda: jnp.sum(cand_s[...], axis=0, keepdims=True))
        chosen = cand_s[...]
        z = jnp.sum(chosen * jnp.concatenate(etiles, axis=0), axis=0, keepdims=True)
        cnts = [jnp.sum(chosen[0:PEER_TOPK], axis=0, keepdims=True)]
        for a in range(1, SUBLANES):
            r0 = PEER_TOPK + (a - 1) * SUBLANES
            cnts.append(jnp.sum(chosen[r0:r0 + SUBLANES], axis=0, keepdims=True))
        tail = chosen[n_cand - SUBLANES:]
        cnts += [tail[a:a + 1, :] for a in range(SUBLANES)]
        ra = rank_s[2 * h]
        c0 = jnp.zeros((nk, tn), f32)
        for a in range(PEER_TOPK):
            c0 = jnp.where(ra == float(a), cnts[a], c0)
        cnt_s[h] = c0
        f0_s[h] = exp_s[2 * h] / z
        rk1_s[h] = rank_s[2 * h + 1].astype(rk1_s.dtype)
        e1_s[h] = exp_s[2 * h + 1].astype(e1_s.dtype)
        return carry

    lax.fori_loop(0, PEER_HEADS, per_head, 0)


def _gelu_t(u_tile, ht):
    at = jnp.dot(u_tile, ht, preferred_element_type=f32)
    return (0.5 * at * (1.0 + lax.erf(at * math.sqrt(0.5)))).astype(_MXU)


def _peer_body(ht_ref, h_ref, wqt_ref, keys_ref, u0_ref, un_ref, vt_ref, g_ref, b_ref, y_ref,
               qt_s, rank_s, exp_s, sv_s, work_s, cwork_s, cand_s, cnt_s, f0_s, rk1_s, e1_s, gel_s, wc_s, acc_s,
               *, alpha, ti):
    e = pl.program_id(1)
    last = pl.num_programs(1) - 1
    tn = ht_ref.shape[1]

    @pl.when(e == 0)
    def _():
        _peer_select(ht_ref, wqt_ref, keys_ref, qt_s, rank_s, exp_s, sv_s, work_s, cwork_s, cand_s, cnt_s, f0_s,
                     rk1_s, e1_s)
        acc_s[...] = jnp.zeros(acc_s.shape, f32)
        gel_s[0] = _gelu_t(u0_ref[...], ht_ref[...])

    for ii in range(ti):
        i = e * ti + ii
        r0, r1 = ii * PEER_N_KEYS, (ii + 1) * PEER_N_KEYS
        w = jnp.zeros((PEER_N_KEYS, tn), _MXU)
        for h in range(PEER_HEADS):
            cnt = cnt_s[h, pl.ds(i, 1), :].astype(_MXU)
            f0 = f0_s[h, pl.ds(i, 1), :].astype(_MXU)
            w = w + jnp.where(rk1_s[h] < cnt, e1_s[h], jnp.zeros((), _MXU)) * f0
        wc_s[r0:r1, :] = w * gel_s[e % 2, r0:r1, :]
        if ii % 2 == 1:
            k0, k1 = (ii - 1) * PEER_N_KEYS, (ii + 1) * PEER_N_KEYS
            acc_s[...] += jnp.dot(vt_ref[:, k0:k1], wc_s[k0:k1, :], preferred_element_type=f32)
    gel_s[(e + 1) % 2] = _gelu_t(un_ref[...], ht_ref[...])

    @pl.when(e == last)
    def _():
        y_ref[...] = _layer_norm(alpha * h_ref[...] + acc_s[...].T, g_ref[...], b_ref[...])


def _peer(ht, h, wqt, keys, u, vt, g, b, alpha):
    d, n = ht.shape
    n_exp = u.shape[0]
    tn = 512 if n % 512 == 0 else n
    ti = 4
    te = ti * PEER_N_KEYS
    n_half = 2 * PEER_HEADS
    n_cand = PEER_TOPK + (SUBLANES - 1) * SUBLANES + SUBLANES
    tok = lambda t, e: (0, t)
    full2 = lambda t, e: (0, 0)
    scratch = [pltpu.VMEM((wqt.shape[0], tn), f32),
               pltpu.VMEM((n_half, PEER_N_KEYS, tn), f32), pltpu.VMEM((n_half, PEER_N_KEYS, tn), f32),
               pltpu.VMEM((n_half, PEER_TOPK, tn), f32), pltpu.VMEM((PEER_N_KEYS, tn), f32),
               pltpu.VMEM((n_cand, tn), f32), pltpu.VMEM((n_cand, tn), f32),
               pltpu.VMEM((PEER_HEADS, PEER_N_KEYS, tn), f32), pltpu.VMEM((PEER_HEADS, PEER_N_KEYS, tn), f32),
               pltpu.VMEM((PEER_HEADS, PEER_N_KEYS, tn), _MXU), pltpu.VMEM((PEER_HEADS, PEER_N_KEYS, tn), _MXU),
               pltpu.VMEM((2, te, tn), _MXU), pltpu.VMEM((te, tn), _MXU), pltpu.VMEM((d, tn), f32)]
    vm = (wqt.shape[0] * tn * 4 + 2 * n_half * PEER_N_KEYS * tn * 4 + 2 * PEER_HEADS * PEER_N_KEYS * tn * 6
          + 3 * te * tn * 2 + d * tn * 4
          + 2 * (d * tn * 2 + 2 * tn * d * 4 + wqt.size * 2 + keys.size * 2 + 3 * te * d * 2) + (14 << 20))
    n_tiles = n_exp // te
    return pl.pallas_call(
        functools.partial(_peer_body, alpha=alpha, ti=ti),
        grid=(n // tn, n_tiles),
        in_specs=[pl.BlockSpec((d, tn), tok),
                  pl.BlockSpec((tn, d), lambda t, e: (t, 0)),
                  pl.BlockSpec(wqt.shape, full2),
                  pl.BlockSpec(keys.shape, lambda t, e: (0, 0, 0)),
                  pl.BlockSpec((te, d), full2),
                  pl.BlockSpec((te, d), lambda t, e: (jnp.minimum(e + 1, n_tiles - 1), 0)),
                  pl.BlockSpec((d, te), lambda t, e: (0, e)),
                  pl.BlockSpec((1, d), full2), pl.BlockSpec((1, d), full2)],
        out_specs=pl.BlockSpec((tn, d), lambda t, e: (t, 0)),
        out_shape=jax.ShapeDtypeStruct((n, d), f32),
        scratch_shapes=scratch,
        compiler_params=_vmem(vm),
        name="peer",
    )(ht, h, wqt, keys, u, u, vt, g, b)


def kernel(x_prompt, x_sample, cache_nsa, cache_moba, state_nsa_win, page_table, w_in, nsa_cmp_pe, nsa_cmp_w, w_branch_a, w_branch_b, w_out, ln1_g, ln1_b, peer_wq, peer_keys, peer_u, peer_v, ln2_g, ln2_b, rel_bias):
    depth = w_in.shape[0]
    batch, seq, d = x_prompt.shape
    bs, dec_seq, _ = x_sample.shape
    n_pages = page_table.shape[1]
    past = n_pages * PAGE
    win_len = state_nsa_win.shape[2]
    assert dec_seq == 1 and cache_nsa.shape[2] == PAGE and cache_moba.shape[2] == PAGE
    assert seq % MOBA_BLK == 0 and past % MOBA_BLK == 0 and win_len == WINDOW and seq % TQ == 0
    alpha = (2 * depth) ** 0.25
    nq = seq // TQ
    n_tok = batch * seq

    tab = _bias_by_distance(rel_bias, max(seq, past + 1) + TQ)
    tab_a, tab_b = tab[:NSA_HEADS], tab[NSA_HEADS:]
    tiles_a, tiles_b = _bias_tiles_t(tab_a, nq), _bias_tiles_t(tab_b, nq)
    cmp_bias = _nsa_cmp_bias_t(tab_a, seq, max(LANES, seq // CMP_STRIDE))

    xp = x_prompt.reshape(n_tok, d)
    xs = x_sample.reshape(bs, d)
    outs = [[] for _ in range(6)]
    for l in range(depth):
        w_packed = _pack_w_in(w_in[l], d)
        wbig, pe = _pack_cmp(nsa_cmp_pe[l], nsa_cmp_w[l])
        wa, wb, wo = w_branch_a[l].astype(_MXU), w_branch_b[l].astype(_MXU), w_out[l].astype(_MXU)
        wqt = peer_wq[l].T.astype(_MXU)
        keys = peer_keys[l].reshape(2 * PEER_HEADS, PEER_N_KEYS, PEER_HALF).astype(_MXU)
        u = peer_u[l].astype(_MXU)
        vt = peer_v[l].T.astype(_MXU)
        g1, b1, g2, b2 = ln1_g[l][None], ln1_b[l][None], ln2_g[l][None], ln2_b[l][None]

        def ffn(oa, ob, gm, x2d):
            h, ht = _merge(oa, ob, gm, x2d, wa, wb, wo, g1, b1, alpha)
            return _peer(ht, h, wqt, keys, u, vt, g2, b2, alpha)

        qa, kvn, kvw, qb, kvb, gm, ga, kvn_t, kvw_t, kvb_t = _project(xp, w_packed, d, seq)
        oa = _nsa_prompt(qa, kvn.reshape(batch, seq, -1), kvw.reshape(batch, seq, -1), ga, cmp_bias, tiles_a,
                         wbig, pe, batch, seq)
        ob = _moba_prompt(qb, kvb.reshape(batch, seq, -1), tiles_b, batch, seq)
        xp = ffn(oa, ob, gm, xp)
        to_row_major = lambda a, c, g: jnp.transpose(a.reshape(batch, c, g, HEAD_DIM, -1), (0, 4, 1, 2, 3))
        outs[0].append(to_row_major(kvn_t, 4, NSA_GROUPS))
        outs[1].append(to_row_major(kvb_t, 2, MOBA_HEADS))
        outs[2].append(to_row_major(kvw_t[:, :, seq - min(WINDOW, seq):], 2, NSA_GROUPS))

        qa, kvn, kvw, qb, kvb, gm, ga = _project(xs, w_packed, d)
        r3 = lambda a: a.reshape(bs, 1, -1)
        to_feature_major = lambda a: jnp.transpose(a, (0, 2, 3, 4, 1))
        oa = _nsa_sample(page_table, to_feature_major(cache_nsa[l]), r3(qa), r3(kvn), r3(kvw),
                         to_feature_major(state_nsa_win[l]), r3(ga), tab_a, wbig, pe)
        ob = _moba_sample(page_table, to_feature_major(cache_moba[l]), r3(qb), r3(kvb), tab_b)
        xs = ffn(oa.reshape(bs, -1), ob.reshape(bs, -1), gm, xs)
        outs[3].append(kvn.reshape(bs, 1, 4, NSA_GROUPS, HEAD_DIM))
        outs[4].append(kvb.reshape(bs, 1, 2, MOBA_HEADS, HEAD_DIM))
        win_new = jnp.concatenate([state_nsa_win[l][:, 1:], kvw.reshape(bs, 1, 2, NSA_GROUPS, HEAD_DIM)], axis=1)
        outs[5].append(win_new)

    stacked = [jnp.stack(o) for o in outs]
    return (xp.reshape(batch, seq, d), xs.reshape(bs, 1, d), *stacked)
```

```python
import functools
import math

import jax
import jax.numpy as jnp
from jax import lax
from jax.experimental import pallas as pl
from jax.experimental.pallas import tpu as pltpu

HEAD_DIM = 64
NSA_HEADS = 8
NSA_GROUPS = 2
NSA_HPG = NSA_HEADS // NSA_GROUPS
CMP_BLK = 32
CMP_STRIDE = 16
SEL_BLK = 64
N_SEL = 16
WINDOW = 512
MOBA_HEADS = 8
MOBA_BLK = 256
MOBA_TOPK = 3
N_BUCKETS = 32
MAX_DISTANCE = 1024
PEER_HEADS = 8
PEER_N_KEYS = 128
PEER_HALF = 128
PEER_TOPK = 16
LN_EPS = 1e-5
NEG = -1e30

LANES = 128
SUBLANES = 8
VMEM_BYTES_V7X = 64 * 1024 * 1024

TQ = 128
PAGE = 128

f32 = jnp.float32
i32 = jnp.int32
_MXU = jnp.bfloat16
_HI = lax.Precision.HIGHEST


def _iota(shape, dim):
    return lax.broadcasted_iota(i32, shape, dim)


def _dot(a, b):
    return jnp.dot(a.astype(_MXU), b.astype(_MXU), preferred_element_type=f32)


def _dot_nt(a, b):
    return lax.dot_general(a.astype(_MXU), b.astype(_MXU), (((1,), (1,)), ((), ())),
                           preferred_element_type=f32)


def _dot_tn(a, b):
    return lax.dot_general(a.astype(_MXU), b.astype(_MXU), (((0,), (0,)), ((), ())), preferred_element_type=f32)


def _dot_hi(a, b):
    return jnp.dot(a, b, precision=_HI, preferred_element_type=f32)


def _dot_nt_hi(a, b):
    return lax.dot_general(a, b, (((1,), (1,)), ((), ())), precision=_HI, preferred_element_type=f32)


def _vmem(nbytes, flags=None):
    return pltpu.CompilerParams(vmem_limit_bytes=int(min(nbytes, VMEM_BYTES_V7X - (4 << 20))), flags=flags)


def _round_up(x, m):
    return -(-x // m) * m


def _rank_lanes(s, n):
    width = s.shape[1]
    j = _iota(s.shape, 1)
    rank = jnp.zeros(s.shape, i32)
    for r in range(1, n):
        lo = pltpu.roll(s, r, 1)
        rank = rank + jnp.where((j >= r) & (lo >= s), 1, 0)
        hi = pltpu.roll(s, width - r, 1)
        rank = rank + jnp.where((j + r < n) & (hi > s), 1, 0)
    return rank


def _rank_rows(s, n):
    j = _iota(s.shape, 0)
    rank = jnp.zeros(s.shape, i32)
    for k in range(n):
        row = s[k:k + 1, :]
        rank = rank + jnp.where((row > s) | ((row == s) & (j > k)), 1, 0)
    return rank


def _softmax_reset(m_s, l_s, acc_s):
    m_s[...] = jnp.full(m_s.shape, NEG, f32)
    l_s[...] = jnp.zeros(l_s.shape, f32)
    acc_s[...] = jnp.zeros(acc_s.shape, f32)


def _softmax_step(s, mask, v, m_s, l_s, acc_s):
    s = jnp.where(mask, s, NEG)
    m_old = m_s[...]
    m_new = jnp.maximum(m_old, jnp.max(s, axis=1, keepdims=True))
    alpha = jnp.exp(m_old - m_new)
    p = jnp.where(mask, jnp.exp(s - m_new), 0.0)
    l_s[...] = alpha * l_s[...] + jnp.sum(p, axis=1, keepdims=True)
    acc_s[...] = alpha * acc_s[...] + _dot(p, v)
    m_s[...] = m_new


def _softmax_step_t(s, mask, v, m_s, l_s, acc_s):
    if mask is not None:
        s = jnp.where(mask, s, NEG)
    m_old = m_s[...]
    m_new = jnp.maximum(m_old, jnp.max(s, axis=0, keepdims=True))
    alpha = jnp.exp(m_old - m_new)
    p = jnp.exp(s - m_new)
    l_s[...] = alpha * l_s[...] + jnp.sum(p, axis=0, keepdims=True)
    acc_s[...] = alpha * acc_s[...] + _dot_tn(v, p)
    m_s[...] = m_new


_W_QA = NSA_HEADS * HEAD_DIM
_W_KVN = 4 * NSA_GROUPS * HEAD_DIM
_W_KVW = 2 * NSA_GROUPS * HEAD_DIM
_W_GA = 3 * NSA_HEADS
_W_QB = MOBA_HEADS * HEAD_DIM
_W_KVB = 2 * MOBA_HEADS * HEAD_DIM


def _proj_body(x_ref, w_ref, qa_ref, kvn_ref, kvw_ref, qb_ref, kvb_ref, gm_ref, ga_ref, *kv_t_refs):
    xb = x_ref[...].astype(_MXU)
    col = 0
    kvn_t, kvw_t, kvb_t = kv_t_refs if kv_t_refs else (None, None, None)
    for ref, act, t_ref in ((qa_ref, None, None), (kvn_ref, None, kvn_t), (kvw_ref, None, kvw_t),
                            (qb_ref, None, None), (kvb_ref, None, kvb_t),
                            (gm_ref, jax.nn.sigmoid, None), (ga_ref, jax.nn.sigmoid, None)):
        width = ref.shape[1]
        y = jnp.dot(xb, w_ref[:, col:col + width], preferred_element_type=f32)
        ref[...] = y if act is None else act(y)
        if t_ref is not None:
            t_ref[0] = y.T
        col += width


def _project(x2d, w_packed, d_model, seq=None):
    n = x2d.shape[0]
    tm = 256 if n % 256 == 0 else n
    widths = (_W_QA, _W_KVN, _W_KVW, _W_QB, _W_KVB, 2 * d_model, LANES)
    n_cols = sum(widths)
    out_specs = [pl.BlockSpec((tm, w), lambda i: (i, 0)) for w in widths]
    out_shape = [jax.ShapeDtypeStruct((n, w), f32) for w in widths]
    if seq is not None:
        tiles = seq // tm
        for w in (_W_KVN, _W_KVW, _W_KVB):
            out_specs.append(pl.BlockSpec((1, w, tm), lambda i: (i // tiles, 0, i % tiles)))
            out_shape.append(jax.ShapeDtypeStruct((n // seq, w, seq), f32))
    return pl.pallas_call(
        _proj_body,
        grid=(n // tm,),
        in_specs=[pl.BlockSpec((tm, d_model), lambda i: (i, 0)),
                  pl.BlockSpec((d_model, n_cols), lambda i: (0, 0))],
        out_specs=out_specs,
        out_shape=out_shape,
        compiler_params=_vmem(2 * (tm * d_model * 4 + d_model * n_cols * 2 + 2 * tm * n_cols * 4) + (8 << 20)),
        name="proj",
    )(x2d, w_packed)


def _pack_w_in(w_in, d_model):
    c_ga = _W_QA + _W_KVN + _W_KVW
    c_qb = c_ga + _W_GA
    pad = jnp.zeros((d_model, LANES - _W_GA), w_in.dtype)
    return jnp.concatenate([w_in[:, :c_ga], w_in[:, c_qb:], w_in[:, c_ga:c_qb], pad], axis=1).astype(_MXU)


def _t5_bucket(dist):
    n = jnp.maximum(dist, 0)
    max_exact = N_BUCKETS // 2
    nf = jnp.maximum(n, 1).astype(f32)
    large = max_exact + (jnp.log(nf / max_exact) / math.log(MAX_DISTANCE / max_exact)
                         * (N_BUCKETS - max_exact)).astype(i32)
    return jnp.where(n < max_exact, n, jnp.minimum(large, N_BUCKETS - 1)).astype(i32)


def _bias_by_distance(rel_bias, max_dist):
    return rel_bias[_t5_bucket(jnp.arange(max_dist, dtype=i32))].T


def _windows(v, step, n, length):
    fine = SUBLANES
    coarse = jnp.stack([v[:, step * fine * a:step * fine * a + length + step * (fine - 1)] for a in range(n // fine)],
                       axis=1)
    out = jnp.stack([coarse[:, :, step * b:step * b + length] for b in range(fine)], axis=2)
    return out.reshape(v.shape[0], n, length)


def _bias_tiles_t(tab, n_tiles):
    h = tab.shape[0]
    tabp = jnp.concatenate([jnp.zeros((h, TQ - 1), tab.dtype), tab], axis=1)
    g = _windows(tabp, 1, TQ, n_tiles * TQ).reshape(h, TQ, n_tiles, TQ)[..., ::-1]
    return jnp.transpose(g, (2, 3, 0, 1)).reshape(n_tiles, TQ, h * TQ)


def _nsa_cmp_bias_t(tab, seq, nc):
    h = tab.shape[0]
    off = CMP_STRIDE * (nc - 1) + CMP_BLK - 1
    tabp = jnp.concatenate([jnp.zeros((h, off), tab.dtype), tab[:, :seq]], axis=1)
    w = _windows(tabp, CMP_STRIDE, nc, seq)[:, ::-1]
    return jnp.transpose(w.reshape(h, nc, seq // TQ, TQ), (2, 1, 0, 3)).reshape(seq // TQ, nc, h * TQ)


def _pack_cmp(cmp_pe, cmp_w):
    r = CMP_BLK // CMP_STRIDE
    w = cmp_w.reshape(2, r, CMP_STRIDE, HEAD_DIM, HEAD_DIM)
    eye = jnp.eye(NSA_GROUPS, dtype=w.dtype)
    wbig = jnp.einsum("cmrde,gh->cmrgdhe", w, eye).reshape(2, r, CMP_STRIDE * NSA_GROUPS * HEAD_DIM,
                                                          NSA_GROUPS * HEAD_DIM)
    pe = cmp_pe.reshape(2, r, CMP_STRIDE, 1, HEAD_DIM)
    pe = jnp.broadcast_to(pe, (2, r, CMP_STRIDE, NSA_GROUPS, HEAD_DIM)).reshape(2, r, 1, -1)
    return wbig.astype(_MXU), pe.astype(f32)


def _cmp_to_sel_matrix(n_rows, n_cols):
    spb = SEL_BLK // CMP_STRIDE
    n = jnp.arange(n_rows, dtype=i32)[:, None]
    j = jnp.arange(n_cols, dtype=i32)[None, :]
    m = jnp.zeros((n_rows, n_cols), f32)
    for k in range(CMP_BLK // CMP_STRIDE):
        m = m + ((n + k) // spb == j).astype(f32)
    return m


def _compress(xflat_ref, c, pe_ref, wbig_ref):
    x = xflat_ref[c]
    a = _dot(x + pe_ref[c, 0], wbig_ref[c, 0])
    b = _dot(x + pe_ref[c, 1], wbig_ref[c, 1])
    return a + pltpu.roll(b, x.shape[0] - 1, 0)


def _nsa_prompt_body(qa_ref, kvn_ref, kvw_ref, ga_ref, bc_ref, bt_ref, wbig_ref, pe_ref, mselt_ref, o_ref,
                     rows_s, xflat_s, kc_s, vc_s, q8_s, sel8_s, ocmp_s, osel_s, m_s, l_s, acc_s, *, seq):
    qt = pl.program_id(1)
    t0 = qt * TQ
    n_seg = seq // CMP_STRIDE
    n_cmp = n_seg - CMP_BLK // CMP_STRIDE + 1
    n_sel = -(-seq // SEL_BLK)
    nr = _round_up(n_sel, SUBLANES)
    nc = kc_s.shape[0]
    cols = NSA_HEADS * TQ
    scale = HEAD_DIM ** -0.5

    @pl.when(qt == 0)
    def _():
        for c in range(2):
            if nc > n_seg:
                xflat_s[c] = jnp.zeros(xflat_s.shape[1:], f32)
            rows_s[...] = kvn_ref[0, :, c * LANES:(c + 1) * LANES]
            for r in range(CMP_STRIDE):
                xflat_s[c, 0:n_seg, r * LANES:(r + 1) * LANES] = rows_s[pl.ds(r, n_seg, stride=CMP_STRIDE), :]
        kc_s[...] = _compress(xflat_s, 0, pe_ref, wbig_ref)
        vc_s[...] = _compress(xflat_s, 1, pe_ref, wbig_ref)
        sel8_s[...] = jnp.zeros(sel8_s.shape, f32)

    qtr = (qa_ref[...] * scale).T
    zero = jnp.zeros((HEAD_DIM, TQ), f32)
    for h in range(NSA_HEADS):
        qh = qtr[h * HEAD_DIM:(h + 1) * HEAD_DIM, :]
        blk = jnp.concatenate([qh, zero] if h // NSA_HPG == 0 else [zero, qh], axis=0)
        q8_s[:, h * TQ:(h + 1) * TQ] = blk.astype(q8_s.dtype)
    q8 = q8_s[...]
    tpos = t0 + (_iota((1, cols), 1) & (TQ - 1))

    nrow = _iota((nc, cols), 0)
    mask_c = (nrow < n_cmp) & (tpos >= nrow * CMP_STRIDE + (CMP_BLK - 1))
    sc = jnp.where(mask_c, _dot(kc_s[...], q8) + bc_ref[0], NEG)
    e = jnp.exp(sc - jnp.max(sc, axis=0, keepdims=True))
    p_c = jnp.where(mask_c, e / jnp.sum(e, axis=0, keepdims=True), 0.0)
    ocmp_s[...] = _dot_tn(vc_s[...], p_c)

    jb = _iota((nr, TQ), 0)
    cur = (t0 + _iota((nr, TQ), 1)) >> 6
    valid = (jb <= cur) & (jb < n_sel)
    forced = valid & ((jb == 0) | (jb == cur) | (jb == cur - 1))
    for g in range(NSA_GROUPS):
        imp = p_c[:, g * NSA_HPG * TQ:(g * NSA_HPG + 1) * TQ]
        for h in range(1, NSA_HPG):
            imp = imp + p_c[:, (g * NSA_HPG + h) * TQ:(g * NSA_HPG + h + 1) * TQ]
        blk = _dot_hi(mselt_ref[...], imp)[0:nr, :]
        score = jnp.where(forced, jnp.inf, jnp.where(valid, blk, -jnp.inf))
        sel = (valid & (_rank_rows(score, n_sel) < N_SEL)).astype(f32)
        for h in range(NSA_HPG):
            sel8_s[0:nr, (g * NSA_HPG + h) * TQ:(g * NSA_HPG + h + 1) * TQ] = sel

    krow = _iota((TQ, cols), 0)
    blocks_per_tile = TQ // SEL_BLK

    causal = tpos >= t0 + krow

    def sel_step(kt, diagonal):
        k = kvn_ref[0, pl.ds(kt * TQ, TQ), 2 * LANES:3 * LANES]
        v = kvn_ref[0, pl.ds(kt * TQ, TQ), 3 * LANES:4 * LANES]
        chosen = sel8_s[pl.ds(kt * blocks_per_tile, 1), :]
        for i in range(1, blocks_per_tile):
            chosen = jnp.where(krow < i * SEL_BLK, chosen, sel8_s[pl.ds(kt * blocks_per_tile + i, 1), :])
        mask = (chosen > 0.5) & causal if diagonal else chosen > 0.5
        _softmax_step_t(_dot(k, q8) + bt_ref[qt - kt], mask, v, m_s, l_s, acc_s)

    _softmax_reset(m_s, l_s, acc_s)
    lax.fori_loop(0, qt, lambda kt, c: (sel_step(kt, False), c)[1], 0)
    sel_step(qt, True)
    osel_s[...] = acc_s[...] / l_s[...]

    w_tiles = WINDOW // TQ

    def win_step(kt, mask):
        k = kvw_ref[0, pl.ds(kt * TQ, TQ), 0:LANES]
        v = kvw_ref[0, pl.ds(kt * TQ, TQ), LANES:2 * LANES]
        _softmax_step_t(_dot(k, q8) + bt_ref[qt - kt], mask, v, m_s, l_s, acc_s)

    _softmax_reset(m_s, l_s, acc_s)

    @pl.when(qt >= w_tiles)
    def _():
        first = qt - w_tiles
        win_step(first, tpos - (first * TQ + krow) <= WINDOW)

    lax.fori_loop(jnp.maximum(qt - w_tiles + 1, 0), qt, lambda kt, c: (win_step(kt, None), c)[1], 0)
    win_step(qt, causal)
    owin = acc_s[...] / l_s[...]

    gat = ga_ref[...].T
    pieces = []
    for h in range(NSA_HEADS):
        g = h // NSA_HPG
        c0, c1 = h * TQ, (h + 1) * TQ
        o = (gat[3 * h:3 * h + 1, :] * ocmp_s[:, c0:c1] + gat[3 * h + 1:3 * h + 2, :] * osel_s[:, c0:c1]
             + gat[3 * h + 2:3 * h + 3, :] * owin[:, c0:c1])
        pieces.append(o[g * HEAD_DIM:(g + 1) * HEAD_DIM, :])
    o_ref[...] = jnp.concatenate(pieces, axis=0).T


def _nsa_prompt(qa, kvn, kvw, ga, bias_c, bias_t, wbig, pe, batch, seq):
    nq = seq // TQ
    n_seg = seq // CMP_STRIDE
    nc = max(LANES, n_seg)
    cols = NSA_HEADS * TQ
    mselt = _cmp_to_sel_matrix(nc, LANES).T
    seg_w = CMP_STRIDE * NSA_GROUPS * HEAD_DIM
    vm = (2 * (seq * 768 * 4 + bias_t.size * 4 + wbig.size * 2 + cols * nc * 4) + 2 * nc * seg_w * 4
          + seq * LANES * 4 + 8 * cols * LANES * 4 + (12 << 20))
    return pl.pallas_call(
        functools.partial(_nsa_prompt_body, seq=seq),
        grid=(batch, nq),
        in_specs=[pl.BlockSpec((TQ, qa.shape[1]), lambda b, q: (b * nq + q, 0)),
                  pl.BlockSpec((1, seq, kvn.shape[2]), lambda b, q: (b, 0, 0)),
                  pl.BlockSpec((1, seq, kvw.shape[2]), lambda b, q: (b, 0, 0)),
                  pl.BlockSpec((TQ, LANES), lambda b, q: (b * nq + q, 0)),
                  pl.BlockSpec((1, nc, cols), lambda b, q: (q, 0, 0)),
                  pl.BlockSpec(bias_t.shape, lambda b, q: (0, 0, 0)),
                  pl.BlockSpec(wbig.shape, lambda b, q: (0, 0, 0, 0)),
                  pl.BlockSpec(pe.shape, lambda b, q: (0, 0, 0, 0)),
                  pl.BlockSpec(mselt.shape, lambda b, q: (0, 0))],
        out_specs=pl.BlockSpec((TQ, qa.shape[1]), lambda b, q: (b * nq + q, 0)),
        out_shape=jax.ShapeDtypeStruct(qa.shape, f32),
        scratch_shapes=[pltpu.VMEM((seq, LANES), f32), pltpu.VMEM((2, nc, seg_w), f32),
                        pltpu.VMEM((nc, LANES), f32), pltpu.VMEM((nc, LANES), f32),
                        pltpu.VMEM((LANES, cols), _MXU), pltpu.VMEM((LANES, cols), f32),
                        pltpu.VMEM((LANES, cols), f32), pltpu.VMEM((LANES, cols), f32),
                        pltpu.VMEM((1, cols), f32), pltpu.VMEM((1, cols), f32), pltpu.VMEM((LANES, cols), f32)],
        compiler_params=_vmem(vm),
        name="nsa_prompt",
    )(qa, kvn, kvw, ga, bias_c, bias_t, wbig, pe, mselt)


def _moba_prompt_body(qb_ref, kvb_ref, bt_ref, o_ref, kmean_s, q8_s, sel8_s, m_s, l_s, acc_s, *, seq):
    qt = pl.program_id(1)
    t0 = qt * TQ
    nb = seq // MOBA_BLK
    nr = _round_up(nb, SUBLANES)
    cols = MOBA_HEADS * TQ
    kw = MOBA_HEADS * HEAD_DIM
    scale = HEAD_DIM ** -0.5
    pairs = MOBA_HEADS // 2

    @pl.when(qt == 0)
    def _():
        kmean_s[...] = jnp.zeros(kmean_s.shape, f32)
        for n in range(nb):
            blk = kvb_ref[0, n * MOBA_BLK:(n + 1) * MOBA_BLK, 0:kw]
            kmean_s[n:n + 1, :] = jnp.sum(blk, axis=0, keepdims=True) * (1.0 / MOBA_BLK)
        sel8_s[...] = jnp.zeros(sel8_s.shape, f32)

    qtr = qb_ref[...].T
    zero = jnp.zeros((HEAD_DIM, TQ), f32)
    nblk = _iota((nr, TQ), 0)
    own = (t0 + _iota((nr, TQ), 1)) >> 8
    past = nblk < own
    for h in range(MOBA_HEADS):
        qh = qtr[h * HEAD_DIM:(h + 1) * HEAD_DIM, :]
        blk = jnp.concatenate([qh, zero] if h % 2 == 0 else [zero, qh], axis=0)
        q8_s[:, h * TQ:(h + 1) * TQ] = (blk * scale).astype(q8_s.dtype)
        gate = _dot_hi(kmean_s[:, (h // 2) * LANES:(h // 2 + 1) * LANES], blk)[0:nr, :]
        score = jnp.where(past, gate, -jnp.inf)
        sel = (past & (_rank_rows(score, nb) < MOBA_TOPK)) | (nblk == own)
        sel8_s[0:nr, h * TQ:(h + 1) * TQ] = sel.astype(f32)

    tpos = t0 + (_iota((1, cols), 1) & (TQ - 1))
    krow = _iota((TQ, cols), 0)

    def step(kt, diagonal):
        s_parts = []
        for pr in range(pairs):
            k = kvb_ref[0, pl.ds(kt * TQ, TQ), pr * LANES:(pr + 1) * LANES]
            s_parts.append(_dot(k, q8_s[:, 2 * pr * TQ:2 * (pr + 1) * TQ]))
        s = jnp.concatenate(s_parts, axis=1) + bt_ref[qt - kt]
        mask = sel8_s[pl.ds((kt * TQ) >> 8, 1), :] > 0.5
        if diagonal:
            mask = mask & (tpos >= t0 + krow)
        s = jnp.where(mask, s, NEG)
        m_old = m_s[...]
        m_new = jnp.maximum(m_old, jnp.max(s, axis=0, keepdims=True))
        alpha = jnp.exp(m_old - m_new)
        p = jnp.exp(s - m_new)
        l_s[...] = alpha * l_s[...] + jnp.sum(p, axis=0, keepdims=True)
        pv = []
        for pr in range(pairs):
            v = kvb_ref[0, pl.ds(kt * TQ, TQ), kw + pr * LANES:kw + (pr + 1) * LANES]
            pv.append(_dot_tn(v, p[:, 2 * pr * TQ:2 * (pr + 1) * TQ]))
        acc_s[...] = alpha * acc_s[...] + jnp.concatenate(pv, axis=1)
        m_s[...] = m_new

    _softmax_reset(m_s, l_s, acc_s)
    lax.fori_loop(0, qt, lambda kt, c: (step(kt, False), c)[1], 0)
    step(qt, True)
    o = acc_s[...] / l_s[...]
    pieces = []
    for h in range(MOBA_HEADS):
        half = h % 2
        pieces.append(o[half * HEAD_DIM:(half + 1) * HEAD_DIM, h * TQ:(h + 1) * TQ])
    o_ref[...] = jnp.concatenate(pieces, axis=0).T


def _moba_prompt(qb, kvb, bias_t, batch, seq):
    nq = seq // TQ
    cols = MOBA_HEADS * TQ
    vm = 2 * (seq * kvb.shape[2] * 4 + bias_t.size * 4) + 8 * cols * LANES * 4 + (12 << 20)
    return pl.pallas_call(
        functools.partial(_moba_prompt_body, seq=seq),
        grid=(batch, nq),
        in_specs=[pl.BlockSpec((TQ, qb.shape[1]), lambda b, q: (b * nq + q, 0)),
                  pl.BlockSpec((1, seq, kvb.shape[2]), lambda b, q: (b, 0, 0)),
                  pl.BlockSpec(bias_t.shape, lambda b, q: (0, 0, 0))],
        out_specs=pl.BlockSpec((TQ, qb.shape[1]), lambda b, q: (b * nq + q, 0)),
        out_shape=jax.ShapeDtypeStruct(qb.shape, f32),
        scratch_shapes=[pltpu.VMEM((LANES, MOBA_HEADS * HEAD_DIM), f32),
                        pltpu.VMEM((LANES, cols), _MXU), pltpu.VMEM((LANES, cols), f32),
                        pltpu.VMEM((1, cols), f32), pltpu.VMEM((1, cols), f32), pltpu.VMEM((LANES, cols), f32)],
        compiler_params=_vmem(vm),
        name="moba_prompt",
    )(qb, kvb, bias_t)


def _page_group(n_pages):
    for g in (8, 4, 2):
        if n_pages % g == 0:
            return g
    raise ValueError("the page count must be even")


def _stack_heads_nsa(q, q8_s):
    half = _iota((1, LANES), 1) >> 6
    for h in range(NSA_HEADS):
        g = h // NSA_HPG
        blk = q[:, (h // 2) * LANES:(h // 2 + 1) * LANES]
        if h % 2 != g:
            blk = pltpu.roll(blk, HEAD_DIM, 1)
        q8_s[h:h + 1, :] = jnp.where(half == g, blk, 0.0).astype(q8_s.dtype)


def _softmax_step_vt(s, mask, vt, m_s, l_s, acc_s):
    s = jnp.where(mask, s, NEG)
    m_old = m_s[...]
    m_new = jnp.maximum(m_old, jnp.max(s, axis=1, keepdims=True))
    alpha = jnp.exp(m_old - m_new)
    p = jnp.where(mask, jnp.exp(s - m_new), 0.0)
    l_s[...] = alpha * l_s[...] + jnp.sum(p, axis=1, keepdims=True)
    acc_s[...] = alpha * acc_s[...] + _dot_nt(p, vt)
    m_s[...] = m_new


def _nsa_sample_body(pt_ref, *refs, n_pages, group):
    del pt_ref
    page_refs = refs[:group]
    (qa_ref, kvn_ref, kvw_ref, st_ref, ga_ref, bc_ref, bs_ref, bn_ref, bw_ref, wbig_ref, pe_ref, msel_ref, o_ref,
     rows_s, xflat_s, kc_s, vc_s, q8_s, sel8_s, ocmp_s, m_s, l_s, acc_s) = refs[group:]
    ph = pl.program_id(1)
    p = pl.program_id(2)
    n_steps = n_pages // group
    t = n_pages * PAGE
    seg_per_page = PAGE // CMP_STRIDE
    n_seg = n_pages * seg_per_page
    n_cmp = n_seg - CMP_BLK // CMP_STRIDE + 1
    cur = t // SEL_BLK
    n_sel = cur + 1
    nc = kc_s.shape[0]
    ls = sel8_s.shape[1]
    keys = group * PAGE
    gd = NSA_GROUPS * HEAD_DIM
    scale = HEAD_DIM ** -0.5

    def one_key(q8, row):
        return _dot_nt(q8, jnp.broadcast_to(row, (SUBLANES, LANES)))[:, 0:1]

    @pl.when(ph == 0)
    def _():
        if nc > n_seg:
            @pl.when(p == 0)
            def _():
                xflat_s[...] = jnp.zeros(xflat_s.shape, f32)
        for c in range(2):
            for g, ref in enumerate(page_refs):
                rows_s[...] = ref[0, c].reshape(gd, PAGE).T
                row0 = pl.multiple_of((p * group + g) * seg_per_page, SUBLANES)
                for r in range(CMP_STRIDE):
                    xflat_s[c, pl.ds(row0, seg_per_page), r * LANES:(r + 1) * LANES] = (
                        rows_s[pl.ds(r, seg_per_page, stride=CMP_STRIDE), :])

        @pl.when(p == n_steps - 1)
        def _():
            kc_s[...] = _compress(xflat_s, 0, pe_ref, wbig_ref)
            vc_s[...] = _compress(xflat_s, 1, pe_ref, wbig_ref)
            _stack_heads_nsa(qa_ref[0] * scale, q8_s)
            q8 = q8_s[...]
            ncol = _iota((NSA_HEADS, nc), 1)
            mask_c = ncol < n_cmp
            sc = jnp.where(mask_c, _dot_nt(q8, kc_s[...]) + bc_ref[...], NEG)
            e = jnp.exp(sc - jnp.max(sc, axis=1, keepdims=True))
            p_c = jnp.where(mask_c, e / jnp.sum(e, axis=1, keepdims=True), 0.0)
            ocmp_s[...] = _dot(p_c, vc_s[...])
            same_group = ((_iota((NSA_HEADS, NSA_HEADS), 0) >> 2) == (_iota((NSA_HEADS, NSA_HEADS), 1) >> 2)).astype(f32)
            blk = _dot_hi(_dot_hi(same_group, p_c), msel_ref[...])
            jb = _iota((NSA_HEADS, ls), 1)
            valid = jb < n_sel
            forced = valid & ((jb == 0) | (jb == cur) | (jb == cur - 1))
            score = jnp.where(forced, jnp.inf, jnp.where(valid, blk, -jnp.inf))
            sel8_s[...] = (valid & (_rank_lanes(score, n_sel) < N_SEL)).astype(sel8_s.dtype)

    @pl.when(ph == 1)
    def _():
        q8 = q8_s[...]

        @pl.when(p == 0)
        def _():
            m_s[...] = one_key(q8, kvn_ref[0, :, 2 * LANES:3 * LANES]) + bn_ref[:, 0:1]
            l_s[...] = jnp.ones(l_s.shape, f32)
            v_new = kvn_ref[0, :, 3 * LANES:4 * LANES].astype(_MXU).astype(f32)
            acc_s[...] = jnp.broadcast_to(v_new, acc_s.shape)

        kt = jnp.concatenate([r[0, 0].reshape(gd, PAGE) for r in page_refs], axis=1)
        vt = jnp.concatenate([r[0, 1].reshape(gd, PAGE) for r in page_refs], axis=1)
        s = _dot(q8, kt) + bs_ref[0]
        jrow = _iota((ls, keys), 0)
        jcol = (p * keys + _iota((ls, keys), 1)) >> 6
        chosen = _dot(sel8_s[...], (jrow == jcol).astype(f32)) > 0.5
        _softmax_step_vt(s, chosen, vt, m_s, l_s, acc_s)

        @pl.when(p == n_steps - 1)
        def _():
            osel = acc_s[...] / l_s[...]
            win = st_ref.shape[4]
            sw = _dot(q8, st_ref[0, 0].reshape(gd, win)) + bw_ref[...]
            sn = one_key(q8, kvw_ref[0, :, 0:LANES]) + bn_ref[:, 0:1]
            mw = jnp.maximum(jnp.max(sw, axis=1, keepdims=True), sn)
            ew = jnp.exp(sw - mw)
            en = jnp.exp(sn - mw)
            v_new = kvw_ref[0, :, LANES:2 * LANES].astype(_MXU).astype(f32)
            owin = ((_dot_nt(ew, st_ref[0, 1].reshape(gd, win)) + en * v_new)
                    / (jnp.sum(ew, axis=1, keepdims=True) + en))
            ga = ga_ref[0]
            for h in range(NSA_HEADS):
                g = h // NSA_HPG
                o = (ga[:, 3 * h:3 * h + 1] * ocmp_s[h:h + 1, :] + ga[:, 3 * h + 1:3 * h + 2] * osel[h:h + 1, :]
                     + ga[:, 3 * h + 2:3 * h + 3] * owin[h:h + 1, :])
                o_ref[0, :, h * HEAD_DIM:(h + 1) * HEAD_DIM] = o[:, g * HEAD_DIM:(g + 1) * HEAD_DIM]


def _nsa_sample(page_table, cache_t, qa, kvn, kvw, state_t, ga, tab, wbig, pe):
    bs, n_pages = page_table.shape
    group = _page_group(n_pages)
    n_steps = n_pages // group
    t = n_pages * PAGE
    win = state_t.shape[4]
    n_cmp = n_pages * (PAGE // CMP_STRIDE) - CMP_BLK // CMP_STRIDE + 1
    nc = max(LANES, n_pages * (PAGE // CMP_STRIDE))
    ls = _round_up(t // SEL_BLK + 1, LANES)
    msel = _cmp_to_sel_matrix(nc, ls)
    d_last = t - (CMP_STRIDE * (n_cmp - 1) + CMP_BLK - 1)
    bias_c = tab[:, d_last:t - CMP_BLK + 2:CMP_STRIDE][:, ::-1]
    bias_c = jnp.pad(bias_c, ((0, 0), (0, nc - n_cmp)))
    bias_s = jnp.transpose(tab[:, 1:t + 1][:, ::-1].reshape(NSA_HEADS, n_steps, group * PAGE), (1, 0, 2))
    bias_n = jnp.broadcast_to(tab[:, 0:1], (NSA_HEADS, LANES))
    bias_w = tab[:, 1:win + 1][:, ::-1]
    seg_w = CMP_STRIDE * NSA_GROUPS * HEAD_DIM
    req = lambda b, ph, p, pt: (b, 0, 0)
    c2 = lambda b, ph, p, pt: (0, 0)
    c4 = lambda b, ph, p, pt: (0, 0, 0, 0)

    def page_spec(g):
        return pl.BlockSpec((1, 2, NSA_GROUPS, HEAD_DIM, PAGE),
                            lambda b, ph, p, pt: (pt[b, p * group + g], ph, 0, 0, 0))

    grid_spec = pltpu.PrefetchScalarGridSpec(
        num_scalar_prefetch=1,
        grid=(bs, 2, n_steps),
        in_specs=([page_spec(g) for g in range(group)]
                  + [pl.BlockSpec((1, 1, qa.shape[2]), req), pl.BlockSpec((1, 1, kvn.shape[2]), req),
                     pl.BlockSpec((1, 1, kvw.shape[2]), req),
                     pl.BlockSpec((1,) + state_t.shape[1:], lambda b, ph, p, pt: (b, 0, 0, 0, 0)),
                     pl.BlockSpec((1, 1, LANES), req),
                     pl.BlockSpec(bias_c.shape, c2),
                     pl.BlockSpec((1, NSA_HEADS, group * PAGE), lambda b, ph, p, pt: (p * ph, 0, 0)),
                     pl.BlockSpec(bias_n.shape, c2), pl.BlockSpec(bias_w.shape, c2),
                     pl.BlockSpec(wbig.shape, c4), pl.BlockSpec(pe.shape, c4), pl.BlockSpec(msel.shape, c2)]),
        out_specs=pl.BlockSpec((1, 1, qa.shape[2]), req),
        scratch_shapes=[pltpu.VMEM((PAGE, LANES), f32),
                        pltpu.VMEM((2, nc, seg_w), f32), pltpu.VMEM((nc, LANES), f32), pltpu.VMEM((nc, LANES), f32),
                        pltpu.VMEM((NSA_HEADS, LANES), _MXU), pltpu.VMEM((NSA_HEADS, ls), _MXU),
                        pltpu.VMEM((NSA_HEADS, LANES), f32),
                        pltpu.VMEM((NSA_HEADS, 1), f32), pltpu.VMEM((NSA_HEADS, 1), f32),
                        pltpu.VMEM((NSA_HEADS, LANES), f32)])
    vm = (2 * nc * seg_w * 4 + 2 * (wbig.size * 2 + msel.size * 4 + win * 256 * 4 + 2 * group * PAGE * LANES * 4)
          + (16 << 20))
    return pl.pallas_call(
        functools.partial(_nsa_sample_body, n_pages=n_pages, group=group),
        grid_spec=grid_spec,
        out_shape=jax.ShapeDtypeStruct(qa.shape, f32),
        compiler_params=_vmem(vm),
        name="nsa_sample",
    )(page_table, *([cache_t] * group), qa, kvn, kvw, state_t, ga, bias_c, bias_s, bias_n, bias_w, wbig, pe, msel)


def _moba_sample_body(pt_ref, *refs, n_pages, group):
    del pt_ref
    page_refs = refs[:group]
    qb_ref, kvb_ref, bs_ref, bn_ref, o_ref, gate_s, mx_s, sum_s, acc_s = refs[group:]
    p = pl.program_id(1)
    n_steps = n_pages // group
    t = n_pages * PAGE
    kw = MOBA_HEADS * HEAD_DIM
    pages_per_blk = MOBA_BLK // PAGE
    blks_per_step = group // pages_per_blk
    own = t // MOBA_BLK
    scale = HEAD_DIM ** -0.5
    head_lane = (_iota((MOBA_HEADS, kw), 1) >> 6) == _iota((MOBA_HEADS, kw), 0)
    q8 = jnp.where(head_lane, jnp.broadcast_to(qb_ref[0], (MOBA_HEADS, kw)), 0.0)
    lane = _iota((MOBA_HEADS, LANES), 1)

    @pl.when(p == 0)
    def _():
        gate_s[...] = jnp.zeros(gate_s.shape, f32)
        mx_s[...] = jnp.full(mx_s.shape, NEG, f32)
        sum_s[...] = jnp.zeros(sum_s.shape, f32)

    for j in range(blks_per_step):
        blk = p * blks_per_step + j
        pages = page_refs[j * pages_per_blk:(j + 1) * pages_per_blk]
        kt = jnp.concatenate([r[0, 0].reshape(kw, PAGE) for r in pages], axis=1)
        vt = jnp.concatenate([r[0, 1].reshape(kw, PAGE) for r in pages], axis=1)
        raw = _dot_hi(q8, kt)
        s = raw * scale + bs_ref[0, :, j * MOBA_BLK:(j + 1) * MOBA_BLK]
        m = jnp.max(s, axis=1, keepdims=True)
        pr = jnp.exp(s - m)
        here = lane == blk
        gate_s[...] = jnp.where(here, jnp.sum(raw, axis=1, keepdims=True) * (1.0 / MOBA_BLK), gate_s[...])
        mx_s[...] = jnp.where(here, m, mx_s[...])
        sum_s[...] = jnp.where(here, jnp.sum(pr, axis=1, keepdims=True), sum_s[...])
        acc_s[blk] = _dot_nt(pr, vt)

    @pl.when(p == n_steps - 1)
    def _():
        past = lane < own
        score = jnp.where(past, gate_s[...], -jnp.inf)
        sel = past & (_rank_lanes(score, own) < MOBA_TOPK)
        s_new = jnp.sum(q8 * kvb_ref[0, :, 0:kw], axis=1, keepdims=True) * scale + bn_ref[:, 0:1]
        v_new = kvb_ref[0, :, kw:2 * kw].astype(_MXU).astype(f32)
        top = jnp.maximum(jnp.max(jnp.where(sel, mx_s[...], NEG), axis=1, keepdims=True), s_new)
        wts = jnp.where(sel, jnp.exp(mx_s[...] - top), 0.0)
        w_new = jnp.exp(s_new - top)
        den = jnp.sum(wts * sum_s[...], axis=1, keepdims=True) + w_new
        num = w_new * v_new
        for j in range(own):
            num = num + wts[:, j:j + 1] * acc_s[j]
        o8 = jnp.where(head_lane, num / den, 0.0)
        o_ref[0] = jnp.sum(o8, axis=0, keepdims=True)


def _moba_sample(page_table, cache_t, qb, kvb, tab):
    bs, n_pages = page_table.shape
    group = _page_group(n_pages)
    n_steps = n_pages // group
    t = n_pages * PAGE
    kw = MOBA_HEADS * HEAD_DIM
    n_blk = t // MOBA_BLK
    assert n_blk <= LANES
    bias_s = jnp.transpose(tab[:, 1:t + 1][:, ::-1].reshape(MOBA_HEADS, n_steps, group * PAGE), (1, 0, 2))
    bias_n = jnp.broadcast_to(tab[:, 0:1], (MOBA_HEADS, LANES))
    req = lambda b, p, pt: (b, 0, 0)

    def page_spec(g):
        return pl.BlockSpec((1, 2, MOBA_HEADS, HEAD_DIM, PAGE), lambda b, p, pt: (pt[b, p * group + g], 0, 0, 0, 0))

    grid_spec = pltpu.PrefetchScalarGridSpec(
        num_scalar_prefetch=1,
        grid=(bs, n_steps),
        in_specs=([page_spec(g) for g in range(group)]
                  + [pl.BlockSpec((1, 1, kw), req), pl.BlockSpec((1, 1, 2 * kw), req),
                     pl.BlockSpec((1, MOBA_HEADS, group * PAGE), lambda b, p, pt: (p, 0, 0)),
                     pl.BlockSpec(bias_n.shape, lambda b, p, pt: (0, 0))]),
        out_specs=pl.BlockSpec((1, 1, kw), req),
        scratch_shapes=[pltpu.VMEM((MOBA_HEADS, LANES), f32), pltpu.VMEM((MOBA_HEADS, LANES), f32),
                        pltpu.VMEM((MOBA_HEADS, LANES), f32), pltpu.VMEM((n_blk, MOBA_HEADS, kw), f32)])
    return pl.pallas_call(
        functools.partial(_moba_sample_body, n_pages=n_pages, group=group),
        grid_spec=grid_spec,
        out_shape=jax.ShapeDtypeStruct(qb.shape, f32),
        compiler_params=_vmem(2 * group * PAGE * 2 * kw * 4 + (24 << 20)),
        name="moba_sample",
    )(page_table, *([cache_t] * group), qb, kvb, bias_s, bias_n)


def _layer_norm(x, g, b):
    mu = jnp.mean(x, axis=-1, keepdims=True)
    xc = x - mu
    var = jnp.mean(xc * xc, axis=-1, keepdims=True)
    return xc * lax.rsqrt(var + LN_EPS) * g + b


def _merge_body(oa_ref, ob_ref, gm_ref, x_ref, wa_ref, wb_ref, wo_ref, g_ref, b_ref, h_ref, ht_ref, *, alpha):
    d = x_ref.shape[1]
    ya = _dot(oa_ref[...], wa_ref[...])
    yb = _dot(ob_ref[...], wb_ref[...])
    gm = gm_ref[...]
    mix = _dot(gm[:, :d] * ya + gm[:, d:] * yb, wo_ref[...])
    h = _layer_norm(alpha * x_ref[...] + mix, g_ref[...], b_ref[...])
    h_ref[...] = h
    ht_ref[...] = h.T.astype(ht_ref.dtype)


def _merge(oa, ob, gm, x2d, wa, wb, wo, g, b, alpha):
    n, d = x2d.shape
    tm = 256 if n % 256 == 0 else n
    row = lambda i: (i, 0)
    full = lambda i: (0, 0)
    vm = 2 * (tm * (oa.shape[1] + ob.shape[1] + gm.shape[1] + 2 * d) * 4 + tm * d * 2
              + (wa.size + wb.size + wo.size) * 2) + (16 << 20)
    return pl.pallas_call(
        functools.partial(_merge_body, alpha=alpha),
        grid=(n // tm,),
        in_specs=[pl.BlockSpec((tm, oa.shape[1]), row), pl.BlockSpec((tm, ob.shape[1]), row),
                  pl.BlockSpec((tm, gm.shape[1]), row), pl.BlockSpec((tm, d), row),
                  pl.BlockSpec(wa.shape, full), pl.BlockSpec(wb.shape, full), pl.BlockSpec(wo.shape, full),
                  pl.BlockSpec((1, d), full), pl.BlockSpec((1, d), full)],
        out_specs=[pl.BlockSpec((tm, d), row), pl.BlockSpec((d, tm), lambda i: (0, i))],
        out_shape=[jax.ShapeDtypeStruct((n, d), f32), jax.ShapeDtypeStruct((d, n), _MXU)],
        compiler_params=_vmem(vm),
        name="merge",
    )(oa, ob, gm, x2d, wa, wb, wo, g, b)


_NOT_TOP = 99.0
_CAND_B = tuple(PEER_TOPK // (a + 1) for a in range(SUBLANES))


def _extract_top(work_s, n_rounds, on_round, one_per_round):
    rows = _iota(work_s.shape, 0)

    def body(r, carry):
        w = work_s[...]
        m = jnp.max(w, axis=0, keepdims=True)
        hit = w == m
        if one_per_round:
            hit = rows == jnp.min(jnp.where(hit, rows, work_s.shape[0]), axis=0, keepdims=True)
        on_round(r, m, hit)
        work_s[...] = jnp.where(hit, -jnp.inf, w)
        return carry

    lax.fori_loop(0, n_rounds, body, 0)


def _top_rounds(work_s, init, on_round, n_hit):
    init()
    _extract_top(work_s, PEER_TOPK, on_round, one_per_round=False)
    bad = jnp.max(jnp.abs(n_hit() - float(PEER_TOPK)))

    @pl.when(bad > 0.5)
    def _():
        init()
        _extract_top(work_s, PEER_TOPK, on_round, one_per_round=True)


def _peer_select(ht_ref, wqt_ref, keys_ref, qt_s, rank_s, exp_s, sv_s, work_s, cwork_s, cand_s, cnt_s, f0_s,
                 rk1_s, e1_s):
    tn = ht_ref.shape[1]
    nk = PEER_N_KEYS
    qt_s[...] = jnp.dot(wqt_ref[...], ht_ref[...], preferred_element_type=f32)

    def per_half(hc, carry):
        exp_s[hc] = _dot(keys_ref[hc], qt_s[pl.ds(pl.multiple_of(hc * PEER_HALF, PEER_HALF), PEER_HALF), :])

        def init():
            work_s[...] = exp_s[hc]
            rank_s[hc] = jnp.full((nk, tn), _NOT_TOP, f32)

        def on_round(r, m, hit):
            sv_s[hc, pl.ds(r, 1), :] = m
            rank_s[hc] = jnp.where(hit, jnp.asarray(r, f32), rank_s[hc])

        def n_hit():
            return jnp.sum(jnp.where(rank_s[hc] < _NOT_TOP, 1.0, 0.0), axis=0, keepdims=True)

        _top_rounds(work_s, init, on_round, n_hit)
        exp_s[hc] = jnp.exp(exp_s[hc] - sv_s[hc, 0:1, :])
        return carry

    lax.fori_loop(0, 2 * PEER_HEADS, per_half, 0)

    n_cand = cand_s.shape[0]

    def per_head(h, carry):
        sv0 = sv_s[2 * h]
        sv1 = sv_s[2 * h + 1]
        e0 = jnp.exp(sv0 - sv0[0:1, :])
        e1 = jnp.exp(sv1 - sv1[0:1, :])
        brow = _iota((SUBLANES, tn), 0)
        tiles, etiles = [sv0[0:1, :] + sv1, ], [e0[0:1, :] * e1]
        for a in range(1, SUBLANES):
            ok = brow < _CAND_B[a]
            tiles.append(jnp.where(ok, sv0[a:a + 1, :] + sv1[0:SUBLANES, :], -jnp.inf))
            etiles.append(jnp.where(ok, e0[a:a + 1, :] * e1[0:SUBLANES, :], 0.0))
        tiles.append(sv0[SUBLANES:, :] + sv1[0:1, :])
        etiles.append(e0[SUBLANES:, :] * e1[0:1, :])
        cands = jnp.concatenate(tiles, axis=0)

        def init():
            cwork_s[...] = cands
            cand_s[...] = jnp.zeros(cand_s.shape, f32)

        def on_round(r, m, hit):
            cand_s[...] = jnp.where(hit, 1.0, cand_s[...])

        _top_rounds(cwork_s, init, on_round, lambda: jnp.sum(cand_s[...], axis=0, keepdims=True))
        chosen = cand_s[...]
        z = jnp.sum(chosen * jnp.concatenate(etiles, axis=0), axis=0, keepdims=True)
        cnts = [jnp.sum(chosen[0:PEER_TOPK], axis=0, keepdims=True)]
        for a in range(1, SUBLANES):
            r0 = PEER_TOPK + (a - 1) * SUBLANES
            cnts.append(jnp.sum(chosen[r0:r0 + SUBLANES], axis=0, keepdims=True))
        tail = chosen[n_cand - SUBLANES:]
        cnts += [tail[a:a + 1, :] for a in range(SUBLANES)]
        ra = rank_s[2 * h]
        c0 = jnp.zeros((nk, tn), f32)
        for a in range(PEER_TOPK):
            c0 = jnp.where(ra == float(a), cnts[a], c0)
        cnt_s[h] = c0
        f0_s[h] = exp_s[2 * h] / z
        rk1_s[h] = rank_s[2 * h + 1].astype(rk1_s.dtype)
        e1_s[h] = exp_s[2 * h + 1].astype(e1_s.dtype)
        return carry

    lax.fori_loop(0, PEER_HEADS, per_head, 0)


def _gelu_t(u_tile, ht):
    at = jnp.dot(u_tile, ht, preferred_element_type=f32)
    return (0.5 * at * (1.0 + lax.erf(at * math.sqrt(0.5)))).astype(_MXU)


def _peer_body(ht_ref, h_ref, wqt_ref, keys_ref, u0_ref, un_ref, vt_ref, g_ref, b_ref, y_ref,
               qt_s, rank_s, exp_s, sv_s, work_s, cwork_s, cand_s, cnt_s, f0_s, rk1_s, e1_s, gel_s, wc_s, acc_s,
               *, alpha, ti):
    e = pl.program_id(1)
    last = pl.num_programs(1) - 1
    tn = ht_ref.shape[1]

    @pl.when(e == 0)
    def _():
        _peer_select(ht_ref, wqt_ref, keys_ref, qt_s, rank_s, exp_s, sv_s, work_s, cwork_s, cand_s, cnt_s, f0_s,
                     rk1_s, e1_s)
        acc_s[...] = jnp.zeros(acc_s.shape, f32)
        gel_s[0] = _gelu_t(u0_ref[...], ht_ref[...])

    for ii in range(ti):
        i = e * ti + ii
        r0, r1 = ii * PEER_N_KEYS, (ii + 1) * PEER_N_KEYS
        w = jnp.zeros((PEER_N_KEYS, tn), _MXU)
        for h in range(PEER_HEADS):
            cnt = cnt_s[h, pl.ds(i, 1), :].astype(_MXU)
            f0 = f0_s[h, pl.ds(i, 1), :].astype(_MXU)
            w = w + jnp.where(rk1_s[h] < cnt, e1_s[h], jnp.zeros((), _MXU)) * f0
        wc_s[r0:r1, :] = w * gel_s[e % 2, r0:r1, :]
        if ii % 2 == 1:
            k0, k1 = (ii - 1) * PEER_N_KEYS, (ii + 1) * PEER_N_KEYS
            acc_s[...] += jnp.dot(vt_ref[:, k0:k1], wc_s[k0:k1, :], preferred_element_type=f32)
    gel_s[(e + 1) % 2] = _gelu_t(un_ref[...], ht_ref[...])

    @pl.when(e == last)
    def _():
        y_ref[...] = _layer_norm(alpha * h_ref[...] + acc_s[...].T, g_ref[...], b_ref[...])


def _peer(ht, h, wqt, keys, u, vt, g, b, alpha):
    d, n = ht.shape
    n_exp = u.shape[0]
    tn = 512 if n % 512 == 0 else n
    ti = 8
    te = ti * PEER_N_KEYS
    n_half = 2 * PEER_HEADS
    n_cand = PEER_TOPK + (SUBLANES - 1) * SUBLANES + SUBLANES
    tok = lambda t, e: (0, t)
    full2 = lambda t, e: (0, 0)
    scratch = [pltpu.VMEM((wqt.shape[0], tn), f32),
               pltpu.VMEM((n_half, PEER_N_KEYS, tn), f32), pltpu.VMEM((n_half, PEER_N_KEYS, tn), f32),
               pltpu.VMEM((n_half, PEER_TOPK, tn), f32), pltpu.VMEM((PEER_N_KEYS, tn), f32),
               pltpu.VMEM((n_cand, tn), f32), pltpu.VMEM((n_cand, tn), f32),
               pltpu.VMEM((PEER_HEADS, PEER_N_KEYS, tn), f32), pltpu.VMEM((PEER_HEADS, PEER_N_KEYS, tn), f32),
               pltpu.VMEM((PEER_HEADS, PEER_N_KEYS, tn), _MXU), pltpu.VMEM((PEER_HEADS, PEER_N_KEYS, tn), _MXU),
               pltpu.VMEM((2, te, tn), _MXU), pltpu.VMEM((te, tn), _MXU), pltpu.VMEM((d, tn), f32)]
    vm = (wqt.shape[0] * tn * 4 + 2 * n_half * PEER_N_KEYS * tn * 4 + 2 * PEER_HEADS * PEER_N_KEYS * tn * 6
          + 3 * te * tn * 2 + d * tn * 4
          + 2 * (d * tn * 2 + 2 * tn * d * 4 + wqt.size * 2 + keys.size * 2 + 3 * te * d * 2) + (14 << 20))
    n_tiles = n_exp // te
    return pl.pallas_call(
        functools.partial(_peer_body, alpha=alpha, ti=ti),
        grid=(n // tn, n_tiles),
        in_specs=[pl.BlockSpec((d, tn), tok),
                  pl.BlockSpec((tn, d), lambda t, e: (t, 0)),
                  pl.BlockSpec(wqt.shape, full2),
                  pl.BlockSpec(keys.shape, lambda t, e: (0, 0, 0)),
                  pl.BlockSpec((te, d), full2),
                  pl.BlockSpec((te, d), lambda t, e: (jnp.minimum(e + 1, n_tiles - 1), 0)),
                  pl.BlockSpec((d, te), lambda t, e: (0, e)),
                  pl.BlockSpec((1, d), full2), pl.BlockSpec((1, d), full2)],
        out_specs=pl.BlockSpec((tn, d), lambda t, e: (t, 0)),
        out_shape=jax.ShapeDtypeStruct((n, d), f32),
        scratch_shapes=scratch,
        compiler_params=_vmem(vm),
        name="peer",
    )(ht, h, wqt, keys, u, u, vt, g, b)


def kernel(x_prompt, x_sample, cache_nsa, cache_moba, state_nsa_win, page_table, w_in, nsa_cmp_pe, nsa_cmp_w, w_branch_a, w_branch_b, w_out, ln1_g, ln1_b, peer_wq, peer_keys, peer_u, peer_v, ln2_g, ln2_b, rel_bias):
    depth = w_in.shape[0]
    batch, seq, d = x_prompt.shape
    bs, dec_seq, _ = x_sample.shape
    n_pages = page_table.shape[1]
    past = n_pages * PAGE
    win_len = state_nsa_win.shape[2]
    assert dec_seq == 1 and cache_nsa.shape[2] == PAGE and cache_moba.shape[2] == PAGE
    assert seq % MOBA_BLK == 0 and past % MOBA_BLK == 0 and win_len == WINDOW and seq % TQ == 0
    alpha = (2 * depth) ** 0.25
    nq = seq // TQ
    n_tok = batch * seq

    tab = _bias_by_distance(rel_bias, max(seq, past + 1) + TQ)
    tab_a, tab_b = tab[:NSA_HEADS], tab[NSA_HEADS:]
    tiles_a, tiles_b = _bias_tiles_t(tab_a, nq), _bias_tiles_t(tab_b, nq)
    cmp_bias = _nsa_cmp_bias_t(tab_a, seq, max(LANES, seq // CMP_STRIDE))

    xp = x_prompt.reshape(n_tok, d)
    xs = x_sample.reshape(bs, d)
    outs = [[] for _ in range(6)]
    for l in range(depth):
        w_packed = _pack_w_in(w_in[l], d)
        wbig, pe = _pack_cmp(nsa_cmp_pe[l], nsa_cmp_w[l])
        wa, wb, wo = w_branch_a[l].astype(_MXU), w_branch_b[l].astype(_MXU), w_out[l].astype(_MXU)
        wqt = peer_wq[l].T.astype(_MXU)
        keys = peer_keys[l].reshape(2 * PEER_HEADS, PEER_N_KEYS, PEER_HALF).astype(_MXU)
        u = peer_u[l].astype(_MXU)
        vt = peer_v[l].T.astype(_MXU)
        g1, b1, g2, b2 = ln1_g[l][None], ln1_b[l][None], ln2_g[l][None], ln2_b[l][None]

        def ffn(oa, ob, gm, x2d):
            h, ht = _merge(oa, ob, gm, x2d, wa, wb, wo, g1, b1, alpha)
            return _peer(ht, h, wqt, keys, u, vt, g2, b2, alpha)

        qa, kvn, kvw, qb, kvb, gm, ga, kvn_t, kvw_t, kvb_t = _project(xp, w_packed, d, seq)
        oa = _nsa_prompt(qa, kvn.reshape(batch, seq, -1), kvw.reshape(batch, seq, -1), ga, cmp_bias, tiles_a,
                         wbig, pe, batch, seq)
        ob = _moba_prompt(qb, kvb.reshape(batch, seq, -1), tiles_b, batch, seq)
        xp = ffn(oa, ob, gm, xp)
        to_row_major = lambda a, c, g: jnp.transpose(a.reshape(batch, c, g, HEAD_DIM, -1), (0, 4, 1, 2, 3))
        outs[0].append(to_row_major(kvn_t, 4, NSA_GROUPS))
        outs[1].append(to_row_major(kvb_t, 2, MOBA_HEADS))
        outs[2].append(to_row_major(kvw_t[:, :, seq - min(WINDOW, seq):], 2, NSA_GROUPS))

        qa, kvn, kvw, qb, kvb, gm, ga = _project(xs, w_packed, d)
        r3 = lambda a: a.reshape(bs, 1, -1)
        to_feature_major = lambda a: jnp.transpose(a, (0, 2, 3, 4, 1))
        oa = _nsa_sample(page_table, to_feature_major(cache_nsa[l]), r3(qa), r3(kvn), r3(kvw),
                         to_feature_major(state_nsa_win[l]), r3(ga), tab_a, wbig, pe)
        ob = _moba_sample(page_table, to_feature_major(cache_moba[l]), r3(qb), r3(kvb), tab_b)
        xs = ffn(oa.reshape(bs, -1), ob.reshape(bs, -1), gm, xs)
        outs[3].append(kvn.reshape(bs, 1, 4, NSA_GROUPS, HEAD_DIM))
        outs[4].append(kvb.reshape(bs, 1, 2, MOBA_HEADS, HEAD_DIM))
        win_new = jnp.concatenate([state_nsa_win[l][:, 1:], kvw.reshape(bs, 1, 2, NSA_GROUPS, HEAD_DIM)], axis=1)
        outs[5].append(win_new)

    stacked = [jnp.stack(o) for o in outs]
    return (xp.reshape(batch, seq, d), xs.reshape(bs, 1, d), *stacked)
```

```python
import functools
import math

import jax
import jax.numpy as jnp
from jax import lax
from jax.experimental import pallas as pl
from jax.experimental.pallas import tpu as pltpu

HEAD_DIM = 64
NSA_HEADS = 8
NSA_GROUPS = 2
NSA_HPG = NSA_HEADS // NSA_GROUPS
CMP_BLK = 32
CMP_STRIDE = 16
SEL_BLK = 64
N_SEL = 16
WINDOW = 512
MOBA_HEADS = 8
MOBA_BLK = 256
MOBA_TOPK = 3
N_BUCKETS = 32
MAX_DISTANCE = 1024
PEER_HEADS = 8
PEER_N_KEYS = 128
PEER_HALF = 128
PEER_TOPK = 16
LN_EPS = 1e-5
NEG = -1e30

LANES = 128
SUBLANES = 8
VMEM_BYTES_V7X = 64 * 1024 * 1024

TQ = 128
PAGE = 128

f32 = jnp.float32
i32 = jnp.int32
_MXU = jnp.bfloat16
_HI = lax.Precision.HIGHEST


def _iota(shape, dim):
    return lax.broadcasted_iota(i32, shape, dim)


def _dot(a, b):
    return jnp.dot(a.astype(_MXU), b.astype(_MXU), preferred_element_type=f32)


def _dot_nt(a, b):
    return lax.dot_general(a.astype(_MXU), b.astype(_MXU), (((1,), (1,)), ((), ())),
                           preferred_element_type=f32)


def _dot_tn(a, b):
    return lax.dot_general(a.astype(_MXU), b.astype(_MXU), (((0,), (0,)), ((), ())), preferred_element_type=f32)


def _dot_hi(a, b):
    return jnp.dot(a, b, precision=_HI, preferred_element_type=f32)


def _dot_nt_hi(a, b):
    return lax.dot_general(a, b, (((1,), (1,)), ((), ())), precision=_HI, preferred_element_type=f32)


def _vmem(nbytes, flags=None):
    return pltpu.CompilerParams(vmem_limit_bytes=int(min(nbytes, VMEM_BYTES_V7X - (4 << 20))), flags=flags)


def _round_up(x, m):
    return -(-x // m) * m


def _rank_lanes(s, n):
    width = s.shape[1]
    j = _iota(s.shape, 1)
    rank = jnp.zeros(s.shape, i32)
    for r in range(1, n):
        lo = pltpu.roll(s, r, 1)
        rank = rank + jnp.where((j >= r) & (lo >= s), 1, 0)
        hi = pltpu.roll(s, width - r, 1)
        rank = rank + jnp.where((j + r < n) & (hi > s), 1, 0)
    return rank


def _rank_rows(s, n):
    j = _iota(s.shape, 0)
    rank = jnp.zeros(s.shape, i32)
    for k in range(n):
        row = s[k:k + 1, :]
        rank = rank + jnp.where((row > s) | ((row == s) & (j > k)), 1, 0)
    return rank


def _softmax_reset(m_s, l_s, acc_s):
    m_s[...] = jnp.full(m_s.shape, NEG, f32)
    l_s[...] = jnp.zeros(l_s.shape, f32)
    acc_s[...] = jnp.zeros(acc_s.shape, f32)


def _softmax_step(s, mask, v, m_s, l_s, acc_s):
    s = jnp.where(mask, s, NEG)
    m_old = m_s[...]
    m_new = jnp.maximum(m_old, jnp.max(s, axis=1, keepdims=True))
    alpha = jnp.exp(m_old - m_new)
    p = jnp.where(mask, jnp.exp(s - m_new), 0.0)
    l_s[...] = alpha * l_s[...] + jnp.sum(p, axis=1, keepdims=True)
    acc_s[...] = alpha * acc_s[...] + _dot(p, v)
    m_s[...] = m_new


def _softmax_step_t(s, mask, v, m_s, l_s, acc_s):
    if mask is not None:
        s = jnp.where(mask, s, NEG)
    m_old = m_s[...]
    m_new = jnp.maximum(m_old, jnp.max(s, axis=0, keepdims=True))
    alpha = jnp.exp(m_old - m_new)
    p = jnp.exp(s - m_new)
    l_s[...] = alpha * l_s[...] + jnp.sum(p, axis=0, keepdims=True)
    acc_s[...] = alpha * acc_s[...] + _dot_tn(v, p)
    m_s[...] = m_new


_W_QA = NSA_HEADS * HEAD_DIM
_W_KVN = 4 * NSA_GROUPS * HEAD_DIM
_W_KVW = 2 * NSA_GROUPS * HEAD_DIM
_W_GA = 3 * NSA_HEADS
_W_QB = MOBA_HEADS * HEAD_DIM
_W_KVB = 2 * MOBA_HEADS * HEAD_DIM


def _proj_body(x_ref, w_ref, qa_ref, kvn_ref, kvw_ref, qb_ref, kvb_ref, gm_ref, ga_ref, *kv_t_refs):
    xb = x_ref[...].astype(_MXU)
    col = 0
    kvn_t, kvw_t, kvb_t = kv_t_refs if kv_t_refs else (None, None, None)
    for ref, act, t_ref in ((qa_ref, None, None), (kvn_ref, None, kvn_t), (kvw_ref, None, kvw_t),
                            (qb_ref, None, None), (kvb_ref, None, kvb_t),
                            (gm_ref, jax.nn.sigmoid, None), (ga_ref, jax.nn.sigmoid, None)):
        width = ref.shape[1]
        y = jnp.dot(xb, w_ref[:, col:col + width], preferred_element_type=f32)
        ref[...] = y if act is None else act(y)
        if t_ref is not None:
            t_ref[0] = y.T
        col += width


def _project(x2d, w_packed, d_model, seq=None):
    n = x2d.shape[0]
    tm = 256 if n % 256 == 0 else n
    widths = (_W_QA, _W_KVN, _W_KVW, _W_QB, _W_KVB, 2 * d_model, LANES)
    n_cols = sum(widths)
    out_specs = [pl.BlockSpec((tm, w), lambda i: (i, 0)) for w in widths]
    out_shape = [jax.ShapeDtypeStruct((n, w), f32) for w in widths]
    if seq is not None:
        tiles = seq // tm
        for w in (_W_KVN, _W_KVW, _W_KVB):
            out_specs.append(pl.BlockSpec((1, w, tm), lambda i: (i // tiles, 0, i % tiles)))
            out_shape.append(jax.ShapeDtypeStruct((n // seq, w, seq), f32))
    return pl.pallas_call(
        _proj_body,
        grid=(n // tm,),
        in_specs=[pl.BlockSpec((tm, d_model), lambda i: (i, 0)),
                  pl.BlockSpec((d_model, n_cols), lambda i: (0, 0))],
        out_specs=out_specs,
        out_shape=out_shape,
        compiler_params=_vmem(2 * (tm * d_model * 4 + d_model * n_cols * 2 + 2 * tm * n_cols * 4) + (8 << 20)),
        name="proj",
    )(x2d, w_packed)


def _pack_w_in(w_in, d_model):
    c_ga = _W_QA + _W_KVN + _W_KVW
    c_qb = c_ga + _W_GA
    pad = jnp.zeros((d_model, LANES - _W_GA), w_in.dtype)
    return jnp.concatenate([w_in[:, :c_ga], w_in[:, c_qb:], w_in[:, c_ga:c_qb], pad], axis=1).astype(_MXU)


def _t5_bucket(dist):
    n = jnp.maximum(dist, 0)
    max_exact = N_BUCKETS // 2
    nf = jnp.maximum(n, 1).astype(f32)
    large = max_exact + (jnp.log(nf / max_exact) / math.log(MAX_DISTANCE / max_exact)
                         * (N_BUCKETS - max_exact)).astype(i32)
    return jnp.where(n < max_exact, n, jnp.minimum(large, N_BUCKETS - 1)).astype(i32)


def _bias_by_distance(rel_bias, max_dist):
    return rel_bias[_t5_bucket(jnp.arange(max_dist, dtype=i32))].T


def _windows(v, step, n, length):
    fine = SUBLANES
    coarse = jnp.stack([v[:, step * fine * a:step * fine * a + length + step * (fine - 1)] for a in range(n // fine)],
                       axis=1)
    out = jnp.stack([coarse[:, :, step * b:step * b + length] for b in range(fine)], axis=2)
    return out.reshape(v.shape[0], n, length)


def _bias_tiles_t(tab, n_tiles):
    h = tab.shape[0]
    tabp = jnp.concatenate([jnp.zeros((h, TQ - 1), tab.dtype), tab], axis=1)
    g = _windows(tabp, 1, TQ, n_tiles * TQ).reshape(h, TQ, n_tiles, TQ)[..., ::-1]
    return jnp.transpose(g, (2, 3, 0, 1)).reshape(n_tiles, TQ, h * TQ)


def _nsa_cmp_bias_t(tab, seq, nc):
    h = tab.shape[0]
    off = CMP_STRIDE * (nc - 1) + CMP_BLK - 1
    tabp = jnp.concatenate([jnp.zeros((h, off), tab.dtype), tab[:, :seq]], axis=1)
    w = _windows(tabp, CMP_STRIDE, nc, seq)[:, ::-1]
    return jnp.transpose(w.reshape(h, nc, seq // TQ, TQ), (2, 1, 0, 3)).reshape(seq // TQ, nc, h * TQ)


def _pack_cmp(cmp_pe, cmp_w):
    r = CMP_BLK // CMP_STRIDE
    w = cmp_w.reshape(2, r, CMP_STRIDE, HEAD_DIM, HEAD_DIM)
    eye = jnp.eye(NSA_GROUPS, dtype=w.dtype)
    wbig = jnp.einsum("cmrde,gh->cmrgdhe", w, eye).reshape(2, r, CMP_STRIDE * NSA_GROUPS * HEAD_DIM,
                                                          NSA_GROUPS * HEAD_DIM)
    pe = cmp_pe.reshape(2, r, CMP_STRIDE, 1, HEAD_DIM)
    pe = jnp.broadcast_to(pe, (2, r, CMP_STRIDE, NSA_GROUPS, HEAD_DIM)).reshape(2, r, 1, -1)
    return wbig.astype(_MXU), pe.astype(f32)


def _cmp_to_sel_matrix(n_rows, n_cols):
    spb = SEL_BLK // CMP_STRIDE
    n = jnp.arange(n_rows, dtype=i32)[:, None]
    j = jnp.arange(n_cols, dtype=i32)[None, :]
    m = jnp.zeros((n_rows, n_cols), f32)
    for k in range(CMP_BLK // CMP_STRIDE):
        m = m + ((n + k) // spb == j).astype(f32)
    return m


def _compress(xflat_ref, c, pe_ref, wbig_ref):
    x = xflat_ref[c]
    a = _dot(x + pe_ref[c, 0], wbig_ref[c, 0])
    b = _dot(x + pe_ref[c, 1], wbig_ref[c, 1])
    return a + pltpu.roll(b, x.shape[0] - 1, 0)


def _nsa_prompt_body(qa_ref, kvn_ref, kvw_ref, ga_ref, bc_ref, bt_ref, wbig_ref, pe_ref, mselt_ref, o_ref,
                     rows_s, xflat_s, kc_s, vc_s, q8_s, sel8_s, ocmp_s, osel_s, m_s, l_s, acc_s, *, seq):
    qt = pl.program_id(1)
    t0 = qt * TQ
    n_seg = seq // CMP_STRIDE
    n_cmp = n_seg - CMP_BLK // CMP_STRIDE + 1
    n_sel = -(-seq // SEL_BLK)
    nr = _round_up(n_sel, SUBLANES)
    nc = kc_s.shape[0]
    cols = NSA_HEADS * TQ
    scale = HEAD_DIM ** -0.5

    @pl.when(qt == 0)
    def _():
        for c in range(2):
            if nc > n_seg:
                xflat_s[c] = jnp.zeros(xflat_s.shape[1:], f32)
            rows_s[...] = kvn_ref[0, :, c * LANES:(c + 1) * LANES]
            for r in range(CMP_STRIDE):
                xflat_s[c, 0:n_seg, r * LANES:(r + 1) * LANES] = rows_s[pl.ds(r, n_seg, stride=CMP_STRIDE), :]
        kc_s[...] = _compress(xflat_s, 0, pe_ref, wbig_ref)
        vc_s[...] = _compress(xflat_s, 1, pe_ref, wbig_ref)
        sel8_s[...] = jnp.zeros(sel8_s.shape, f32)

    qtr = (qa_ref[...] * scale).T
    zero = jnp.zeros((HEAD_DIM, TQ), f32)
    for h in range(NSA_HEADS):
        qh = qtr[h * HEAD_DIM:(h + 1) * HEAD_DIM, :]
        blk = jnp.concatenate([qh, zero] if h // NSA_HPG == 0 else [zero, qh], axis=0)
        q8_s[:, h * TQ:(h + 1) * TQ] = blk.astype(q8_s.dtype)
    q8 = q8_s[...]
    tpos = t0 + (_iota((1, cols), 1) & (TQ - 1))

    nrow = _iota((nc, cols), 0)
    mask_c = (nrow < n_cmp) & (tpos >= nrow * CMP_STRIDE + (CMP_BLK - 1))
    sc = jnp.where(mask_c, _dot(kc_s[...], q8) + bc_ref[0], NEG)
    e = jnp.exp(sc - jnp.max(sc, axis=0, keepdims=True))
    p_c = jnp.where(mask_c, e / jnp.sum(e, axis=0, keepdims=True), 0.0)
    ocmp_s[...] = _dot_tn(vc_s[...], p_c)

    jb = _iota((nr, TQ), 0)
    cur = (t0 + _iota((nr, TQ), 1)) >> 6
    valid = (jb <= cur) & (jb < n_sel)
    forced = valid & ((jb == 0) | (jb == cur) | (jb == cur - 1))
    for g in range(NSA_GROUPS):
        imp = p_c[:, g * NSA_HPG * TQ:(g * NSA_HPG + 1) * TQ]
        for h in range(1, NSA_HPG):
            imp = imp + p_c[:, (g * NSA_HPG + h) * TQ:(g * NSA_HPG + h + 1) * TQ]
        blk = _dot_hi(mselt_ref[...], imp)[0:nr, :]
        score = jnp.where(forced, jnp.inf, jnp.where(valid, blk, -jnp.inf))
        sel = (valid & (_rank_rows(score, n_sel) < N_SEL)).astype(f32)
        for h in range(NSA_HPG):
            sel8_s[0:nr, (g * NSA_HPG + h) * TQ:(g * NSA_HPG + h + 1) * TQ] = sel

    krow = _iota((TQ, cols), 0)
    blocks_per_tile = TQ // SEL_BLK

    causal = tpos >= t0 + krow

    def bias_rows(kt, n):
        return jnp.concatenate([bt_ref[qt - kt - j] for j in range(n)], axis=0) if n > 1 else bt_ref[qt - kt]

    def sel_step(kt, n, diagonal):
        k = kvn_ref[0, pl.ds(kt * TQ, n * TQ), 2 * LANES:3 * LANES]
        v = kvn_ref[0, pl.ds(kt * TQ, n * TQ), 3 * LANES:4 * LANES]
        chosen = jnp.concatenate(
            [jnp.broadcast_to(sel8_s[pl.ds(kt * blocks_per_tile + i, 1), :], (SEL_BLK, cols))
             for i in range(n * blocks_per_tile)], axis=0)
        mask = (chosen > 0.5) & causal if diagonal else chosen > 0.5
        _softmax_step_t(_dot(k, q8) + bias_rows(kt, n), mask, v, m_s, l_s, acc_s)

    def run_tiles(lo, hi, step):
        count = hi - lo
        lax.fori_loop(0, count // 2, lambda i, c: (step(lo + 2 * i, 2), c)[1], 0)

        @pl.when(count % 2 == 1)
        def _():
            step(hi - 1, 1)

    _softmax_reset(m_s, l_s, acc_s)
    run_tiles(0, qt, lambda kt, n: sel_step(kt, n, False))
    sel_step(qt, 1, True)
    osel_s[...] = acc_s[...] / l_s[...]

    w_tiles = WINDOW // TQ

    def win_step(kt, n, mask):
        k = kvw_ref[0, pl.ds(kt * TQ, n * TQ), 0:LANES]
        v = kvw_ref[0, pl.ds(kt * TQ, n * TQ), LANES:2 * LANES]
        _softmax_step_t(_dot(k, q8) + bias_rows(kt, n), mask, v, m_s, l_s, acc_s)

    _softmax_reset(m_s, l_s, acc_s)

    @pl.when(qt >= w_tiles)
    def _():
        first = qt - w_tiles
        win_step(first, 1, tpos - (first * TQ + krow) <= WINDOW)

    run_tiles(jnp.maximum(qt - w_tiles + 1, 0), qt, lambda kt, n: win_step(kt, n, None))
    win_step(qt, 1, causal)
    owin = acc_s[...] / l_s[...]

    gat = ga_ref[...].T
    pieces = []
    for h in range(NSA_HEADS):
        g = h // NSA_HPG
        c0, c1 = h * TQ, (h + 1) * TQ
        o = (gat[3 * h:3 * h + 1, :] * ocmp_s[:, c0:c1] + gat[3 * h + 1:3 * h + 2, :] * osel_s[:, c0:c1]
             + gat[3 * h + 2:3 * h + 3, :] * owin[:, c0:c1])
        pieces.append(o[g * HEAD_DIM:(g + 1) * HEAD_DIM, :])
    o_ref[...] = jnp.concatenate(pieces, axis=0).T


def _nsa_prompt(qa, kvn, kvw, ga, bias_c, bias_t, wbig, pe, batch, seq):
    nq = seq // TQ
    n_seg = seq // CMP_STRIDE
    nc = max(LANES, n_seg)
    cols = NSA_HEADS * TQ
    mselt = _cmp_to_sel_matrix(nc, LANES).T
    seg_w = CMP_STRIDE * NSA_GROUPS * HEAD_DIM
    vm = (2 * (seq * 768 * 4 + bias_t.size * 4 + wbig.size * 2 + cols * nc * 4) + 2 * nc * seg_w * 4
          + seq * LANES * 4 + 8 * cols * LANES * 4 + (12 << 20))
    return pl.pallas_call(
        functools.partial(_nsa_prompt_body, seq=seq),
        grid=(batch, nq),
        in_specs=[pl.BlockSpec((TQ, qa.shape[1]), lambda b, q: (b * nq + q, 0)),
                  pl.BlockSpec((1, seq, kvn.shape[2]), lambda b, q: (b, 0, 0)),
                  pl.BlockSpec((1, seq, kvw.shape[2]), lambda b, q: (b, 0, 0)),
                  pl.BlockSpec((TQ, LANES), lambda b, q: (b * nq + q, 0)),
                  pl.BlockSpec((1, nc, cols), lambda b, q: (q, 0, 0)),
                  pl.BlockSpec(bias_t.shape, lambda b, q: (0, 0, 0)),
                  pl.BlockSpec(wbig.shape, lambda b, q: (0, 0, 0, 0)),
                  pl.BlockSpec(pe.shape, lambda b, q: (0, 0, 0, 0)),
                  pl.BlockSpec(mselt.shape, lambda b, q: (0, 0))],
        out_specs=pl.BlockSpec((TQ, qa.shape[1]), lambda b, q: (b * nq + q, 0)),
        out_shape=jax.ShapeDtypeStruct(qa.shape, f32),
        scratch_shapes=[pltpu.VMEM((seq, LANES), f32), pltpu.VMEM((2, nc, seg_w), f32),
                        pltpu.VMEM((nc, LANES), f32), pltpu.VMEM((nc, LANES), f32),
                        pltpu.VMEM((LANES, cols), _MXU), pltpu.VMEM((LANES, cols), f32),
                        pltpu.VMEM((LANES, cols), f32), pltpu.VMEM((LANES, cols), f32),
                        pltpu.VMEM((1, cols), f32), pltpu.VMEM((1, cols), f32), pltpu.VMEM((LANES, cols), f32)],
        compiler_params=_vmem(vm),
        name="nsa_prompt",
    )(qa, kvn, kvw, ga, bias_c, bias_t, wbig, pe, mselt)


def _moba_prompt_body(qb_ref, kvb_ref, bt_ref, o_ref, kmean_s, q8_s, sel8_s, m_s, l_s, acc_s, *, seq):
    qt = pl.program_id(1)
    t0 = qt * TQ
    nb = seq // MOBA_BLK
    nr = _round_up(nb, SUBLANES)
    cols = MOBA_HEADS * TQ
    kw = MOBA_HEADS * HEAD_DIM
    scale = HEAD_DIM ** -0.5
    pairs = MOBA_HEADS // 2

    @pl.when(qt == 0)
    def _():
        kmean_s[...] = jnp.zeros(kmean_s.shape, f32)
        for n in range(nb):
            blk = kvb_ref[0, n * MOBA_BLK:(n + 1) * MOBA_BLK, 0:kw]
            kmean_s[n:n + 1, :] = jnp.sum(blk, axis=0, keepdims=True) * (1.0 / MOBA_BLK)
        sel8_s[...] = jnp.zeros(sel8_s.shape, f32)

    qtr = qb_ref[...].T
    zero = jnp.zeros((HEAD_DIM, TQ), f32)
    nblk = _iota((nr, TQ), 0)
    own = (t0 + _iota((nr, TQ), 1)) >> 8
    past = nblk < own
    for h in range(MOBA_HEADS):
        qh = qtr[h * HEAD_DIM:(h + 1) * HEAD_DIM, :]
        blk = jnp.concatenate([qh, zero] if h % 2 == 0 else [zero, qh], axis=0)
        q8_s[:, h * TQ:(h + 1) * TQ] = (blk * scale).astype(q8_s.dtype)
        gate = _dot_hi(kmean_s[:, (h // 2) * LANES:(h // 2 + 1) * LANES], blk)[0:nr, :]
        score = jnp.where(past, gate, -jnp.inf)
        sel = (past & (_rank_rows(score, nb) < MOBA_TOPK)) | (nblk == own)
        sel8_s[0:nr, h * TQ:(h + 1) * TQ] = sel.astype(f32)

    tpos = t0 + (_iota((1, cols), 1) & (TQ - 1))
    krow = _iota((TQ, cols), 0)

    def step(kt, n, diagonal):
        s_parts = []
        for pr in range(pairs):
            k = kvb_ref[0, pl.ds(kt * TQ, n * TQ), pr * LANES:(pr + 1) * LANES]
            s_parts.append(_dot(k, q8_s[:, 2 * pr * TQ:2 * (pr + 1) * TQ]))
        bias = jnp.concatenate([bt_ref[qt - kt - j] for j in range(n)], axis=0) if n > 1 else bt_ref[qt - kt]
        s = jnp.concatenate(s_parts, axis=1) + bias
        mask = sel8_s[pl.ds((kt * TQ) >> 8, 1), :] > 0.5
        if diagonal:
            mask = mask & (tpos >= t0 + krow)
        s = jnp.where(mask, s, NEG)
        m_old = m_s[...]
        m_new = jnp.maximum(m_old, jnp.max(s, axis=0, keepdims=True))
        alpha = jnp.exp(m_old - m_new)
        p = jnp.exp(s - m_new)
        l_s[...] = alpha * l_s[...] + jnp.sum(p, axis=0, keepdims=True)
        pv = []
        for pr in range(pairs):
            v = kvb_ref[0, pl.ds(kt * TQ, n * TQ), kw + pr * LANES:kw + (pr + 1) * LANES]
            pv.append(_dot_tn(v, p[:, 2 * pr * TQ:2 * (pr + 1) * TQ]))
        acc_s[...] = alpha * acc_s[...] + jnp.concatenate(pv, axis=1)
        m_s[...] = m_new

    _softmax_reset(m_s, l_s, acc_s)
    lax.fori_loop(0, qt // 2, lambda i, c: (step(2 * i, 2, False), c)[1], 0)

    @pl.when(qt % 2 == 1)
    def _():
        step(qt - 1, 1, False)

    step(qt, 1, True)
    o = acc_s[...] / l_s[...]
    pieces = []
    for h in range(MOBA_HEADS):
        half = h % 2
        pieces.append(o[half * HEAD_DIM:(half + 1) * HEAD_DIM, h * TQ:(h + 1) * TQ])
    o_ref[...] = jnp.concatenate(pieces, axis=0).T


def _moba_prompt(qb, kvb, bias_t, batch, seq):
    nq = seq // TQ
    cols = MOBA_HEADS * TQ
    vm = 2 * (seq * kvb.shape[2] * 4 + bias_t.size * 4) + 8 * cols * LANES * 4 + (12 << 20)
    return pl.pallas_call(
        functools.partial(_moba_prompt_body, seq=seq),
        grid=(batch, nq),
        in_specs=[pl.BlockSpec((TQ, qb.shape[1]), lambda b, q: (b * nq + q, 0)),
                  pl.BlockSpec((1, seq, kvb.shape[2]), lambda b, q: (b, 0, 0)),
                  pl.BlockSpec(bias_t.shape, lambda b, q: (0, 0, 0))],
        out_specs=pl.BlockSpec((TQ, qb.shape[1]), lambda b, q: (b * nq + q, 0)),
        out_shape=jax.ShapeDtypeStruct(qb.shape, f32),
        scratch_shapes=[pltpu.VMEM((LANES, MOBA_HEADS * HEAD_DIM), f32),
                        pltpu.VMEM((LANES, cols), _MXU), pltpu.VMEM((LANES, cols), f32),
                        pltpu.VMEM((1, cols), f32), pltpu.VMEM((1, cols), f32), pltpu.VMEM((LANES, cols), f32)],
        compiler_params=_vmem(vm),
        name="moba_prompt",
    )(qb, kvb, bias_t)


def _page_group(n_pages):
    for g in (8, 4, 2):
        if n_pages % g == 0:
            return g
    raise ValueError("the page count must be even")


def _stack_heads_nsa(q, q8_s):
    half = _iota((1, LANES), 1) >> 6
    for h in range(NSA_HEADS):
        g = h // NSA_HPG
        blk = q[:, (h // 2) * LANES:(h // 2 + 1) * LANES]
        if h % 2 != g:
            blk = pltpu.roll(blk, HEAD_DIM, 1)
        q8_s[h:h + 1, :] = jnp.where(half == g, blk, 0.0).astype(q8_s.dtype)


def _softmax_step_vt(s, mask, vt, m_s, l_s, acc_s):
    s = jnp.where(mask, s, NEG)
    m_old = m_s[...]
    m_new = jnp.maximum(m_old, jnp.max(s, axis=1, keepdims=True))
    alpha = jnp.exp(m_old - m_new)
    p = jnp.where(mask, jnp.exp(s - m_new), 0.0)
    l_s[...] = alpha * l_s[...] + jnp.sum(p, axis=1, keepdims=True)
    acc_s[...] = alpha * acc_s[...] + _dot_nt(p, vt)
    m_s[...] = m_new


def _nsa_sample_body(pt_ref, *refs, n_pages, group):
    del pt_ref
    page_refs = refs[:group]
    (qa_ref, kvn_ref, kvw_ref, st_ref, ga_ref, bc_ref, bs_ref, bn_ref, bw_ref, wbig_ref, pe_ref, msel_ref, o_ref,
     rows_s, xflat_s, kc_s, vc_s, q8_s, sel8_s, ocmp_s, m_s, l_s, acc_s) = refs[group:]
    ph = pl.program_id(1)
    p = pl.program_id(2)
    n_steps = n_pages // group
    t = n_pages * PAGE
    seg_per_page = PAGE // CMP_STRIDE
    n_seg = n_pages * seg_per_page
    n_cmp = n_seg - CMP_BLK // CMP_STRIDE + 1
    cur = t // SEL_BLK
    n_sel = cur + 1
    nc = kc_s.shape[0]
    ls = sel8_s.shape[1]
    keys = group * PAGE
    gd = NSA_GROUPS * HEAD_DIM
    scale = HEAD_DIM ** -0.5

    def one_key(q8, row):
        return _dot_nt(q8, jnp.broadcast_to(row, (SUBLANES, LANES)))[:, 0:1]

    @pl.when(ph == 0)
    def _():
        if nc > n_seg:
            @pl.when(p == 0)
            def _():
                xflat_s[...] = jnp.zeros(xflat_s.shape, f32)
        for c in range(2):
            for g, ref in enumerate(page_refs):
                rows_s[...] = ref[0, c].reshape(gd, PAGE).T
                row0 = pl.multiple_of((p * group + g) * seg_per_page, SUBLANES)
                for r in range(CMP_STRIDE):
                    xflat_s[c, pl.ds(row0, seg_per_page), r * LANES:(r + 1) * LANES] = (
                        rows_s[pl.ds(r, seg_per_page, stride=CMP_STRIDE), :])

        @pl.when(p == n_steps - 1)
        def _():
            kc_s[...] = _compress(xflat_s, 0, pe_ref, wbig_ref)
            vc_s[...] = _compress(xflat_s, 1, pe_ref, wbig_ref)
            _stack_heads_nsa(qa_ref[0] * scale, q8_s)
            q8 = q8_s[...]
            ncol = _iota((NSA_HEADS, nc), 1)
            mask_c = ncol < n_cmp
            sc = jnp.where(mask_c, _dot_nt(q8, kc_s[...]) + bc_ref[...], NEG)
            e = jnp.exp(sc - jnp.max(sc, axis=1, keepdims=True))
            p_c = jnp.where(mask_c, e / jnp.sum(e, axis=1, keepdims=True), 0.0)
            ocmp_s[...] = _dot(p_c, vc_s[...])
            same_group = ((_iota((NSA_HEADS, NSA_HEADS), 0) >> 2) == (_iota((NSA_HEADS, NSA_HEADS), 1) >> 2)).astype(f32)
            blk = _dot_hi(_dot_hi(same_group, p_c), msel_ref[...])
            jb = _iota((NSA_HEADS, ls), 1)
            valid = jb < n_sel
            forced = valid & ((jb == 0) | (jb == cur) | (jb == cur - 1))
            score = jnp.where(forced, jnp.inf, jnp.where(valid, blk, -jnp.inf))
            sel8_s[...] = (valid & (_rank_lanes(score, n_sel) < N_SEL)).astype(sel8_s.dtype)

    @pl.when(ph == 1)
    def _():
        q8 = q8_s[...]

        @pl.when(p == 0)
        def _():
            m_s[...] = one_key(q8, kvn_ref[0, :, 2 * LANES:3 * LANES]) + bn_ref[:, 0:1]
            l_s[...] = jnp.ones(l_s.shape, f32)
            v_new = kvn_ref[0, :, 3 * LANES:4 * LANES].astype(_MXU).astype(f32)
            acc_s[...] = jnp.broadcast_to(v_new, acc_s.shape)

        kt = jnp.concatenate([r[0, 0].reshape(gd, PAGE) for r in page_refs], axis=1)
        vt = jnp.concatenate([r[0, 1].reshape(gd, PAGE) for r in page_refs], axis=1)
        s = _dot(q8, kt) + bs_ref[0]
        jrow = _iota((ls, keys), 0)
        jcol = (p * keys + _iota((ls, keys), 1)) >> 6
        chosen = _dot(sel8_s[...], (jrow == jcol).astype(f32)) > 0.5
        _softmax_step_vt(s, chosen, vt, m_s, l_s, acc_s)

        @pl.when(p == n_steps - 1)
        def _():
            osel = acc_s[...] / l_s[...]
            win = st_ref.shape[4]
            sw = _dot(q8, st_ref[0, 0].reshape(gd, win)) + bw_ref[...]
            sn = one_key(q8, kvw_ref[0, :, 0:LANES]) + bn_ref[:, 0:1]
            mw = jnp.maximum(jnp.max(sw, axis=1, keepdims=True), sn)
            ew = jnp.exp(sw - mw)
            en = jnp.exp(sn - mw)
            v_new = kvw_ref[0, :, LANES:2 * LANES].astype(_MXU).astype(f32)
            owin = ((_dot_nt(ew, st_ref[0, 1].reshape(gd, win)) + en * v_new)
                    / (jnp.sum(ew, axis=1, keepdims=True) + en))
            ga = ga_ref[0]
            for h in range(NSA_HEADS):
                g = h // NSA_HPG
                o = (ga[:, 3 * h:3 * h + 1] * ocmp_s[h:h + 1, :] + ga[:, 3 * h + 1:3 * h + 2] * osel[h:h + 1, :]
                     + ga[:, 3 * h + 2:3 * h + 3] * owin[h:h + 1, :])
                o_ref[0, :, h * HEAD_DIM:(h + 1) * HEAD_DIM] = o[:, g * HEAD_DIM:(g + 1) * HEAD_DIM]


def _nsa_sample(page_table, cache_t, qa, kvn, kvw, state_t, ga, tab, wbig, pe):
    bs, n_pages = page_table.shape
    group = _page_group(n_pages)
    n_steps = n_pages // group
    t = n_pages * PAGE
    win = state_t.shape[4]
    n_cmp = n_pages * (PAGE // CMP_STRIDE) - CMP_BLK // CMP_STRIDE + 1
    nc = max(LANES, n_pages * (PAGE // CMP_STRIDE))
    ls = _round_up(t // SEL_BLK + 1, LANES)
    msel = _cmp_to_sel_matrix(nc, ls)
    d_last = t - (CMP_STRIDE * (n_cmp - 1) + CMP_BLK - 1)
    bias_c = tab[:, d_last:t - CMP_BLK + 2:CMP_STRIDE][:, ::-1]
    bias_c = jnp.pad(bias_c, ((0, 0), (0, nc - n_cmp)))
    bias_s = jnp.transpose(tab[:, 1:t + 1][:, ::-1].reshape(NSA_HEADS, n_steps, group * PAGE), (1, 0, 2))
    bias_n = jnp.broadcast_to(tab[:, 0:1], (NSA_HEADS, LANES))
    bias_w = tab[:, 1:win + 1][:, ::-1]
    seg_w = CMP_STRIDE * NSA_GROUPS * HEAD_DIM
    req = lambda b, ph, p, pt: (b, 0, 0)
    c2 = lambda b, ph, p, pt: (0, 0)
    c4 = lambda b, ph, p, pt: (0, 0, 0, 0)

    def page_spec(g):
        return pl.BlockSpec((1, 2, NSA_GROUPS, HEAD_DIM, PAGE),
                            lambda b, ph, p, pt: (pt[b, p * group + g], ph, 0, 0, 0))

    grid_spec = pltpu.PrefetchScalarGridSpec(
        num_scalar_prefetch=1,
        grid=(bs, 2, n_steps),
        in_specs=([page_spec(g) for g in range(group)]
                  + [pl.BlockSpec((1, 1, qa.shape[2]), req), pl.BlockSpec((1, 1, kvn.shape[2]), req),
                     pl.BlockSpec((1, 1, kvw.shape[2]), req),
                     pl.BlockSpec((1,) + state_t.shape[1:], lambda b, ph, p, pt: (b, 0, 0, 0, 0)),
                     pl.BlockSpec((1, 1, LANES), req),
                     pl.BlockSpec(bias_c.shape, c2),
                     pl.BlockSpec((1, NSA_HEADS, group * PAGE), lambda b, ph, p, pt: (p * ph, 0, 0)),
                     pl.BlockSpec(bias_n.shape, c2), pl.BlockSpec(bias_w.shape, c2),
                     pl.BlockSpec(wbig.shape, c4), pl.BlockSpec(pe.shape, c4), pl.BlockSpec(msel.shape, c2)]),
        out_specs=pl.BlockSpec((1, 1, qa.shape[2]), req),
        scratch_shapes=[pltpu.VMEM((PAGE, LANES), f32),
                        pltpu.VMEM((2, nc, seg_w), f32), pltpu.VMEM((nc, LANES), f32), pltpu.VMEM((nc, LANES), f32),
                        pltpu.VMEM((NSA_HEADS, LANES), _MXU), pltpu.VMEM((NSA_HEADS, ls), _MXU),
                        pltpu.VMEM((NSA_HEADS, LANES), f32),
                        pltpu.VMEM((NSA_HEADS, 1), f32), pltpu.VMEM((NSA_HEADS, 1), f32),
                        pltpu.VMEM((NSA_HEADS, LANES), f32)])
    vm = (2 * nc * seg_w * 4 + 2 * (wbig.size * 2 + msel.size * 4 + win * 256 * 4 + 2 * group * PAGE * LANES * 4)
          + (16 << 20))
    return pl.pallas_call(
        functools.partial(_nsa_sample_body, n_pages=n_pages, group=group),
        grid_spec=grid_spec,
        out_shape=jax.ShapeDtypeStruct(qa.shape, f32),
        compiler_params=_vmem(vm),
        name="nsa_sample",
    )(page_table, *([cache_t] * group), qa, kvn, kvw, state_t, ga, bias_c, bias_s, bias_n, bias_w, wbig, pe, msel)


def _moba_sample_body(pt_ref, *refs, n_pages, group):
    del pt_ref
    page_refs = refs[:group]
    qb_ref, kvb_ref, bs_ref, bn_ref, o_ref, gate_s, mx_s, sum_s, acc_s = refs[group:]
    p = pl.program_id(1)
    n_steps = n_pages // group
    t = n_pages * PAGE
    kw = MOBA_HEADS * HEAD_DIM
    pages_per_blk = MOBA_BLK // PAGE
    blks_per_step = group // pages_per_blk
    own = t // MOBA_BLK
    scale = HEAD_DIM ** -0.5
    head_lane = (_iota((MOBA_HEADS, kw), 1) >> 6) == _iota((MOBA_HEADS, kw), 0)
    q8 = jnp.where(head_lane, jnp.broadcast_to(qb_ref[0], (MOBA_HEADS, kw)), 0.0)
    lane = _iota((MOBA_HEADS, LANES), 1)

    @pl.when(p == 0)
    def _():
        gate_s[...] = jnp.zeros(gate_s.shape, f32)
        mx_s[...] = jnp.full(mx_s.shape, NEG, f32)
        sum_s[...] = jnp.zeros(sum_s.shape, f32)

    for j in range(blks_per_step):
        blk = p * blks_per_step + j
        pages = page_refs[j * pages_per_blk:(j + 1) * pages_per_blk]
        kt = jnp.concatenate([r[0, 0].reshape(kw, PAGE) for r in pages], axis=1)
        vt = jnp.concatenate([r[0, 1].reshape(kw, PAGE) for r in pages], axis=1)
        raw = _dot_hi(q8, kt)
        s = raw * scale + bs_ref[0, :, j * MOBA_BLK:(j + 1) * MOBA_BLK]
        m = jnp.max(s, axis=1, keepdims=True)
        pr = jnp.exp(s - m)
        here = lane == blk
        gate_s[...] = jnp.where(here, jnp.sum(raw, axis=1, keepdims=True) * (1.0 / MOBA_BLK), gate_s[...])
        mx_s[...] = jnp.where(here, m, mx_s[...])
        sum_s[...] = jnp.where(here, jnp.sum(pr, axis=1, keepdims=True), sum_s[...])
        acc_s[blk] = _dot_nt(pr, vt)

    @pl.when(p == n_steps - 1)
    def _():
        past = lane < own
        score = jnp.where(past, gate_s[...], -jnp.inf)
        sel = past & (_rank_lanes(score, own) < MOBA_TOPK)
        s_new = jnp.sum(q8 * kvb_ref[0, :, 0:kw], axis=1, keepdims=True) * scale + bn_ref[:, 0:1]
        v_new = kvb_ref[0, :, kw:2 * kw].astype(_MXU).astype(f32)
        top = jnp.maximum(jnp.max(jnp.where(sel, mx_s[...], NEG), axis=1, keepdims=True), s_new)
        wts = jnp.where(sel, jnp.exp(mx_s[...] - top), 0.0)
        w_new = jnp.exp(s_new - top)
        den = jnp.sum(wts * sum_s[...], axis=1, keepdims=True) + w_new
        num = w_new * v_new
        for j in range(own):
            num = num + wts[:, j:j + 1] * acc_s[j]
        o8 = jnp.where(head_lane, num / den, 0.0)
        o_ref[0] = jnp.sum(o8, axis=0, keepdims=True)


def _moba_sample(page_table, cache_t, qb, kvb, tab):
    bs, n_pages = page_table.shape
    group = _page_group(n_pages)
    n_steps = n_pages // group
    t = n_pages * PAGE
    kw = MOBA_HEADS * HEAD_DIM
    n_blk = t // MOBA_BLK
    assert n_blk <= LANES
    bias_s = jnp.transpose(tab[:, 1:t + 1][:, ::-1].reshape(MOBA_HEADS, n_steps, group * PAGE), (1, 0, 2))
    bias_n = jnp.broadcast_to(tab[:, 0:1], (MOBA_HEADS, LANES))
    req = lambda b, p, pt: (b, 0, 0)

    def page_spec(g):
        return pl.BlockSpec((1, 2, MOBA_HEADS, HEAD_DIM, PAGE), lambda b, p, pt: (pt[b, p * group + g], 0, 0, 0, 0))

    grid_spec = pltpu.PrefetchScalarGridSpec(
        num_scalar_prefetch=1,
        grid=(bs, n_steps),
        in_specs=([page_spec(g) for g in range(group)]
                  + [pl.BlockSpec((1, 1, kw), req), pl.BlockSpec((1, 1, 2 * kw), req),
                     pl.BlockSpec((1, MOBA_HEADS, group * PAGE), lambda b, p, pt: (p, 0, 0)),
                     pl.BlockSpec(bias_n.shape, lambda b, p, pt: (0, 0))]),
        out_specs=pl.BlockSpec((1, 1, kw), req),
        scratch_shapes=[pltpu.VMEM((MOBA_HEADS, LANES), f32), pltpu.VMEM((MOBA_HEADS, LANES), f32),
                        pltpu.VMEM((MOBA_HEADS, LANES), f32), pltpu.VMEM((n_blk, MOBA_HEADS, kw), f32)])
    return pl.pallas_call(
        functools.partial(_moba_sample_body, n_pages=n_pages, group=group),
        grid_spec=grid_spec,
        out_shape=jax.ShapeDtypeStruct(qb.shape, f32),
        compiler_params=_vmem(2 * group * PAGE * 2 * kw * 4 + (24 << 20)),
        name="moba_sample",
    )(page_table, *([cache_t] * group), qb, kvb, bias_s, bias_n)


def _layer_norm(x, g, b):
    mu = jnp.mean(x, axis=-1, keepdims=True)
    xc = x - mu
    var = jnp.mean(xc * xc, axis=-1, keepdims=True)
    return xc * lax.rsqrt(var + LN_EPS) * g + b


def _merge_body(oa_ref, ob_ref, gm_ref, x_ref, wa_ref, wb_ref, wo_ref, g_ref, b_ref, h_ref, ht_ref, *, alpha):
    d = x_ref.shape[1]
    ya = _dot(oa_ref[...], wa_ref[...])
    yb = _dot(ob_ref[...], wb_ref[...])
    gm = gm_ref[...]
    mix = _dot(gm[:, :d] * ya + gm[:, d:] * yb, wo_ref[...])
    h = _layer_norm(alpha * x_ref[...] + mix, g_ref[...], b_ref[...])
    h_ref[...] = h
    ht_ref[...] = h.T.astype(ht_ref.dtype)


def _merge(oa, ob, gm, x2d, wa, wb, wo, g, b, alpha):
    n, d = x2d.shape
    tm = 256 if n % 256 == 0 else n
    row = lambda i: (i, 0)
    full = lambda i: (0, 0)
    vm = 2 * (tm * (oa.shape[1] + ob.shape[1] + gm.shape[1] + 2 * d) * 4 + tm * d * 2
              + (wa.size + wb.size + wo.size) * 2) + (16 << 20)
    return pl.pallas_call(
        functools.partial(_merge_body, alpha=alpha),
        grid=(n // tm,),
        in_specs=[pl.BlockSpec((tm, oa.shape[1]), row), pl.BlockSpec((tm, ob.shape[1]), row),
                  pl.BlockSpec((tm, gm.shape[1]), row), pl.BlockSpec((tm, d), row),
                  pl.BlockSpec(wa.shape, full), pl.BlockSpec(wb.shape, full), pl.BlockSpec(wo.shape, full),
                  pl.BlockSpec((1, d), full), pl.BlockSpec((1, d), full)],
        out_specs=[pl.BlockSpec((tm, d), row), pl.BlockSpec((d, tm), lambda i: (0, i))],
        out_shape=[jax.ShapeDtypeStruct((n, d), f32), jax.ShapeDtypeStruct((d, n), _MXU)],
        compiler_params=_vmem(vm),
        name="merge",
    )(oa, ob, gm, x2d, wa, wb, wo, g, b)


_NOT_TOP = 99.0
_CAND_B = tuple(PEER_TOPK // (a + 1) for a in range(SUBLANES))


def _extract_top(work_s, n_rounds, on_round, one_per_round):
    rows = _iota(work_s.shape, 0)

    def body(r, carry):
        w = work_s[...]
        m = jnp.max(w, axis=0, keepdims=True)
        hit = w == m
        if one_per_round:
            hit = rows == jnp.min(jnp.where(hit, rows, work_s.shape[0]), axis=0, keepdims=True)
        on_round(r, m, hit)
        work_s[...] = jnp.where(hit, -jnp.inf, w)
        return carry

    lax.fori_loop(0, n_rounds, body, 0)


def _top_rounds(work_s, init, on_round, n_hit):
    init()
    _extract_top(work_s, PEER_TOPK, on_round, one_per_round=False)
    bad = jnp.max(jnp.abs(n_hit() - float(PEER_TOPK)))

    @pl.when(bad > 0.5)
    def _():
        init()
        _extract_top(work_s, PEER_TOPK, on_round, one_per_round=True)


def _peer_select(ht_ref, wqt_ref, keys_ref, qt_s, rank_s, exp_s, sv_s, work_s, cwork_s, cand_s, cnt_s, f0_s,
                 rk1_s, e1_s):
    tn = ht_ref.shape[1]
    nk = PEER_N_KEYS
    qt_s[...] = jnp.dot(wqt_ref[...], ht_ref[...], preferred_element_type=f32)

    def per_half(hc, carry):
        exp_s[hc] = _dot(keys_ref[hc], qt_s[pl.ds(pl.multiple_of(hc * PEER_HALF, PEER_HALF), PEER_HALF), :])

        def init():
            work_s[...] = exp_s[hc]
            rank_s[hc] = jnp.full((nk, tn), _NOT_TOP, f32)

        def on_round(r, m, hit):
            sv_s[hc, pl.ds(r, 1), :] = m
            rank_s[hc] = jnp.where(hit, jnp.asarray(r, f32), rank_s[hc])

        def n_hit():
            return jnp.sum(jnp.where(rank_s[hc] < _NOT_TOP, 1.0, 0.0), axis=0, keepdims=True)

        _top_rounds(work_s, init, on_round, n_hit)
        exp_s[hc] = jnp.exp(exp_s[hc] - sv_s[hc, 0:1, :])
        return carry

    lax.fori_loop(0, 2 * PEER_HEADS, per_half, 0)

    n_cand = cand_s.shape[0]

    def per_head(h, carry):
        sv0 = sv_s[2 * h]
        sv1 = sv_s[2 * h + 1]
        e0 = jnp.exp(sv0 - sv0[0:1, :])
        e1 = jnp.exp(sv1 - sv1[0:1, :])
        brow = _iota((SUBLANES, tn), 0)
        tiles, etiles = [sv0[0:1, :] + sv1, ], [e0[0:1, :] * e1]
        for a in range(1, SUBLANES):
            ok = brow < _CAND_B[a]
            tiles.append(jnp.where(ok, sv0[a:a + 1, :] + sv1[0:SUBLANES, :], -jnp.inf))
            etiles.append(jnp.where(ok, e0[a:a + 1, :] * e1[0:SUBLANES, :], 0.0))
        tiles.append(sv0[SUBLANES:, :] + sv1[0:1, :])
        etiles.append(e0[SUBLANES:, :] * e1[0:1, :])
        cands = jnp.concatenate(tiles, axis=0)

        def init():
            cwork_s[...] = cands
            cand_s[...] = jnp.zeros(cand_s.shape, f32)

        def on_round(r, m, hit):
            cand_s[...] = jnp.where(hit, 1.0, cand_s[...])

        _top_rounds(cwork_s, init, on_round, lambda: jnp.sum(cand_s[...], axis=0, keepdims=True))
        chosen = cand_s[...]
        z = jnp.sum(chosen * jnp.concatenate(etiles, axis=0), axis=0, keepdims=True)
        cnts = [jnp.sum(chosen[0:PEER_TOPK], axis=0, keepdims=True)]
        for a in range(1, SUBLANES):
            r0 = PEER_TOPK + (a - 1) * SUBLANES
            cnts.append(jnp.sum(chosen[r0:r0 + SUBLANES], axis=0, keepdims=True))
        tail = chosen[n_cand - SUBLANES:]
        cnts += [tail[a:a + 1, :] for a in range(SUBLANES)]
        ra = rank_s[2 * h]
        c0 = jnp.zeros((nk, tn), f32)
        for a in range(PEER_TOPK):
            c0 = jnp.where(ra == float(a), cnts[a], c0)
        cnt_s[h] = c0
        f0_s[h] = exp_s[2 * h] / z
        rk1_s[h] = rank_s[2 * h + 1].astype(rk1_s.dtype)
        e1_s[h] = exp_s[2 * h + 1].astype(e1_s.dtype)
        return carry

    lax.fori_loop(0, PEER_HEADS, per_head, 0)


def _gelu_t(u_tile, ht):
    at = jnp.dot(u_tile, ht, preferred_element_type=f32)
    return (0.5 * at * (1.0 + lax.erf(at * math.sqrt(0.5)))).astype(_MXU)


def _peer_body(ht_ref, h_ref, wqt_ref, keys_ref, u0_ref, un_ref, vt_ref, g_ref, b_ref, y_ref,
               qt_s, rank_s, exp_s, sv_s, work_s, cwork_s, cand_s, cnt_s, f0_s, rk1_s, e1_s, gel_s, wc_s, acc_s,
               *, alpha, ti):
    e = pl.program_id(1)
    last = pl.num_programs(1) - 1
    tn = ht_ref.shape[1]

    @pl.when(e == 0)
    def _():
        _peer_select(ht_ref, wqt_ref, keys_ref, qt_s, rank_s, exp_s, sv_s, work_s, cwork_s, cand_s, cnt_s, f0_s,
                     rk1_s, e1_s)
        acc_s[...] = jnp.zeros(acc_s.shape, f32)
        gel_s[0] = _gelu_t(u0_ref[...], ht_ref[...])

    for ii in range(ti):
        i = e * ti + ii
        r0, r1 = ii * PEER_N_KEYS, (ii + 1) * PEER_N_KEYS
        w = jnp.zeros((PEER_N_KEYS, tn), _MXU)
        for h in range(PEER_HEADS):
            cnt = cnt_s[h, pl.ds(i, 1), :].astype(_MXU)
            f0 = f0_s[h, pl.ds(i, 1), :].astype(_MXU)
            w = w + jnp.where(rk1_s[h] < cnt, e1_s[h], jnp.zeros((), _MXU)) * f0
        wc_s[r0:r1, :] = w * gel_s[e % 2, r0:r1, :]
        if ii % 2 == 1:
            k0, k1 = (ii - 1) * PEER_N_KEYS, (ii + 1) * PEER_N_KEYS
            acc_s[...] += jnp.dot(vt_ref[:, k0:k1], wc_s[k0:k1, :], preferred_element_type=f32)
    gel_s[(e + 1) % 2] = _gelu_t(un_ref[...], ht_ref[...])

    @pl.when(e == last)
    def _():
        y_ref[...] = _layer_norm(alpha * h_ref[...] + acc_s[...].T, g_ref[...], b_ref[...])


def _peer(ht, h, wqt, keys, u, vt, g, b, alpha):
    d, n = ht.shape
    n_exp = u.shape[0]
    tn = 512 if n % 512 == 0 else n
    ti = 8
    te = ti * PEER_N_KEYS
    n_half = 2 * PEER_HEADS
    n_cand = PEER_TOPK + (SUBLANES - 1) * SUBLANES + SUBLANES
    tok = lambda t, e: (0, t)
    full2 = lambda t, e: (0, 0)
    scratch = [pltpu.VMEM((wqt.shape[0], tn), f32),
               pltpu.VMEM((n_half, PEER_N_KEYS, tn), f32), pltpu.VMEM((n_half, PEER_N_KEYS, tn), f32),
               pltpu.VMEM((n_half, PEER_TOPK, tn), f32), pltpu.VMEM((PEER_N_KEYS, tn), f32),
               pltpu.VMEM((n_cand, tn), f32), pltpu.VMEM((n_cand, tn), f32),
               pltpu.VMEM((PEER_HEADS, PEER_N_KEYS, tn), f32), pltpu.VMEM((PEER_HEADS, PEER_N_KEYS, tn), f32),
               pltpu.VMEM((PEER_HEADS, PEER_N_KEYS, tn), _MXU), pltpu.VMEM((PEER_HEADS, PEER_N_KEYS, tn), _MXU),
               pltpu.VMEM((2, te, tn), _MXU), pltpu.VMEM((te, tn), _MXU), pltpu.VMEM((d, tn), f32)]
    vm = (wqt.shape[0] * tn * 4 + 2 * n_half * PEER_N_KEYS * tn * 4 + 2 * PEER_HEADS * PEER_N_KEYS * tn * 6
          + 3 * te * tn * 2 + d * tn * 4
          + 2 * (d * tn * 2 + 2 * tn * d * 4 + wqt.size * 2 + keys.size * 2 + 3 * te * d * 2) + (14 << 20))
    n_tiles = n_exp // te
    return pl.pallas_call(
        functools.partial(_peer_body, alpha=alpha, ti=ti),
        grid=(n // tn, n_tiles),
        in_specs=[pl.BlockSpec((d, tn), tok),
                  pl.BlockSpec((tn, d), lambda t, e: (t, 0)),
                  pl.BlockSpec(wqt.shape, full2),
                  pl.BlockSpec(keys.shape, lambda t, e: (0, 0, 0)),
                  pl.BlockSpec((te, d), full2),
                  pl.BlockSpec((te, d), lambda t, e: (jnp.minimum(e + 1, n_tiles - 1), 0)),
                  pl.BlockSpec((d, te), lambda t, e: (0, e)),
                  pl.BlockSpec((1, d), full2), pl.BlockSpec((1, d), full2)],
        out_specs=pl.BlockSpec((tn, d), lambda t, e: (t, 0)),
        out_shape=jax.ShapeDtypeStruct((n, d), f32),
        scratch_shapes=scratch,
        compiler_params=_vmem(vm),
        name="peer",
    )(ht, h, wqt, keys, u, u, vt, g, b)


def kernel(x_prompt, x_sample, cache_nsa, cache_moba, state_nsa_win, page_table, w_in, nsa_cmp_pe, nsa_cmp_w, w_branch_a, w_branch_b, w_out, ln1_g, ln1_b, peer_wq, peer_keys, peer_u, peer_v, ln2_g, ln2_b, rel_bias):
    depth = w_in.shape[0]
    batch, seq, d = x_prompt.shape
    bs, dec_seq, _ = x_sample.shape
    n_pages = page_table.shape[1]
    past = n_pages * PAGE
    win_len = state_nsa_win.shape[2]
    assert dec_seq == 1 and cache_nsa.shape[2] == PAGE and cache_moba.shape[2] == PAGE
    assert seq % MOBA_BLK == 0 and past % MOBA_BLK == 0 and win_len == WINDOW and seq % TQ == 0
    alpha = (2 * depth) ** 0.25
    nq = seq // TQ
    n_tok = batch * seq

    tab = _bias_by_distance(rel_bias, max(seq, past + 1) + TQ)
    tab_a, tab_b = tab[:NSA_HEADS], tab[NSA_HEADS:]
    tiles_a, tiles_b = _bias_tiles_t(tab_a, nq), _bias_tiles_t(tab_b, nq)
    cmp_bias = _nsa_cmp_bias_t(tab_a, seq, max(LANES, seq // CMP_STRIDE))

    xp = x_prompt.reshape(n_tok, d)
    xs = x_sample.reshape(bs, d)
    outs = [[] for _ in range(6)]
    for l in range(depth):
        w_packed = _pack_w_in(w_in[l], d)
        wbig, pe = _pack_cmp(nsa_cmp_pe[l], nsa_cmp_w[l])
        wa, wb, wo = w_branch_a[l].astype(_MXU), w_branch_b[l].astype(_MXU), w_out[l].astype(_MXU)
        wqt = peer_wq[l].T.astype(_MXU)
        keys = peer_keys[l].reshape(2 * PEER_HEADS, PEER_N_KEYS, PEER_HALF).astype(_MXU)
        u = peer_u[l].astype(_MXU)
        vt = peer_v[l].T.astype(_MXU)
        g1, b1, g2, b2 = ln1_g[l][None], ln1_b[l][None], ln2_g[l][None], ln2_b[l][None]

        def ffn(oa, ob, gm, x2d):
            h, ht = _merge(oa, ob, gm, x2d, wa, wb, wo, g1, b1, alpha)
            return _peer(ht, h, wqt, keys, u, vt, g2, b2, alpha)

        qa, kvn, kvw, qb, kvb, gm, ga, kvn_t, kvw_t, kvb_t = _project(xp, w_packed, d, seq)
        oa = _nsa_prompt(qa, kvn.reshape(batch, seq, -1), kvw.reshape(batch, seq, -1), ga, cmp_bias, tiles_a,
                         wbig, pe, batch, seq)
        ob = _moba_prompt(qb, kvb.reshape(batch, seq, -1), tiles_b, batch, seq)
        xp = ffn(oa, ob, gm, xp)
        to_row_major = lambda a, c, g: jnp.transpose(a.reshape(batch, c, g, HEAD_DIM, -1), (0, 4, 1, 2, 3))
        outs[0].append(to_row_major(kvn_t, 4, NSA_GROUPS))
        outs[1].append(to_row_major(kvb_t, 2, MOBA_HEADS))
        outs[2].append(to_row_major(kvw_t[:, :, seq - min(WINDOW, seq):], 2, NSA_GROUPS))

        qa, kvn, kvw, qb, kvb, gm, ga = _project(xs, w_packed, d)
        r3 = lambda a: a.reshape(bs, 1, -1)
        to_feature_major = lambda a: jnp.transpose(a, (0, 2, 3, 4, 1))
        oa = _nsa_sample(page_table, to_feature_major(cache_nsa[l]), r3(qa), r3(kvn), r3(kvw),
                         to_feature_major(state_nsa_win[l]), r3(ga), tab_a, wbig, pe)
        ob = _moba_sample(page_table, to_feature_major(cache_moba[l]), r3(qb), r3(kvb), tab_b)
        xs = ffn(oa.reshape(bs, -1), ob.reshape(bs, -1), gm, xs)
        outs[3].append(kvn.reshape(bs, 1, 4, NSA_GROUPS, HEAD_DIM))
        outs[4].append(kvb.reshape(bs, 1, 2, MOBA_HEADS, HEAD_DIM))
        win_new = jnp.concatenate([state_nsa_win[l][:, 1:], kvw.reshape(bs, 1, 2, NSA_GROUPS, HEAD_DIM)], axis=1)
        outs[5].append(win_new)

    stacked = [jnp.stack(o) for o in outs]
    return (xp.reshape(batch, seq, d), xs.reshape(bs, 1, d), *stacked)
```

```python
import functools
import math

import jax
import jax.numpy as jnp
from jax import lax
from jax.experimental import pallas as pl
from jax.experimental.pallas import tpu as pltpu

HEAD_DIM = 64
NSA_HEADS = 8
NSA_GROUPS = 2
NSA_HPG = NSA_HEADS // NSA_GROUPS
CMP_BLK = 32
CMP_STRIDE = 16
SEL_BLK = 64
N_SEL = 16
WINDOW = 512
MOBA_HEADS = 8
MOBA_BLK = 256
MOBA_TOPK = 3
N_BUCKETS = 32
MAX_DISTANCE = 1024
PEER_HEADS = 8
PEER_N_KEYS = 128
PEER_HALF = 128
PEER_TOPK = 16
LN_EPS = 1e-5
NEG = -1e30

LANES = 128
SUBLANES = 8
VMEM_BYTES_V7X = 64 * 1024 * 1024

TQ = 128
PAGE = 128

f32 = jnp.float32
i32 = jnp.int32
_MXU = jnp.bfloat16
_HI = lax.Precision.HIGHEST


def _iota(shape, dim):
    return lax.broadcasted_iota(i32, shape, dim)


def _dot(a, b):
    return jnp.dot(a.astype(_MXU), b.astype(_MXU), preferred_element_type=f32)


def _dot_nt(a, b):
    return lax.dot_general(a.astype(_MXU), b.astype(_MXU), (((1,), (1,)), ((), ())),
                           preferred_element_type=f32)


def _dot_tn(a, b):
    return lax.dot_general(a.astype(_MXU), b.astype(_MXU), (((0,), (0,)), ((), ())), preferred_element_type=f32)


def _dot_hi(a, b):
    return jnp.dot(a, b, precision=_HI, preferred_element_type=f32)


def _vmem(nbytes, flags=None):
    return pltpu.CompilerParams(vmem_limit_bytes=int(min(nbytes, VMEM_BYTES_V7X - (4 << 20))), flags=flags)


def _round_up(x, m):
    return -(-x // m) * m


def _rank_lanes(s, n):
    width = s.shape[1]
    j = _iota(s.shape, 1)
    rank = jnp.zeros(s.shape, i32)
    for r in range(1, n):
        lo = pltpu.roll(s, r, 1)
        rank = rank + jnp.where((j >= r) & (lo >= s), 1, 0)
        hi = pltpu.roll(s, width - r, 1)
        rank = rank + jnp.where((j + r < n) & (hi > s), 1, 0)
    return rank


def _rank_rows(s, n):
    j = _iota(s.shape, 0)
    rank = jnp.zeros(s.shape, i32)
    for k in range(n):
        row = s[k:k + 1, :]
        rank = rank + jnp.where((row > s) | ((row == s) & (j > k)), 1, 0)
    return rank


def _softmax_reset(m_s, l_s, acc_s):
    m_s[...] = jnp.full(m_s.shape, NEG, f32)
    l_s[...] = jnp.zeros(l_s.shape, f32)
    acc_s[...] = jnp.zeros(acc_s.shape, f32)


def _softmax_step_t(s, mask, v, m_s, l_s, acc_s):
    if mask is not None:
        s = jnp.where(mask, s, NEG)
    m_old = m_s[...]
    m_new = jnp.maximum(m_old, jnp.max(s, axis=0, keepdims=True))
    alpha = jnp.exp(m_old - m_new)
    p = jnp.exp(s - m_new)
    l_s[...] = alpha * l_s[...] + jnp.sum(p, axis=0, keepdims=True)
    acc_s[...] = alpha * acc_s[...] + _dot_tn(v, p)
    m_s[...] = m_new


_W_QA = NSA_HEADS * HEAD_DIM
_W_KVN = 4 * NSA_GROUPS * HEAD_DIM
_W_KVW = 2 * NSA_GROUPS * HEAD_DIM
_W_GA = 3 * NSA_HEADS
_W_QB = MOBA_HEADS * HEAD_DIM
_W_KVB = 2 * MOBA_HEADS * HEAD_DIM


def _proj_body(x_ref, w_ref, qa_ref, kvn_ref, kvw_ref, qb_ref, kvb_ref, gm_ref, ga_ref, *kv_t_refs):
    xb = x_ref[...].astype(_MXU)
    col = 0
    kvn_t, kvw_t, kvb_t = kv_t_refs if kv_t_refs else (None, None, None)
    for ref, act, t_ref in ((qa_ref, None, None), (kvn_ref, None, kvn_t), (kvw_ref, None, kvw_t),
                            (qb_ref, None, None), (kvb_ref, None, kvb_t),
                            (gm_ref, jax.nn.sigmoid, None), (ga_ref, jax.nn.sigmoid, None)):
        width = ref.shape[1]
        y = jnp.dot(xb, w_ref[:, col:col + width], preferred_element_type=f32)
        ref[...] = y if act is None else act(y)
        if t_ref is not None:
            t_ref[0] = y.T
        col += width


def _project(x2d, w_packed, d_model, seq=None):
    n = x2d.shape[0]
    tm = 256 if n % 256 == 0 else n
    widths = (_W_QA, _W_KVN, _W_KVW, _W_QB, _W_KVB, 2 * d_model, LANES)
    n_cols = sum(widths)
    out_specs = [pl.BlockSpec((tm, w), lambda i: (i, 0)) for w in widths]
    out_shape = [jax.ShapeDtypeStruct((n, w), f32) for w in widths]
    if seq is not None:
        tiles = seq // tm
        for w in (_W_KVN, _W_KVW, _W_KVB):
            out_specs.append(pl.BlockSpec((1, w, tm), lambda i: (i // tiles, 0, i % tiles)))
            out_shape.append(jax.ShapeDtypeStruct((n // seq, w, seq), f32))
    return pl.pallas_call(
        _proj_body,
        grid=(n // tm,),
        in_specs=[pl.BlockSpec((tm, d_model), lambda i: (i, 0)),
                  pl.BlockSpec((d_model, n_cols), lambda i: (0, 0))],
        out_specs=out_specs,
        out_shape=out_shape,
        compiler_params=_vmem(2 * (tm * d_model * 4 + d_model * n_cols * 2 + 2 * tm * n_cols * 4) + (8 << 20)),
        name="proj",
    )(x2d, w_packed)


def _pack_w_in(w_in, d_model):
    c_ga = _W_QA + _W_KVN + _W_KVW
    c_qb = c_ga + _W_GA
    pad = jnp.zeros((d_model, LANES - _W_GA), w_in.dtype)
    return jnp.concatenate([w_in[:, :c_ga], w_in[:, c_qb:], w_in[:, c_ga:c_qb], pad], axis=1).astype(_MXU)


def _t5_bucket(dist):
    n = jnp.maximum(dist, 0)
    max_exact = N_BUCKETS // 2
    nf = jnp.maximum(n, 1).astype(f32)
    large = max_exact + (jnp.log(nf / max_exact) / math.log(MAX_DISTANCE / max_exact)
                         * (N_BUCKETS - max_exact)).astype(i32)
    return jnp.where(n < max_exact, n, jnp.minimum(large, N_BUCKETS - 1)).astype(i32)


def _bias_by_distance(rel_bias, max_dist):
    return rel_bias[_t5_bucket(jnp.arange(max_dist, dtype=i32))].T


def _windows(v, step, n, length):
    fine = SUBLANES
    coarse = jnp.stack([v[:, step * fine * a:step * fine * a + length + step * (fine - 1)] for a in range(n // fine)],
                       axis=1)
    out = jnp.stack([coarse[:, :, step * b:step * b + length] for b in range(fine)], axis=2)
    return out.reshape(v.shape[0], n, length)


def _bias_tiles_t(tab, n_tiles):
    h = tab.shape[0]
    tabp = jnp.concatenate([jnp.zeros((h, TQ - 1), tab.dtype), tab], axis=1)
    g = _windows(tabp, 1, TQ, n_tiles * TQ).reshape(h, TQ, n_tiles, TQ)[..., ::-1]
    return jnp.transpose(g, (2, 3, 0, 1)).reshape(n_tiles, TQ, h * TQ)


def _nsa_cmp_bias_t(tab, seq, nc):
    h = tab.shape[0]
    off = CMP_STRIDE * (nc - 1) + CMP_BLK - 1
    tabp = jnp.concatenate([jnp.zeros((h, off), tab.dtype), tab[:, :seq]], axis=1)
    w = _windows(tabp, CMP_STRIDE, nc, seq)[:, ::-1]
    return jnp.transpose(w.reshape(h, nc, seq // TQ, TQ), (2, 1, 0, 3)).reshape(seq // TQ, nc, h * TQ)


def _pack_cmp(cmp_pe, cmp_w):
    r = CMP_BLK // CMP_STRIDE
    w = cmp_w.reshape(2, r, CMP_STRIDE, HEAD_DIM, HEAD_DIM)
    eye = jnp.eye(NSA_GROUPS, dtype=w.dtype)
    wbig = jnp.einsum("cmrde,gh->cmrgdhe", w, eye).reshape(2, r, CMP_STRIDE * NSA_GROUPS * HEAD_DIM,
                                                          NSA_GROUPS * HEAD_DIM)
    pe = cmp_pe.reshape(2, r, CMP_STRIDE, 1, HEAD_DIM)
    pe = jnp.broadcast_to(pe, (2, r, CMP_STRIDE, NSA_GROUPS, HEAD_DIM)).reshape(2, r, 1, -1)
    return wbig.astype(_MXU), pe.astype(f32)


def _cmp_to_sel_matrix(n_rows, n_cols):
    spb = SEL_BLK // CMP_STRIDE
    n = jnp.arange(n_rows, dtype=i32)[:, None]
    j = jnp.arange(n_cols, dtype=i32)[None, :]
    m = jnp.zeros((n_rows, n_cols), f32)
    for k in range(CMP_BLK // CMP_STRIDE):
        m = m + ((n + k) // spb == j).astype(f32)
    return m


def _compress(xflat_ref, c, pe_ref, wbig_ref):
    x = xflat_ref[c]
    a = _dot(x + pe_ref[c, 0], wbig_ref[c, 0])
    b = _dot(x + pe_ref[c, 1], wbig_ref[c, 1])
    return a + pltpu.roll(b, x.shape[0] - 1, 0)


def _nsa_prompt_body(qa_ref, kvn_ref, kvw_ref, ga_ref, bc_ref, bt_ref, wbig_ref, pe_ref, mselt_ref, o_ref,
                     rows_s, xflat_s, kc_s, vc_s, q8_s, sel8_s, ocmp_s, osel_s, m_s, l_s, acc_s, *, seq):
    qt = pl.program_id(1)
    t0 = qt * TQ
    n_seg = seq // CMP_STRIDE
    n_cmp = n_seg - CMP_BLK // CMP_STRIDE + 1
    n_sel = -(-seq // SEL_BLK)
    nr = _round_up(n_sel, SUBLANES)
    nc = kc_s.shape[0]
    cols = NSA_HEADS * TQ
    scale = HEAD_DIM ** -0.5

    @pl.when(qt == 0)
    def _():
        for c in range(2):
            if nc > n_seg:
                xflat_s[c] = jnp.zeros(xflat_s.shape[1:], f32)
            rows_s[...] = kvn_ref[0, :, c * LANES:(c + 1) * LANES]
            for r in range(CMP_STRIDE):
                xflat_s[c, 0:n_seg, r * LANES:(r + 1) * LANES] = rows_s[pl.ds(r, n_seg, stride=CMP_STRIDE), :]
        kc_s[...] = _compress(xflat_s, 0, pe_ref, wbig_ref)
        vc_s[...] = _compress(xflat_s, 1, pe_ref, wbig_ref)
        sel8_s[...] = jnp.zeros(sel8_s.shape, f32)

    qtr = (qa_ref[...] * scale).T
    zero = jnp.zeros((HEAD_DIM, TQ), f32)
    for h in range(NSA_HEADS):
        qh = qtr[h * HEAD_DIM:(h + 1) * HEAD_DIM, :]
        blk = jnp.concatenate([qh, zero] if h // NSA_HPG == 0 else [zero, qh], axis=0)
        q8_s[:, h * TQ:(h + 1) * TQ] = blk.astype(q8_s.dtype)
    q8 = q8_s[...]
    tpos = t0 + (_iota((1, cols), 1) & (TQ - 1))

    nrow = _iota((nc, cols), 0)
    mask_c = (nrow < n_cmp) & (tpos >= nrow * CMP_STRIDE + (CMP_BLK - 1))
    sc = jnp.where(mask_c, _dot(kc_s[...], q8) + bc_ref[0], NEG)
    e = jnp.exp(sc - jnp.max(sc, axis=0, keepdims=True))
    p_c = jnp.where(mask_c, e / jnp.sum(e, axis=0, keepdims=True), 0.0)
    ocmp_s[...] = _dot_tn(vc_s[...], p_c)

    jb = _iota((nr, TQ), 0)
    cur = (t0 + _iota((nr, TQ), 1)) >> 6
    valid = (jb <= cur) & (jb < n_sel)
    forced = valid & ((jb == 0) | (jb == cur) | (jb == cur - 1))
    for g in range(NSA_GROUPS):
        imp = p_c[:, g * NSA_HPG * TQ:(g * NSA_HPG + 1) * TQ]
        for h in range(1, NSA_HPG):
            imp = imp + p_c[:, (g * NSA_HPG + h) * TQ:(g * NSA_HPG + h + 1) * TQ]
        blk = _dot_hi(mselt_ref[...], imp)[0:nr, :]
        score = jnp.where(forced, jnp.inf, jnp.where(valid, blk, -jnp.inf))
        sel = (valid & (_rank_rows(score, n_sel) < N_SEL)).astype(f32)
        for h in range(NSA_HPG):
            sel8_s[0:nr, (g * NSA_HPG + h) * TQ:(g * NSA_HPG + h + 1) * TQ] = sel

    krow = _iota((TQ, cols), 0)
    blocks_per_tile = TQ // SEL_BLK

    causal = tpos >= t0 + krow

    def bias_rows(kt, n):
        return jnp.concatenate([bt_ref[qt - kt - j] for j in range(n)], axis=0) if n > 1 else bt_ref[qt - kt]

    def sel_step(kt, n, diagonal):
        k = kvn_ref[0, pl.ds(kt * TQ, n * TQ), 2 * LANES:3 * LANES]
        v = kvn_ref[0, pl.ds(kt * TQ, n * TQ), 3 * LANES:4 * LANES]
        chosen = jnp.concatenate(
            [jnp.broadcast_to(sel8_s[pl.ds(kt * blocks_per_tile + i, 1), :], (SEL_BLK, cols))
             for i in range(n * blocks_per_tile)], axis=0)
        mask = (chosen > 0.5) & causal if diagonal else chosen > 0.5
        _softmax_step_t(_dot(k, q8) + bias_rows(kt, n), mask, v, m_s, l_s, acc_s)

    def run_tiles(lo, hi, step):
        count = hi - lo
        lax.fori_loop(0, count // 2, lambda i, c: (step(lo + 2 * i, 2), c)[1], 0)

        @pl.when(count % 2 == 1)
        def _():
            step(hi - 1, 1)

    _softmax_reset(m_s, l_s, acc_s)
    run_tiles(0, qt, lambda kt, n: sel_step(kt, n, False))
    sel_step(qt, 1, True)
    osel_s[...] = acc_s[...] / l_s[...]

    w_tiles = WINDOW // TQ

    def win_step(kt, n, mask):
        k = kvw_ref[0, pl.ds(kt * TQ, n * TQ), 0:LANES]
        v = kvw_ref[0, pl.ds(kt * TQ, n * TQ), LANES:2 * LANES]
        _softmax_step_t(_dot(k, q8) + bias_rows(kt, n), mask, v, m_s, l_s, acc_s)

    _softmax_reset(m_s, l_s, acc_s)

    @pl.when(qt >= w_tiles)
    def _():
        first = qt - w_tiles
        win_step(first, 1, tpos - (first * TQ + krow) <= WINDOW)

    run_tiles(jnp.maximum(qt - w_tiles + 1, 0), qt, lambda kt, n: win_step(kt, n, None))
    win_step(qt, 1, causal)
    owin = acc_s[...] / l_s[...]

    gat = ga_ref[...].T
    pieces = []
    for h in range(NSA_HEADS):
        g = h // NSA_HPG
        c0, c1 = h * TQ, (h + 1) * TQ
        o = (gat[3 * h:3 * h + 1, :] * ocmp_s[:, c0:c1] + gat[3 * h + 1:3 * h + 2, :] * osel_s[:, c0:c1]
             + gat[3 * h + 2:3 * h + 3, :] * owin[:, c0:c1])
        pieces.append(o[g * HEAD_DIM:(g + 1) * HEAD_DIM, :])
    o_ref[...] = jnp.concatenate(pieces, axis=0).T


def _nsa_prompt(qa, kvn, kvw, ga, bias_c, bias_t, wbig, pe, batch, seq):
    nq = seq // TQ
    n_seg = seq // CMP_STRIDE
    nc = max(LANES, n_seg)
    cols = NSA_HEADS * TQ
    mselt = _cmp_to_sel_matrix(nc, LANES).T
    seg_w = CMP_STRIDE * NSA_GROUPS * HEAD_DIM
    vm = (2 * (seq * 768 * 4 + bias_t.size * 4 + wbig.size * 2 + cols * nc * 4) + 2 * nc * seg_w * 4
          + seq * LANES * 4 + 8 * cols * LANES * 4 + (12 << 20))
    return pl.pallas_call(
        functools.partial(_nsa_prompt_body, seq=seq),
        grid=(batch, nq),
        in_specs=[pl.BlockSpec((TQ, qa.shape[1]), lambda b, q: (b * nq + q, 0)),
                  pl.BlockSpec((1, seq, kvn.shape[2]), lambda b, q: (b, 0, 0)),
                  pl.BlockSpec((1, seq, kvw.shape[2]), lambda b, q: (b, 0, 0)),
                  pl.BlockSpec((TQ, LANES), lambda b, q: (b * nq + q, 0)),
                  pl.BlockSpec((1, nc, cols), lambda b, q: (q, 0, 0)),
                  pl.BlockSpec(bias_t.shape, lambda b, q: (0, 0, 0)),
                  pl.BlockSpec(wbig.shape, lambda b, q: (0, 0, 0, 0)),
                  pl.BlockSpec(pe.shape, lambda b, q: (0, 0, 0, 0)),
                  pl.BlockSpec(mselt.shape, lambda b, q: (0, 0))],
        out_specs=pl.BlockSpec((TQ, qa.shape[1]), lambda b, q: (b * nq + q, 0)),
        out_shape=jax.ShapeDtypeStruct(qa.shape, f32),
        scratch_shapes=[pltpu.VMEM((seq, LANES), f32), pltpu.VMEM((2, nc, seg_w), f32),
                        pltpu.VMEM((nc, LANES), f32), pltpu.VMEM((nc, LANES), f32),
                        pltpu.VMEM((LANES, cols), _MXU), pltpu.VMEM((LANES, cols), f32),
                        pltpu.VMEM((LANES, cols), f32), pltpu.VMEM((LANES, cols), f32),
                        pltpu.VMEM((1, cols), f32), pltpu.VMEM((1, cols), f32), pltpu.VMEM((LANES, cols), f32)],
        compiler_params=_vmem(vm),
        name="nsa_prompt",
    )(qa, kvn, kvw, ga, bias_c, bias_t, wbig, pe, mselt)


def _moba_prompt_body(qb_ref, kvb_ref, bt_ref, o_ref, kmean_s, q8_s, sel8_s, m_s, l_s, acc_s, *, seq):
    qt = pl.program_id(1)
    t0 = qt * TQ
    nb = seq // MOBA_BLK
    nr = _round_up(nb, SUBLANES)
    cols = MOBA_HEADS * TQ
    kw = MOBA_HEADS * HEAD_DIM
    scale = HEAD_DIM ** -0.5
    pairs = MOBA_HEADS // 2

    @pl.when(qt == 0)
    def _():
        kmean_s[...] = jnp.zeros(kmean_s.shape, f32)
        for n in range(nb):
            blk = kvb_ref[0, n * MOBA_BLK:(n + 1) * MOBA_BLK, 0:kw]
            kmean_s[n:n + 1, :] = jnp.sum(blk, axis=0, keepdims=True) * (1.0 / MOBA_BLK)
        sel8_s[...] = jnp.zeros(sel8_s.shape, f32)

    qtr = qb_ref[...].T
    zero = jnp.zeros((HEAD_DIM, TQ), f32)
    nblk = _iota((nr, TQ), 0)
    own = (t0 + _iota((nr, TQ), 1)) >> 8
    past = nblk < own
    for h in range(MOBA_HEADS):
        qh = qtr[h * HEAD_DIM:(h + 1) * HEAD_DIM, :]
        blk = jnp.concatenate([qh, zero] if h % 2 == 0 else [zero, qh], axis=0)
        q8_s[:, h * TQ:(h + 1) * TQ] = (blk * scale).astype(q8_s.dtype)
        gate = _dot_hi(kmean_s[:, (h // 2) * LANES:(h // 2 + 1) * LANES], blk)[0:nr, :]
        score = jnp.where(past, gate, -jnp.inf)
        sel = (past & (_rank_rows(score, nb) < MOBA_TOPK)) | (nblk == own)
        sel8_s[0:nr, h * TQ:(h + 1) * TQ] = sel.astype(f32)

    tpos = t0 + (_iota((1, cols), 1) & (TQ - 1))
    krow = _iota((TQ, cols), 0)

    def step(kt, n, diagonal):
        s_parts = []
        for pr in range(pairs):
            k = kvb_ref[0, pl.ds(kt * TQ, n * TQ), pr * LANES:(pr + 1) * LANES]
            s_parts.append(_dot(k, q8_s[:, 2 * pr * TQ:2 * (pr + 1) * TQ]))
        bias = jnp.concatenate([bt_ref[qt - kt - j] for j in range(n)], axis=0) if n > 1 else bt_ref[qt - kt]
        s = jnp.concatenate(s_parts, axis=1) + bias
        mask = sel8_s[pl.ds((kt * TQ) >> 8, 1), :] > 0.5
        if diagonal:
            mask = mask & (tpos >= t0 + krow)
        s = jnp.where(mask, s, NEG)
        m_old = m_s[...]
        m_new = jnp.maximum(m_old, jnp.max(s, axis=0, keepdims=True))
        alpha = jnp.exp(m_old - m_new)
        p = jnp.exp(s - m_new)
        l_s[...] = alpha * l_s[...] + jnp.sum(p, axis=0, keepdims=True)
        pv = []
        for pr in range(pairs):
            v = kvb_ref[0, pl.ds(kt * TQ, n * TQ), kw + pr * LANES:kw + (pr + 1) * LANES]
            pv.append(_dot_tn(v, p[:, 2 * pr * TQ:2 * (pr + 1) * TQ]))
        acc_s[...] = alpha * acc_s[...] + jnp.concatenate(pv, axis=1)
        m_s[...] = m_new

    _softmax_reset(m_s, l_s, acc_s)
    lax.fori_loop(0, qt // 2, lambda i, c: (step(2 * i, 2, False), c)[1], 0)

    @pl.when(qt % 2 == 1)
    def _():
        step(qt - 1, 1, False)

    step(qt, 1, True)
    o = acc_s[...] / l_s[...]
    pieces = []
    for h in range(MOBA_HEADS):
        half = h % 2
        pieces.append(o[half * HEAD_DIM:(half + 1) * HEAD_DIM, h * TQ:(h + 1) * TQ])
    o_ref[...] = jnp.concatenate(pieces, axis=0).T


def _moba_prompt(qb, kvb, bias_t, batch, seq):
    nq = seq // TQ
    cols = MOBA_HEADS * TQ
    vm = 2 * (seq * kvb.shape[2] * 4 + bias_t.size * 4) + 8 * cols * LANES * 4 + (12 << 20)
    return pl.pallas_call(
        functools.partial(_moba_prompt_body, seq=seq),
        grid=(batch, nq),
        in_specs=[pl.BlockSpec((TQ, qb.shape[1]), lambda b, q: (b * nq + q, 0)),
                  pl.BlockSpec((1, seq, kvb.shape[2]), lambda b, q: (b, 0, 0)),
                  pl.BlockSpec(bias_t.shape, lambda b, q: (0, 0, 0))],
        out_specs=pl.BlockSpec((TQ, qb.shape[1]), lambda b, q: (b * nq + q, 0)),
        out_shape=jax.ShapeDtypeStruct(qb.shape, f32),
        scratch_shapes=[pltpu.VMEM((LANES, MOBA_HEADS * HEAD_DIM), f32),
                        pltpu.VMEM((LANES, cols), _MXU), pltpu.VMEM((LANES, cols), f32),
                        pltpu.VMEM((1, cols), f32), pltpu.VMEM((1, cols), f32), pltpu.VMEM((LANES, cols), f32)],
        compiler_params=_vmem(vm),
        name="moba_prompt",
    )(qb, kvb, bias_t)


def _page_group(n_pages):
    for g in (8, 4, 2):
        if n_pages % g == 0:
            return g
    raise ValueError("the page count must be even")


def _stack_heads_nsa(q, q8_s):
    half = _iota((1, LANES), 1) >> 6
    for h in range(NSA_HEADS):
        g = h // NSA_HPG
        blk = q[:, (h // 2) * LANES:(h // 2 + 1) * LANES]
        if h % 2 != g:
            blk = pltpu.roll(blk, HEAD_DIM, 1)
        q8_s[h:h + 1, :] = jnp.where(half == g, blk, 0.0).astype(q8_s.dtype)


def _softmax_step_vt(s, mask, vt, m_s, l_s, acc_s):
    s = jnp.where(mask, s, NEG)
    m_old = m_s[...]
    m_new = jnp.maximum(m_old, jnp.max(s, axis=1, keepdims=True))
    alpha = jnp.exp(m_old - m_new)
    p = jnp.where(mask, jnp.exp(s - m_new), 0.0)
    l_s[...] = alpha * l_s[...] + jnp.sum(p, axis=1, keepdims=True)
    acc_s[...] = alpha * acc_s[...] + _dot_nt(p, vt)
    m_s[...] = m_new


def _nsa_sample_body(pt_ref, *refs, n_pages, group):
    del pt_ref
    page_refs = refs[:group]
    (qa_ref, kvn_ref, kvw_ref, st_ref, ga_ref, bc_ref, bs_ref, bn_ref, bw_ref, wbig_ref, pe_ref, msel_ref, o_ref,
     rows_s, xflat_s, kc_s, vc_s, q8_s, sel8_s, ocmp_s, m_s, l_s, acc_s) = refs[group:]
    ph = pl.program_id(1)
    p = pl.program_id(2)
    n_steps = n_pages // group
    t = n_pages * PAGE
    seg_per_page = PAGE // CMP_STRIDE
    n_seg = n_pages * seg_per_page
    n_cmp = n_seg - CMP_BLK // CMP_STRIDE + 1
    cur = t // SEL_BLK
    n_sel = cur + 1
    nc = kc_s.shape[0]
    ls = sel8_s.shape[1]
    keys = group * PAGE
    gd = NSA_GROUPS * HEAD_DIM
    scale = HEAD_DIM ** -0.5

    def one_key(q8, row):
        return _dot_nt(q8, jnp.broadcast_to(row, (SUBLANES, LANES)))[:, 0:1]

    @pl.when(ph == 0)
    def _():
        if nc > n_seg:
            @pl.when(p == 0)
            def _():
                xflat_s[...] = jnp.zeros(xflat_s.shape, f32)
        for c in range(2):
            for g, ref in enumerate(page_refs):
                rows_s[...] = ref[0, c].reshape(gd, PAGE).T
                row0 = pl.multiple_of((p * group + g) * seg_per_page, SUBLANES)
                for r in range(CMP_STRIDE):
                    xflat_s[c, pl.ds(row0, seg_per_page), r * LANES:(r + 1) * LANES] = (
                        rows_s[pl.ds(r, seg_per_page, stride=CMP_STRIDE), :])

        @pl.when(p == n_steps - 1)
        def _():
            kc_s[...] = _compress(xflat_s, 0, pe_ref, wbig_ref)
            vc_s[...] = _compress(xflat_s, 1, pe_ref, wbig_ref)
            _stack_heads_nsa(qa_ref[0] * scale, q8_s)
            q8 = q8_s[...]
            ncol = _iota((NSA_HEADS, nc), 1)
            mask_c = ncol < n_cmp
            sc = jnp.where(mask_c, _dot_nt(q8, kc_s[...]) + bc_ref[...], NEG)
            e = jnp.exp(sc - jnp.max(sc, axis=1, keepdims=True))
            p_c = jnp.where(mask_c, e / jnp.sum(e, axis=1, keepdims=True), 0.0)
            ocmp_s[...] = _dot(p_c, vc_s[...])
            same_group = ((_iota((NSA_HEADS, NSA_HEADS), 0) >> 2) == (_iota((NSA_HEADS, NSA_HEADS), 1) >> 2)).astype(f32)
            blk = _dot_hi(_dot_hi(same_group, p_c), msel_ref[...])
            jb = _iota((NSA_HEADS, ls), 1)
            valid = jb < n_sel
            forced = valid & ((jb == 0) | (jb == cur) | (jb == cur - 1))
            score = jnp.where(forced, jnp.inf, jnp.where(valid, blk, -jnp.inf))
            sel8_s[...] = (valid & (_rank_lanes(score, n_sel) < N_SEL)).astype(sel8_s.dtype)

    @pl.when(ph == 1)
    def _():
        q8 = q8_s[...]

        @pl.when(p == 0)
        def _():
            m_s[...] = one_key(q8, kvn_ref[0, :, 2 * LANES:3 * LANES]) + bn_ref[:, 0:1]
            l_s[...] = jnp.ones(l_s.shape, f32)
            v_new = kvn_ref[0, :, 3 * LANES:4 * LANES].astype(_MXU).astype(f32)
            acc_s[...] = jnp.broadcast_to(v_new, acc_s.shape)

        kt = jnp.concatenate([r[0, 0].reshape(gd, PAGE) for r in page_refs], axis=1)
        vt = jnp.concatenate([r[0, 1].reshape(gd, PAGE) for r in page_refs], axis=1)
        s = _dot(q8, kt) + bs_ref[0]
        jrow = _iota((ls, keys), 0)
        jcol = (p * keys + _iota((ls, keys), 1)) >> 6
        chosen = _dot(sel8_s[...], (jrow == jcol).astype(f32)) > 0.5
        _softmax_step_vt(s, chosen, vt, m_s, l_s, acc_s)

        @pl.when(p == n_steps - 1)
        def _():
            osel = acc_s[...] / l_s[...]
            win = st_ref.shape[4]
            sw = _dot(q8, st_ref[0, 0].reshape(gd, win)) + bw_ref[...]
            sn = one_key(q8, kvw_ref[0, :, 0:LANES]) + bn_ref[:, 0:1]
            mw = jnp.maximum(jnp.max(sw, axis=1, keepdims=True), sn)
            ew = jnp.exp(sw - mw)
            en = jnp.exp(sn - mw)
            v_new = kvw_ref[0, :, LANES:2 * LANES].astype(_MXU).astype(f32)
            owin = ((_dot_nt(ew, st_ref[0, 1].reshape(gd, win)) + en * v_new)
                    / (jnp.sum(ew, axis=1, keepdims=True) + en))
            ga = ga_ref[0]
            for h in range(NSA_HEADS):
                g = h // NSA_HPG
                o = (ga[:, 3 * h:3 * h + 1] * ocmp_s[h:h + 1, :] + ga[:, 3 * h + 1:3 * h + 2] * osel[h:h + 1, :]
                     + ga[:, 3 * h + 2:3 * h + 3] * owin[h:h + 1, :])
                o_ref[0, :, h * HEAD_DIM:(h + 1) * HEAD_DIM] = o[:, g * HEAD_DIM:(g + 1) * HEAD_DIM]


def _nsa_sample(page_table, cache_t, qa, kvn, kvw, state_t, ga, tab, wbig, pe):
    bs, n_pages = page_table.shape
    group = _page_group(n_pages)
    n_steps = n_pages // group
    t = n_pages * PAGE
    win = state_t.shape[4]
    n_cmp = n_pages * (PAGE // CMP_STRIDE) - CMP_BLK // CMP_STRIDE + 1
    nc = max(LANES, n_pages * (PAGE // CMP_STRIDE))
    ls = _round_up(t // SEL_BLK + 1, LANES)
    msel = _cmp_to_sel_matrix(nc, ls)
    d_last = t - (CMP_STRIDE * (n_cmp - 1) + CMP_BLK - 1)
    bias_c = tab[:, d_last:t - CMP_BLK + 2:CMP_STRIDE][:, ::-1]
    bias_c = jnp.pad(bias_c, ((0, 0), (0, nc - n_cmp)))
    bias_s = jnp.transpose(tab[:, 1:t + 1][:, ::-1].reshape(NSA_HEADS, n_steps, group * PAGE), (1, 0, 2))
    bias_n = jnp.broadcast_to(tab[:, 0:1], (NSA_HEADS, LANES))
    bias_w = tab[:, 1:win + 1][:, ::-1]
    seg_w = CMP_STRIDE * NSA_GROUPS * HEAD_DIM
    req = lambda b, ph, p, pt: (b, 0, 0)
    c2 = lambda b, ph, p, pt: (0, 0)
    c4 = lambda b, ph, p, pt: (0, 0, 0, 0)

    def page_spec(g):
        return pl.BlockSpec((1, 2, NSA_GROUPS, HEAD_DIM, PAGE),
                            lambda b, ph, p, pt: (pt[b, p * group + g], ph, 0, 0, 0))

    grid_spec = pltpu.PrefetchScalarGridSpec(
        num_scalar_prefetch=1,
        grid=(bs, 2, n_steps),
        in_specs=([page_spec(g) for g in range(group)]
                  + [pl.BlockSpec((1, 1, qa.shape[2]), req), pl.BlockSpec((1, 1, kvn.shape[2]), req),
                     pl.BlockSpec((1, 1, kvw.shape[2]), req),
                     pl.BlockSpec((1,) + state_t.shape[1:], lambda b, ph, p, pt: (b, 0, 0, 0, 0)),
                     pl.BlockSpec((1, 1, LANES), req),
                     pl.BlockSpec(bias_c.shape, c2),
                     pl.BlockSpec((1, NSA_HEADS, group * PAGE), lambda b, ph, p, pt: (p * ph, 0, 0)),
                     pl.BlockSpec(bias_n.shape, c2), pl.BlockSpec(bias_w.shape, c2),
                     pl.BlockSpec(wbig.shape, c4), pl.BlockSpec(pe.shape, c4), pl.BlockSpec(msel.shape, c2)]),
        out_specs=pl.BlockSpec((1, 1, qa.shape[2]), req),
        scratch_shapes=[pltpu.VMEM((PAGE, LANES), f32),
                        pltpu.VMEM((2, nc, seg_w), f32), pltpu.VMEM((nc, LANES), f32), pltpu.VMEM((nc, LANES), f32),
                        pltpu.VMEM((NSA_HEADS, LANES), _MXU), pltpu.VMEM((NSA_HEADS, ls), _MXU),
                        pltpu.VMEM((NSA_HEADS, LANES), f32),
                        pltpu.VMEM((NSA_HEADS, 1), f32), pltpu.VMEM((NSA_HEADS, 1), f32),
                        pltpu.VMEM((NSA_HEADS, LANES), f32)])
    vm = (2 * nc * seg_w * 4 + 2 * (wbig.size * 2 + msel.size * 4 + win * 256 * 4 + 2 * group * PAGE * LANES * 4)
          + (16 << 20))
    return pl.pallas_call(
        functools.partial(_nsa_sample_body, n_pages=n_pages, group=group),
        grid_spec=grid_spec,
        out_shape=jax.ShapeDtypeStruct(qa.shape, f32),
        compiler_params=_vmem(vm),
        name="nsa_sample",
    )(page_table, *([cache_t] * group), qa, kvn, kvw, state_t, ga, bias_c, bias_s, bias_n, bias_w, wbig, pe, msel)


def _moba_sample_body(pt_ref, *refs, n_pages, group):
    del pt_ref
    page_refs = refs[:group]
    qb_ref, kvb_ref, bs_ref, bn_ref, o_ref, gate_s, mx_s, sum_s, acc_s = refs[group:]
    p = pl.program_id(1)
    n_steps = n_pages // group
    t = n_pages * PAGE
    kw = MOBA_HEADS * HEAD_DIM
    pages_per_blk = MOBA_BLK // PAGE
    blks_per_step = group // pages_per_blk
    own = t // MOBA_BLK
    scale = HEAD_DIM ** -0.5
    head_lane = (_iota((MOBA_HEADS, kw), 1) >> 6) == _iota((MOBA_HEADS, kw), 0)
    q8 = jnp.where(head_lane, jnp.broadcast_to(qb_ref[0], (MOBA_HEADS, kw)), 0.0)
    lane = _iota((MOBA_HEADS, LANES), 1)

    @pl.when(p == 0)
    def _():
        gate_s[...] = jnp.zeros(gate_s.shape, f32)
        mx_s[...] = jnp.full(mx_s.shape, NEG, f32)
        sum_s[...] = jnp.zeros(sum_s.shape, f32)

    for j in range(blks_per_step):
        blk = p * blks_per_step + j
        pages = page_refs[j * pages_per_blk:(j + 1) * pages_per_blk]
        kt = jnp.concatenate([r[0, 0].reshape(kw, PAGE) for r in pages], axis=1)
        vt = jnp.concatenate([r[0, 1].reshape(kw, PAGE) for r in pages], axis=1)
        raw = _dot_hi(q8, kt)
        s = raw * scale + bs_ref[0, :, j * MOBA_BLK:(j + 1) * MOBA_BLK]
        m = jnp.max(s, axis=1, keepdims=True)
        pr = jnp.exp(s - m)
        here = lane == blk
        gate_s[...] = jnp.where(here, jnp.sum(raw, axis=1, keepdims=True) * (1.0 / MOBA_BLK), gate_s[...])
        mx_s[...] = jnp.where(here, m, mx_s[...])
        sum_s[...] = jnp.where(here, jnp.sum(pr, axis=1, keepdims=True), sum_s[...])
        acc_s[blk] = _dot_nt(pr, vt)

    @pl.when(p == n_steps - 1)
    def _():
        past = lane < own
        score = jnp.where(past, gate_s[...], -jnp.inf)
        sel = past & (_rank_lanes(score, own) < MOBA_TOPK)
        s_new = jnp.sum(q8 * kvb_ref[0, :, 0:kw], axis=1, keepdims=True) * scale + bn_ref[:, 0:1]
        v_new = kvb_ref[0, :, kw:2 * kw].astype(_MXU).astype(f32)
        top = jnp.maximum(jnp.max(jnp.where(sel, mx_s[...], NEG), axis=1, keepdims=True), s_new)
        wts = jnp.where(sel, jnp.exp(mx_s[...] - top), 0.0)
        w_new = jnp.exp(s_new - top)
        den = jnp.sum(wts * sum_s[...], axis=1, keepdims=True) + w_new
        num = w_new * v_new
        for j in range(own):
            num = num + wts[:, j:j + 1] * acc_s[j]
        o8 = jnp.where(head_lane, num / den, 0.0)
        o_ref[0] = jnp.sum(o8, axis=0, keepdims=True)


def _moba_sample(page_table, cache_t, qb, kvb, tab):
    bs, n_pages = page_table.shape
    group = _page_group(n_pages)
    n_steps = n_pages // group
    t = n_pages * PAGE
    kw = MOBA_HEADS * HEAD_DIM
    n_blk = t // MOBA_BLK
    assert n_blk <= LANES
    bias_s = jnp.transpose(tab[:, 1:t + 1][:, ::-1].reshape(MOBA_HEADS, n_steps, group * PAGE), (1, 0, 2))
    bias_n = jnp.broadcast_to(tab[:, 0:1], (MOBA_HEADS, LANES))
    req = lambda b, p, pt: (b, 0, 0)

    def page_spec(g):
        return pl.BlockSpec((1, 2, MOBA_HEADS, HEAD_DIM, PAGE), lambda b, p, pt: (pt[b, p * group + g], 0, 0, 0, 0))

    grid_spec = pltpu.PrefetchScalarGridSpec(
        num_scalar_prefetch=1,
        grid=(bs, n_steps),
        in_specs=([page_spec(g) for g in range(group)]
                  + [pl.BlockSpec((1, 1, kw), req), pl.BlockSpec((1, 1, 2 * kw), req),
                     pl.BlockSpec((1, MOBA_HEADS, group * PAGE), lambda b, p, pt: (p, 0, 0)),
                     pl.BlockSpec(bias_n.shape, lambda b, p, pt: (0, 0))]),
        out_specs=pl.BlockSpec((1, 1, kw), req),
        scratch_shapes=[pltpu.VMEM((MOBA_HEADS, LANES), f32), pltpu.VMEM((MOBA_HEADS, LANES), f32),
                        pltpu.VMEM((MOBA_HEADS, LANES), f32), pltpu.VMEM((n_blk, MOBA_HEADS, kw), f32)])
    return pl.pallas_call(
        functools.partial(_moba_sample_body, n_pages=n_pages, group=group),
        grid_spec=grid_spec,
        out_shape=jax.ShapeDtypeStruct(qb.shape, f32),
        compiler_params=_vmem(2 * group * PAGE * 2 * kw * 4 + (24 << 20)),
        name="moba_sample",
    )(page_table, *([cache_t] * group), qb, kvb, bias_s, bias_n)


def _layer_norm(x, g, b):
    mu = jnp.mean(x, axis=-1, keepdims=True)
    xc = x - mu
    var = jnp.mean(xc * xc, axis=-1, keepdims=True)
    return xc * lax.rsqrt(var + LN_EPS) * g + b


def _merge_body(oa_ref, ob_ref, gm_ref, x_ref, wa_ref, wb_ref, wo_ref, g_ref, b_ref, h_ref, ht_ref, *, alpha):
    d = x_ref.shape[1]
    ya = _dot(oa_ref[...], wa_ref[...])
    yb = _dot(ob_ref[...], wb_ref[...])
    gm = gm_ref[...]
    mix = _dot(gm[:, :d] * ya + gm[:, d:] * yb, wo_ref[...])
    h = _layer_norm(alpha * x_ref[...] + mix, g_ref[...], b_ref[...])
    h_ref[...] = h
    ht_ref[...] = h.T.astype(ht_ref.dtype)


def _merge(oa, ob, gm, x2d, wa, wb, wo, g, b, alpha):
    n, d = x2d.shape
    tm = 256 if n % 256 == 0 else n
    row = lambda i: (i, 0)
    full = lambda i: (0, 0)
    vm = 2 * (tm * (oa.shape[1] + ob.shape[1] + gm.shape[1] + 2 * d) * 4 + tm * d * 2
              + (wa.size + wb.size + wo.size) * 2) + (16 << 20)
    return pl.pallas_call(
        functools.partial(_merge_body, alpha=alpha),
        grid=(n // tm,),
        in_specs=[pl.BlockSpec((tm, oa.shape[1]), row), pl.BlockSpec((tm, ob.shape[1]), row),
                  pl.BlockSpec((tm, gm.shape[1]), row), pl.BlockSpec((tm, d), row),
                  pl.BlockSpec(wa.shape, full), pl.BlockSpec(wb.shape, full), pl.BlockSpec(wo.shape, full),
                  pl.BlockSpec((1, d), full), pl.BlockSpec((1, d), full)],
        out_specs=[pl.BlockSpec((tm, d), row), pl.BlockSpec((d, tm), lambda i: (0, i))],
        out_shape=[jax.ShapeDtypeStruct((n, d), f32), jax.ShapeDtypeStruct((d, n), _MXU)],
        compiler_params=_vmem(vm),
        name="merge",
    )(oa, ob, gm, x2d, wa, wb, wo, g, b)


_CAND_B = tuple(PEER_TOPK // (a + 1) for a in range(SUBLANES))
_TAKEN = -3.0e38
_ROUND_SCALE = 1024.0


def _is_taken(w):
    return w < 0.5 * _TAKEN


def _taken_round(w):
    return jnp.round((w - _TAKEN) * (_ROUND_SCALE / -_TAKEN))


def _extract_top(work_s, n_rounds, on_round, one_per_round):
    rows = _iota(work_s.shape, 0)

    def body(r, carry):
        w = work_s[...]
        m = jnp.max(w, axis=0, keepdims=True)
        hit = w == m
        if one_per_round:
            hit = rows == jnp.min(jnp.where(hit, rows, work_s.shape[0]), axis=0, keepdims=True)
        on_round(r, m)
        work_s[...] = jnp.where(hit, _TAKEN * (1.0 - lax.convert_element_type(r, f32) * (1.0 / _ROUND_SCALE)), w)
        return carry

    lax.fori_loop(0, n_rounds, body, 0)


def _top_rounds(work_s, init, on_round):
    init()
    _extract_top(work_s, PEER_TOPK, on_round, one_per_round=False)
    n_taken = jnp.sum(jnp.where(_is_taken(work_s[...]), 1.0, 0.0), axis=0, keepdims=True)
    bad = jnp.max(jnp.abs(n_taken - float(PEER_TOPK)))

    @pl.when(bad > 0.5)
    def _():
        init()
        _extract_top(work_s, PEER_TOPK, on_round, one_per_round=True)


def _peer_select(ht_ref, wqt_ref, keys_ref, qt_s, rank_s, exp_s, sv_s, work_s, cwork_s, cnt_s, f0_s,
                 rk1_s, e1_s):
    tn = ht_ref.shape[1]
    nk = PEER_N_KEYS
    qt_s[...] = jnp.dot(wqt_ref[...], ht_ref[...], preferred_element_type=f32)

    def per_half(hc, carry):
        exp_s[hc] = _dot(keys_ref[hc], qt_s[pl.ds(pl.multiple_of(hc * PEER_HALF, PEER_HALF), PEER_HALF), :])

        def init():
            work_s[...] = exp_s[hc]

        def on_round(r, m):
            sv_s[hc, pl.ds(r, 1), :] = m

        _top_rounds(work_s, init, on_round)
        rank_s[hc] = _taken_round(work_s[...])
        exp_s[hc] = jnp.exp(exp_s[hc] - sv_s[hc, 0:1, :])
        return carry

    lax.fori_loop(0, 2 * PEER_HEADS, per_half, 0)

    n_cand = cwork_s.shape[0]

    def per_head(h, carry):
        sv0 = sv_s[2 * h]
        sv1 = sv_s[2 * h + 1]
        e0 = jnp.exp(sv0 - sv0[0:1, :])
        e1 = jnp.exp(sv1 - sv1[0:1, :])
        brow = _iota((SUBLANES, tn), 0)
        tiles, etiles = [sv0[0:1, :] + sv1, ], [e0[0:1, :] * e1]
        for a in range(1, SUBLANES):
            ok = brow < _CAND_B[a]
            tiles.append(jnp.where(ok, sv0[a:a + 1, :] + sv1[0:SUBLANES, :], NEG))
            etiles.append(jnp.where(ok, e0[a:a + 1, :] * e1[0:SUBLANES, :], 0.0))
        tiles.append(sv0[SUBLANES:, :] + sv1[0:1, :])
        etiles.append(e0[SUBLANES:, :] * e1[0:1, :])
        cands = jnp.concatenate(tiles, axis=0)

        def init():
            cwork_s[...] = cands

        _top_rounds(cwork_s, init, lambda r, m: None)
        chosen = jnp.where(_is_taken(cwork_s[...]), 1.0, 0.0)
        z = jnp.sum(chosen * jnp.concatenate(etiles, axis=0), axis=0, keepdims=True)
        cnts = [jnp.sum(chosen[0:PEER_TOPK], axis=0, keepdims=True)]
        for a in range(1, SUBLANES):
            r0 = PEER_TOPK + (a - 1) * SUBLANES
            cnts.append(jnp.sum(chosen[r0:r0 + SUBLANES], axis=0, keepdims=True))
        tail = chosen[n_cand - SUBLANES:]
        cnts += [tail[a:a + 1, :] for a in range(SUBLANES)]
        ra = rank_s[2 * h]
        c0 = jnp.zeros((nk, tn), f32)
        for a in range(PEER_TOPK):
            c0 = jnp.where(ra == float(a), cnts[a], c0)
        cnt_s[h] = c0
        f0_s[h] = exp_s[2 * h] / z
        rk1_s[h] = rank_s[2 * h + 1].astype(rk1_s.dtype)
        e1_s[h] = exp_s[2 * h + 1].astype(e1_s.dtype)
        return carry

    lax.fori_loop(0, PEER_HEADS, per_head, 0)


def _gelu_t(u_tile, ht):
    at = jnp.dot(u_tile, ht, preferred_element_type=f32)
    return (0.5 * at * (1.0 + lax.erf(at * math.sqrt(0.5)))).astype(_MXU)


def _peer_body(ht_ref, h_ref, wqt_ref, keys_ref, u0_ref, un_ref, vt_ref, g_ref, b_ref, y_ref,
               qt_s, rank_s, exp_s, sv_s, work_s, cwork_s, cnt_s, f0_s, rk1_s, e1_s, gel_s, wc_s, acc_s,
               *, alpha, ti):
    e = pl.program_id(1)
    last = pl.num_programs(1) - 1
    tn = ht_ref.shape[1]

    @pl.when(e == 0)
    def _():
        _peer_select(ht_ref, wqt_ref, keys_ref, qt_s, rank_s, exp_s, sv_s, work_s, cwork_s, cnt_s, f0_s,
                     rk1_s, e1_s)
        acc_s[...] = jnp.zeros(acc_s.shape, f32)
        gel_s[0] = _gelu_t(u0_ref[...], ht_ref[...])

    for ii in range(ti):
        i = e * ti + ii
        r0, r1 = ii * PEER_N_KEYS, (ii + 1) * PEER_N_KEYS
        w = jnp.zeros((PEER_N_KEYS, tn), _MXU)
        for h in range(PEER_HEADS):
            cnt = cnt_s[h, pl.ds(i, 1), :].astype(_MXU)
            f0 = f0_s[h, pl.ds(i, 1), :].astype(_MXU)
            w = w + jnp.where(rk1_s[h] < cnt, e1_s[h], jnp.zeros((), _MXU)) * f0
        wc_s[r0:r1, :] = w * gel_s[e % 2, r0:r1, :]
        if ii % 2 == 1:
            k0, k1 = (ii - 1) * PEER_N_KEYS, (ii + 1) * PEER_N_KEYS
            acc_s[...] += jnp.dot(vt_ref[:, k0:k1], wc_s[k0:k1, :], preferred_element_type=f32)
    gel_s[(e + 1) % 2] = _gelu_t(un_ref[...], ht_ref[...])

    @pl.when(e == last)
    def _():
        y_ref[...] = _layer_norm(alpha * h_ref[...] + acc_s[...].T, g_ref[...], b_ref[...])


def _peer(ht, h, wqt, keys, u, vt, g, b, alpha):
    d, n = ht.shape
    n_exp = u.shape[0]
    tn = 512 if n % 512 == 0 else n
    ti = 8
    te = ti * PEER_N_KEYS
    n_half = 2 * PEER_HEADS
    n_cand = PEER_TOPK + (SUBLANES - 1) * SUBLANES + SUBLANES
    tok = lambda t, e: (0, t)
    full2 = lambda t, e: (0, 0)
    scratch = [pltpu.VMEM((wqt.shape[0], tn), f32),
               pltpu.VMEM((n_half, PEER_N_KEYS, tn), f32), pltpu.VMEM((n_half, PEER_N_KEYS, tn), f32),
               pltpu.VMEM((n_half, PEER_TOPK, tn), f32), pltpu.VMEM((PEER_N_KEYS, tn), f32),
               pltpu.VMEM((n_cand, tn), f32),
               pltpu.VMEM((PEER_HEADS, PEER_N_KEYS, tn), f32), pltpu.VMEM((PEER_HEADS, PEER_N_KEYS, tn), f32),
               pltpu.VMEM((PEER_HEADS, PEER_N_KEYS, tn), _MXU), pltpu.VMEM((PEER_HEADS, PEER_N_KEYS, tn), _MXU),
               pltpu.VMEM((2, te, tn), _MXU), pltpu.VMEM((te, tn), _MXU), pltpu.VMEM((d, tn), f32)]
    vm = (wqt.shape[0] * tn * 4 + 2 * n_half * PEER_N_KEYS * tn * 4 + 2 * PEER_HEADS * PEER_N_KEYS * tn * 6
          + 3 * te * tn * 2 + d * tn * 4
          + 2 * (d * tn * 2 + 2 * tn * d * 4 + wqt.size * 2 + keys.size * 2 + 3 * te * d * 2) + (14 << 20))
    n_tiles = n_exp // te
    return pl.pallas_call(
        functools.partial(_peer_body, alpha=alpha, ti=ti),
        grid=(n // tn, n_tiles),
        in_specs=[pl.BlockSpec((d, tn), tok),
                  pl.BlockSpec((tn, d), lambda t, e: (t, 0)),
                  pl.BlockSpec(wqt.shape, full2),
                  pl.BlockSpec(keys.shape, lambda t, e: (0, 0, 0)),
                  pl.BlockSpec((te, d), full2),
                  pl.BlockSpec((te, d), lambda t, e: (jnp.minimum(e + 1, n_tiles - 1), 0)),
                  pl.BlockSpec((d, te), lambda t, e: (0, e)),
                  pl.BlockSpec((1, d), full2), pl.BlockSpec((1, d), full2)],
        out_specs=pl.BlockSpec((tn, d), lambda t, e: (t, 0)),
        out_shape=jax.ShapeDtypeStruct((n, d), f32),
        scratch_shapes=scratch,
        compiler_params=_vmem(vm),
        name="peer",
    )(ht, h, wqt, keys, u, u, vt, g, b)


def kernel(x_prompt, x_sample, cache_nsa, cache_moba, state_nsa_win, page_table, w_in, nsa_cmp_pe, nsa_cmp_w, w_branch_a, w_branch_b, w_out, ln1_g, ln1_b, peer_wq, peer_keys, peer_u, peer_v, ln2_g, ln2_b, rel_bias):
    depth = w_in.shape[0]
    batch, seq, d = x_prompt.shape
    bs, dec_seq, _ = x_sample.shape
    n_pages = page_table.shape[1]
    past = n_pages * PAGE
    win_len = state_nsa_win.shape[2]
    assert dec_seq == 1 and cache_nsa.shape[2] == PAGE and cache_moba.shape[2] == PAGE
    assert seq % MOBA_BLK == 0 and past % MOBA_BLK == 0 and win_len == WINDOW and seq % TQ == 0
    alpha = (2 * depth) ** 0.25
    nq = seq // TQ
    n_tok = batch * seq

    tab = _bias_by_distance(rel_bias, max(seq, past + 1) + TQ)
    tab_a, tab_b = tab[:NSA_HEADS], tab[NSA_HEADS:]
    tiles_a, tiles_b = _bias_tiles_t(tab_a, nq), _bias_tiles_t(tab_b, nq)
    cmp_bias = _nsa_cmp_bias_t(tab_a, seq, max(LANES, seq // CMP_STRIDE))

    xp = x_prompt.reshape(n_tok, d)
    xs = x_sample.reshape(bs, d)
    outs = [[] for _ in range(6)]
    for l in range(depth):
        w_packed = _pack_w_in(w_in[l], d)
        wbig, pe = _pack_cmp(nsa_cmp_pe[l], nsa_cmp_w[l])
        wa, wb, wo = w_branch_a[l].astype(_MXU), w_branch_b[l].astype(_MXU), w_out[l].astype(_MXU)
        wqt = peer_wq[l].T.astype(_MXU)
        keys = peer_keys[l].reshape(2 * PEER_HEADS, PEER_N_KEYS, PEER_HALF).astype(_MXU)
        u = peer_u[l].astype(_MXU)
        vt = peer_v[l].T.astype(_MXU)
        g1, b1, g2, b2 = ln1_g[l][None], ln1_b[l][None], ln2_g[l][None], ln2_b[l][None]

        def ffn(oa, ob, gm, x2d):
            h, ht = _merge(oa, ob, gm, x2d, wa, wb, wo, g1, b1, alpha)
            return _peer(ht, h, wqt, keys, u, vt, g2, b2, alpha)

        qa, kvn, kvw, qb, kvb, gm, ga, kvn_t, kvw_t, kvb_t = _project(xp, w_packed, d, seq)
        oa = _nsa_prompt(qa, kvn.reshape(batch, seq, -1), kvw.reshape(batch, seq, -1), ga, cmp_bias, tiles_a,
                         wbig, pe, batch, seq)
        ob = _moba_prompt(qb, kvb.reshape(batch, seq, -1), tiles_b, batch, seq)
        xp = ffn(oa, ob, gm, xp)
        to_row_major = lambda a, c, g: jnp.transpose(a.reshape(batch, c, g, HEAD_DIM, -1), (0, 4, 1, 2, 3))
        outs[0].append(to_row_major(kvn_t, 4, NSA_GROUPS))
        outs[1].append(to_row_major(kvb_t, 2, MOBA_HEADS))
        outs[2].append(to_row_major(kvw_t[:, :, seq - min(WINDOW, seq):], 2, NSA_GROUPS))

        qa, kvn, kvw, qb, kvb, gm, ga = _project(xs, w_packed, d)
        r3 = lambda a: a.reshape(bs, 1, -1)
        to_feature_major = lambda a: jnp.transpose(a, (0, 2, 3, 4, 1))
        oa = _nsa_sample(page_table, to_feature_major(cache_nsa[l]), r3(qa), r3(kvn), r3(kvw),
                         to_feature_major(state_nsa_win[l]), r3(ga), tab_a, wbig, pe)
        ob = _moba_sample(page_table, to_feature_major(cache_moba[l]), r3(qb), r3(kvb), tab_b)
        xs = ffn(oa.reshape(bs, -1), ob.reshape(bs, -1), gm, xs)
        outs[3].append(kvn.reshape(bs, 1, 4, NSA_GROUPS, HEAD_DIM))
        outs[4].append(kvb.reshape(bs, 1, 2, MOBA_HEADS, HEAD_DIM))
        win_new = jnp.concatenate([state_nsa_win[l][:, 1:], kvw.reshape(bs, 1, 2, NSA_GROUPS, HEAD_DIM)], axis=1)
        outs[5].append(win_new)

    stacked = [jnp.stack(o) for o in outs]
    return (xp.reshape(batch, seq, d), xs.reshape(bs, 1, d), *stacked)
```
